```python
import jax, jax.numpy as jnp
from jax import lax
import numpy as np


D_MODEL = 1024
BATCH = 2
SEQ = 8192
DEPTH = 2

N_MIXERS = 2
ATT_HEADS = 8
ATT_HEAD_DIM = D_MODEL // ATT_HEADS
MOBA_BLOCK = 256
MOBA_TOPK = 3
MOBA_QUERY_CHUNK = 32
MLSTM_HEADS = 8
MLSTM_DV = D_MODEL // MLSTM_HEADS
MLSTM_DQK = MLSTM_DV // 2
MLSTM_CHUNK = 64
GATE_SOFTCAP = 15.0
D_FF = 4 * D_MODEL
RMS_EPS = 1e-6
N_ATTN_LAYERS = (DEPTH + N_MIXERS - 1) // N_MIXERS
N_MLSTM_LAYERS = DEPTH // N_MIXERS

kernel_name = "hybrid_moba_mlstm_sandwich_block"


def rms_norm(x, gain):
    xf = x.astype(jnp.float32)
    y = xf * lax.rsqrt(jnp.mean(xf * xf, axis=-1, keepdims=True) + RMS_EPS)
    return (y * gain.astype(jnp.float32)).astype(x.dtype)


def alibi_slopes(n_heads):
    return jnp.exp2(-8.0 * jnp.arange(1, n_heads + 1, dtype=jnp.float32) / n_heads)


def soft_cap(z):
    return GATE_SOFTCAP * jnp.tanh(z / GATE_SOFTCAP)


def moba_attention(x, w_qkv, w_o):
    bsz, seq, _ = x.shape
    H, dh, blk, qlen = ATT_HEADS, ATT_HEAD_DIM, MOBA_BLOCK, MOBA_QUERY_CHUNK
    f32 = jnp.float32
    q, k, v = jnp.split(x @ w_qkv, 3, axis=-1)

    def to_heads(t):
        return t.reshape(bsz, seq, H, dh).transpose(0, 2, 1, 3)

    q = to_heads(q) * (dh ** -0.5)
    k = to_heads(k)
    v = to_heads(v)
    n_blk = -(-seq // blk)
    pad = n_blk * blk - seq
    k_blk = jnp.pad(k, ((0, 0), (0, 0), (0, pad), (0, 0))).reshape(bsz, H, n_blk, blk, dh)
    v_blk = jnp.pad(v, ((0, 0), (0, 0), (0, pad), (0, 0))).reshape(bsz, H, n_blk, blk, dh)

    k_mean = jnp.mean(k_blk.astype(f32), axis=3)
    gate = jnp.einsum('bhsd,bhnd->bhsn', q.astype(f32), k_mean)
    n_past = jnp.arange(seq) // blk
    is_past = jnp.arange(n_blk)[None, :] < n_past[:, None]
    gate = jnp.where(is_past, gate, -jnp.inf)
    top = min(MOBA_TOPK, n_blk)
    _, sel = lax.top_k(gate, top)
    sel = sel.astype(jnp.int32)

    n_chunk = seq // qlen
    q_c = jnp.moveaxis(q.reshape(bsz, H, n_chunk, qlen, dh), 2, 0)
    sel_c = jnp.moveaxis(sel.reshape(bsz, H, n_chunk, qlen, top), 2, 0)
    slopes = alibi_slopes(H)
    b_idx = jnp.arange(bsz)[:, None, None, None]
    h_idx = jnp.arange(H)[None, :, None, None]
    offs = jnp.arange(blk)

    def attend_chunk(args):
        c, q_blk, sel_blk = args
        t = c * qlen + jnp.arange(qlen)
        own = (c * qlen) // blk
        k_own = lax.dynamic_index_in_dim(k_blk, own, axis=2, keepdims=False)
        v_own = lax.dynamic_index_in_dim(v_blk, own, axis=2, keepdims=False)
        dist_own = (t[:, None] - (own * blk + offs)[None, :]).astype(f32)
        s_own = jnp.einsum('bhqd,bhkd->bhqk', q_blk, k_own).astype(f32) - slopes[:, None, None] * dist_own
        s_own = jnp.where(dist_own >= 0, s_own, -jnp.inf)
        k_sel = k_blk[b_idx, h_idx, sel_blk]
        v_sel = v_blk[b_idx, h_idx, sel_blk]
        dist_sel = (t[:, None, None] - (sel_blk[..., None] * blk + offs)).astype(f32)
        s_sel = jnp.einsum('bhqd,bhqjkd->bhqjk', q_blk, k_sel).astype(f32) - slopes[:, None, None, None] * dist_sel
        slot_ok = jnp.arange(top)[None, :] < (t // blk)[:, None]
        s_sel = jnp.where(slot_ok[:, :, None], s_sel, -jnp.inf)
        scores = jnp.concatenate([s_own, s_sel.reshape(bsz, H, qlen, top * blk)], axis=-1)
        p = jax.nn.softmax(scores, axis=-1).astype(v_blk.dtype)
        p_own = p[..., :blk]
        p_sel = p[..., blk:].reshape(bsz, H, qlen, top, blk)
        return (jnp.einsum('bhqk,bhkd->bhqd', p_own, v_own)
                + jnp.einsum('bhqjk,bhqjkd->bhqd', p_sel, v_sel))

    o = lax.map(attend_chunk, (jnp.arange(n_chunk), q_c, sel_c))
    o = o.transpose(1, 0, 3, 2, 4).reshape(bsz, seq, H * dh)
    return o @ w_o


def mlstm_chunkwise(q, k, v, log_i, log_f):
    bsz, H, seq, dqk = q.shape
    dv = v.shape[-1]
    L = MLSTM_CHUNK
    n_chunk = seq // L

    def chunks(t):
        return jnp.moveaxis(t.reshape(bsz, H, n_chunk, L, *t.shape[3:]), 2, 0)

    causal = jnp.tril(jnp.ones((L, L), dtype=bool))

    def step(carry, inp):
        C, n, m = carry
        qc, kc, vc, ic, fc = inp
        b = jnp.cumsum(fc, axis=-1)
        log_d = jnp.where(causal, b[..., :, None] - b[..., None, :] + ic[..., None, :], -jnp.inf)
        log_inter = b + m[..., None]
        m_t = jnp.maximum(log_inter, jnp.max(log_d, axis=-1))
        d = jnp.exp(log_d - m_t[..., None])
        w_inter = jnp.exp(log_inter - m_t)
        s = jnp.einsum('bhtd,bhsd->bhts', qc, kc) * d
        num = (w_inter[..., None] * jnp.einsum('bhtd,bhdv->bhtv', qc, C)
               + jnp.einsum('bhts,bhsv->bhtv', s, vc))
        den = w_inter * jnp.einsum('bhtd,bhd->bht', qc, n) + jnp.sum(s, axis=-1)
        h = num / jnp.maximum(jnp.abs(den), jnp.exp(-m_t))[..., None]
        b_end = b[..., -1]
        log_w = b_end[..., None] - b + ic
        m_new = jnp.maximum(b_end + m, jnp.max(log_w, axis=-1))
        decay = jnp.exp(b_end + m - m_new)
        w = jnp.exp(log_w - m_new[..., None])
        C = decay[..., None, None] * C + jnp.einsum('bhs,bhsd,bhsv->bhdv', w, kc, vc)
        n = decay[..., None] * n + jnp.einsum('bhs,bhsd->bhd', w, kc)
        return (C, n, m_new), h

    f32 = jnp.float32
    init = (jnp.zeros((bsz, H, dqk, dv), f32), jnp.zeros((bsz, H, dqk), f32), jnp.zeros((bsz, H), f32))
    _, h = lax.scan(step, init, (chunks(q), chunks(k), chunks(v), chunks(log_i), chunks(log_f)))
    return jnp.moveaxis(h, 0, 2).reshape(bsz, H, seq, dv)


def mlstm_mixer(x, w_in, b_gates, norm_h, w_out):
    bsz, seq, _ = x.shape
    H, dqk, dv = MLSTM_HEADS, MLSTM_DQK, MLSTM_DV
    f32 = jnp.float32
    proj = x @ w_in
    q, k, v, o_pre, gate_pre = jnp.split(
        proj, [H * dqk, 2 * H * dqk, 2 * H * dqk + H * dv, 2 * H * dqk + 2 * H * dv], axis=-1)

    def to_heads(t, dim):
        return t.reshape(bsz, seq, H, dim).transpose(0, 2, 1, 3).astype(f32)

    q = to_heads(q, dqk)
    k = to_heads(k, dqk) * (dqk ** -0.5)
    v = to_heads(v, dv)
    gates = soft_cap(gate_pre.astype(f32) + b_gates.astype(f32)).transpose(0, 2, 1)
    log_i = gates[:, :H]
    log_f = jax.nn.log_sigmoid(gates[:, H:])
    h = mlstm_chunkwise(q, k, v, log_i, log_f)
    h = h * lax.rsqrt(jnp.mean(h * h, axis=-1, keepdims=True) + RMS_EPS)
    h = h.transpose(0, 2, 1, 3).reshape(bsz, seq, H * dv) * norm_h.astype(f32)
    y = (jax.nn.sigmoid(o_pre.astype(f32)) * h).astype(x.dtype)
    return y @ w_out


def squared_relu_mlp(x, w_up, w_down):
    return jnp.square(jax.nn.relu(x @ w_up)) @ w_down


def setup_inputs(seed: int = 0) -> dict:
    key = jax.random.key(seed)
    ks = jax.random.split(key, 16)
    f32 = jnp.float32

    def dense(k, shape):
        return jax.random.normal(k, shape, f32) * shape[-2] ** -0.5

    def gain(k, shape):
        return 1.0 + 0.05 * jax.random.normal(k, shape, f32)

    H = MLSTM_HEADS
    mlstm_in = 2 * H * MLSTM_DQK + 2 * H * MLSTM_DV + 2 * H
    i_bias = 0.1 * jax.random.normal(ks[13], (N_MLSTM_LAYERS, H), f32)
    f_bias = 3.0 + 0.5 * jax.random.normal(ks[14], (N_MLSTM_LAYERS, H), f32)
    return {
        "x": jax.random.normal(ks[0], (BATCH, SEQ, D_MODEL), f32),
        "norm_mix_pre": gain(ks[1], (DEPTH, D_MODEL)),
        "norm_mix_post": gain(ks[2], (DEPTH, D_MODEL)),
        "norm_ffn_pre": gain(ks[3], (DEPTH, D_MODEL)),
        "norm_ffn_post": gain(ks[4], (DEPTH, D_MODEL)),
        "w_up": dense(ks[5], (DEPTH, D_MODEL, D_FF)),
        "w_down": dense(ks[6], (DEPTH, D_FF, D_MODEL)),
        "attn_w_qkv": dense(ks[7], (N_ATTN_LAYERS, D_MODEL, 3 * ATT_HEADS * ATT_HEAD_DIM)),
        "attn_w_o": dense(ks[8], (N_ATTN_LAYERS, ATT_HEADS * ATT_HEAD_DIM, D_MODEL)),
        "mlstm_w_in": dense(ks[9], (N_MLSTM_LAYERS, D_MODEL, mlstm_in)),
        "mlstm_b_gates": jnp.concatenate([i_bias, f_bias], axis=-1),
        "mlstm_norm_h": gain(ks[10], (N_MLSTM_LAYERS, H * MLSTM_DV)),
        "mlstm_w_out": dense(ks[11], (N_MLSTM_LAYERS, H * MLSTM_DV, D_MODEL)),
    }


def reference(x, norm_mix_pre, norm_mix_post, norm_ffn_pre, norm_ffn_post, w_up, w_down,
              attn_w_qkv, attn_w_o, mlstm_w_in, mlstm_b_gates, mlstm_norm_h, mlstm_w_out):
    h = x
    for layer in range(DEPTH):
        j = layer // N_MIXERS
        u = rms_norm(h, norm_mix_pre[layer])
        if layer % N_MIXERS == 0:
            u = moba_attention(u, attn_w_qkv[j], attn_w_o[j])
        else:
            u = mlstm_mixer(u, mlstm_w_in[j], mlstm_b_gates[j], mlstm_norm_h[j], mlstm_w_out[j])
        h = h + rms_norm(u, norm_mix_post[layer])
        u = squared_relu_mlp(rms_norm(h, norm_ffn_pre[layer]), w_up[layer], w_down[layer])
        h = h + rms_norm(u, norm_ffn_post[layer])
    return h
```

```python
import functools

import jax
import jax.numpy as jnp
from jax import lax
from jax.experimental import pallas as pl
from jax.experimental.pallas import tpu as pltpu

F32 = jnp.float32
BF16 = jnp.bfloat16

RMS_EPS = 1e-6
ATT_HEADS = 8
MOBA_BLOCK = 256
MOBA_TOPK = 3
MLSTM_HEADS = 8
GATE_SOFTCAP = 15.0

LANES = 128
VMEM_LIMIT = 56 * 1024 * 1024
NEG_BIG = -1e30
POS_BIG = 1e30

_NT = (((1,), (1,)), ((), ()))
_TN = (((0,), (0,)), ((), ()))


def _params(*sem):
    return pltpu.CompilerParams(dimension_semantics=sem, vmem_limit_bytes=VMEM_LIMIT)


def _rms_scale(x, gain):
    ms = jnp.mean(x * x, axis=-1, keepdims=True)
    return x * lax.rsqrt(ms + RMS_EPS) * gain


def _qkv_kernel(x_ref, g_ref, wq_ref, wk_ref, wvt_ref, q_ref, k_ref, km_ref, vt_ref,
                *, scale, blk, heads):
    xn = _rms_scale(x_ref[...], g_ref[...]).astype(BF16)
    tm = xn.shape[0]
    dh = xn.shape[1] // heads
    q = jnp.dot(xn, wq_ref[...], preferred_element_type=F32)
    q_ref[...] = (q * scale).astype(BF16)
    k = jnp.dot(xn, wk_ref[...], preferred_element_type=F32)
    k_ref[...] = k.astype(BF16)
    for c in range(tm // blk):
        km_ref[c] = jnp.mean(k[c * blk:(c + 1) * blk], axis=0, keepdims=True)
    vt = lax.dot_general(wvt_ref[...], xn, _NT, preferred_element_type=F32)
    for h in range(heads):
        for c in range(tm // blk):
            vt_ref[0, h, c] = vt[h * dh:(h + 1) * dh, c * blk:(c + 1) * blk].astype(BF16)


def _qkv_proj(x, gain, wq, wk, wvt, *, heads, blk, tm):
    bsz, seq, d = x.shape
    dh = d // heads
    nblk = seq // blk
    m = bsz * seq
    cpt = tm // blk
    tiles_per_b = seq // tm
    kern = functools.partial(_qkv_kernel, scale=dh ** -0.5, blk=blk, heads=heads)
    const = lambda i: (0, 0)
    q, k, km, vt = pl.pallas_call(
        kern,
        grid=(m // tm,),
        in_specs=[
            pl.BlockSpec((tm, d), lambda i: (i, 0)),
            pl.BlockSpec((1, d), const),
            pl.BlockSpec((d, d), const),
            pl.BlockSpec((d, d), const),
            pl.BlockSpec((d, d), const),
        ],
        out_specs=[
            pl.BlockSpec((tm, d), lambda i: (i, 0)),
            pl.BlockSpec((tm, d), lambda i: (i, 0)),
            pl.BlockSpec((cpt, 1, d), lambda i: (i, 0, 0)),
            pl.BlockSpec((1, heads, cpt, dh, blk),
                         lambda i: (i // tiles_per_b, 0, i % tiles_per_b, 0, 0)),
        ],
        out_shape=[
            jax.ShapeDtypeStruct((m, d), BF16),
            jax.ShapeDtypeStruct((m, d), BF16),
            jax.ShapeDtypeStruct((m // blk, 1, d), F32),
            jax.ShapeDtypeStruct((bsz, heads, nblk, dh, blk), BF16),
        ],
        compiler_params=_params("parallel"),
        name="qkv_proj",
    )(x.reshape(m, d), gain.reshape(1, d), wq, wk, wvt)
    return (q.reshape(bsz, seq, d), k.reshape(bsz, seq, d),
            km.reshape(bsz, nblk, d), vt)


def _moba_kernel(slope_ref, q_ref, k_ref, vt_ref, km_ref, o_ref, sel_ref, bias_ref, acc_ref,
                 *, hp, blk, nblk, top, dh):
    g0 = pl.program_id(1) * hp
    i = pl.program_id(2)
    slopes = [slope_ref[g0 + hh] for hh in range(hp)]

    @pl.when(i == 0)
    def _():
        kr = lax.broadcasted_iota(jnp.int32, (blk, blk), 0).astype(F32)
        for hh in range(hp):
            bias_ref[hh] = kr * slopes[hh]

    row = lax.broadcasted_iota(jnp.int32, (nblk, blk), 0)
    for hh in range(hp):
        q = q_ref[0, :, hh * dh:(hh + 1) * dh]
        km = km_ref[0, :, hh * dh:(hh + 1) * dh]
        km_hi = km.astype(BF16)
        km_lo = (km - km_hi.astype(F32)).astype(BF16)
        gate = (lax.dot_general(km_hi, q, _NT, preferred_element_type=F32)
                + lax.dot_general(km_lo, q, _NT, preferred_element_type=F32))
        gate = jnp.where(row < i, gate, -jnp.inf)
        for j in range(nblk - 1):
            gj = gate[j:j + 1, :]
            beats = (gate > gj) | ((gate == gj) & (row < j))
            cnt = jnp.sum(jnp.where(beats, 1.0, 0.0), axis=0, keepdims=True)
            sel_ref[hh, j:j + 1, :] = (cnt < top).astype(F32)

    kq = lax.broadcasted_iota(jnp.int32, (blk, blk), 0)
    qq = lax.broadcasted_iota(jnp.int32, (blk, blk), 1)
    causal = kq <= qq
    own = pl.multiple_of(i * blk, blk)
    init = []
    for hh in range(hp):
        q = q_ref[0, :, hh * dh:(hh + 1) * dh]
        kb = k_ref[0, pl.ds(own, blk), hh * dh:(hh + 1) * dh]
        s = lax.dot_general(kb, q, _NT, preferred_element_type=F32) + bias_ref[hh]
        s = jnp.where(causal, s, NEG_BIG)
        m = jnp.max(s, axis=0, keepdims=True)
        p = jnp.exp(s - m)
        l = jnp.sum(p, axis=0, keepdims=True)
        acc_ref[hh] = jnp.dot(vt_ref[0, hh, i], p.astype(BF16), preferred_element_type=F32)
        init.append((m, l))

    def body(j, carry):
        new = []
        rel = ((j - i) * blk).astype(F32)
        start = pl.multiple_of(j * blk, blk)
        for hh in range(hp):
            m, l = carry[hh]
            q = q_ref[0, :, hh * dh:(hh + 1) * dh]
            kb = k_ref[0, pl.ds(start, blk), hh * dh:(hh + 1) * dh]
            s = lax.dot_general(kb, q, _NT, preferred_element_type=F32) + bias_ref[hh]
            cj = rel * slopes[hh]
            sel = sel_ref[hh, pl.ds(j, 1), :] > 0.5
            mj = jnp.where(sel, jnp.max(s, axis=0, keepdims=True) + cj, NEG_BIG)
            m_new = jnp.maximum(m, mj)
            alpha = jnp.exp(m - m_new)
            shift = jnp.where(sel, m_new - cj, POS_BIG)
            p = jnp.exp(s - shift)
            l = alpha * l + jnp.sum(p, axis=0, keepdims=True)
            pv = jnp.dot(vt_ref[0, hh, j], p.astype(BF16), preferred_element_type=F32)
            acc_ref[hh] = alpha * acc_ref[hh] + pv
            new.append((m_new, l))
        return tuple(new)

    fin = lax.fori_loop(0, i, body, tuple(init))
    for hh in range(hp):
        _, l = fin[hh]
        o = acc_ref[hh] * (1.0 / l)
        o_ref[0, :, hh * dh:(hh + 1) * dh] = o.T.astype(BF16)


def _moba_attention(q, k, vt, km, slopes, *, heads, blk, top, hp):
    bsz, seq, d = q.shape
    dh = d // heads
    nblk = seq // blk
    kern = functools.partial(_moba_kernel, hp=hp, blk=blk, nblk=nblk, top=min(top, nblk), dh=dh)
    return pl.pallas_call(
        kern,
        grid=(bsz, heads // hp, nblk),
        in_specs=[
            pl.BlockSpec(memory_space=pltpu.SMEM),
            pl.BlockSpec((1, blk, hp * dh), lambda b, g, i: (b, i, g)),
            pl.BlockSpec((1, seq, hp * dh), lambda b, g, i: (b, 0, g)),
            pl.BlockSpec((1, hp, nblk, dh, blk), lambda b, g, i: (b, g, 0, 0, 0)),
            pl.BlockSpec((1, nblk, hp * dh), lambda b, g, i: (b, 0, g)),
        ],
        out_specs=pl.BlockSpec((1, blk, hp * dh), lambda b, g, i: (b, i, g)),
        out_shape=jax.ShapeDtypeStruct((bsz, seq, d), BF16),
        scratch_shapes=[
            pltpu.VMEM((hp, nblk, blk), F32),
            pltpu.VMEM((hp, blk, blk), F32),
            pltpu.VMEM((hp, dh, blk), F32),
        ],
        compiler_params=_params("parallel", "parallel", "arbitrary"),
        name="moba_attention",
    )(slopes, q, k, vt, km)


def _proj_res_norm_kernel(a_ref, w_ref, h_ref, g_ref, o_ref):
    u = jnp.dot(a_ref[...], w_ref[...], preferred_element_type=F32)
    o_ref[...] = h_ref[...] + _rms_scale(u, g_ref[...])


def _proj_res_norm(a, w, h, gain, *, tm):
    m, kdim = a.shape
    d = w.shape[1]
    return pl.pallas_call(
        _proj_res_norm_kernel,
        grid=(m // tm,),
        in_specs=[
            pl.BlockSpec((tm, kdim), lambda i: (i, 0)),
            pl.BlockSpec((kdim, d), lambda i: (0, 0)),
            pl.BlockSpec((tm, d), lambda i: (i, 0)),
            pl.BlockSpec((1, d), lambda i: (0, 0)),
        ],
        out_specs=pl.BlockSpec((tm, d), lambda i: (i, 0)),
        out_shape=jax.ShapeDtypeStruct((m, d), F32),
        compiler_params=_params("parallel"),
        name="proj_res_norm",
    )(a, w, h, gain.reshape(1, d))


def _mlp_kernel(h_ref, g1_ref, wu_ref, wd_ref, g2_ref, o_ref, xn_ref, acc_ref):
    f = pl.program_id(1)

    @pl.when(f == 0)
    def _():
        xn_ref[...] = _rms_scale(h_ref[...], g1_ref[...]).astype(BF16)
        acc_ref[...] = jnp.zeros_like(acc_ref)

    a = jnp.dot(xn_ref[...], wu_ref[...], preferred_element_type=F32)
    a = jnp.maximum(a, 0.0)
    acc_ref[...] += jnp.dot((a * a).astype(BF16), wd_ref[...], preferred_element_type=F32)

    @pl.when(f == pl.num_programs(1) - 1)
    def _():
        o_ref[...] = h_ref[...] + _rms_scale(acc_ref[...], g2_ref[...])


def _mlp(h, g_pre, w_up, w_down, g_post, *, tm, tf):
    m, d = h.shape
    ff = w_up.shape[1]
    return pl.pallas_call(
        _mlp_kernel,
        grid=(m // tm, ff // tf),
        in_specs=[
            pl.BlockSpec((tm, d), lambda i, f: (i, 0)),
            pl.BlockSpec((1, d), lambda i, f: (0, 0)),
            pl.BlockSpec((d, tf), lambda i, f: (0, f)),
            pl.BlockSpec((tf, d), lambda i, f: (f, 0)),
            pl.BlockSpec((1, d), lambda i, f: (0, 0)),
        ],
        out_specs=pl.BlockSpec((tm, d), lambda i, f: (i, 0)),
        out_shape=jax.ShapeDtypeStruct((m, d), F32),
        scratch_shapes=[pltpu.VMEM((tm, d), BF16), pltpu.VMEM((tm, d), F32)],
        compiler_params=_params("parallel", "arbitrary"),
        name="mlp",
    )(h, g_pre.reshape(1, d), w_up, w_down, g_post.reshape(1, d))


def _mlstm_in_kernel(x_ref, g_ref, wqk_ref, wv_ref, wo_ref, wg_ref, bg_ref,
                     q_ref, k_ref, v_ref, op_ref, gates_ref, *, heads, kscale):
    xn = _rms_scale(x_ref[...], g_ref[...]).astype(BF16)
    qk = jnp.dot(xn, wqk_ref[...], preferred_element_type=F32)
    half = qk.shape[1] // 2
    q_ref[...] = qk[:, :half].astype(BF16)
    k_ref[...] = (qk[:, half:] * kscale).astype(BF16)
    v_ref[...] = jnp.dot(xn, wv_ref[...], preferred_element_type=F32).astype(BF16)
    op_ref[...] = jnp.dot(xn, wo_ref[...], preferred_element_type=F32).astype(BF16)
    z = jnp.dot(xn, wg_ref[...], preferred_element_type=F32) + bg_ref[...]
    z = GATE_SOFTCAP * jnp.tanh(z * (1.0 / GATE_SOFTCAP))
    log_f = jnp.minimum(z, 0.0) - jnp.log1p(jnp.exp(-jnp.abs(z)))
    lane = lax.broadcasted_iota(jnp.int32, z.shape, 1)
    gates_ref[...] = jnp.where(lane < heads, z, log_f)


def _mlstm_in_proj(x, gain, wqk, wv, wo, wg, bg, *, heads, dqk, tm):
    m, d = x.shape
    nqk = wqk.shape[1]
    dv_all = wv.shape[1]
    kern = functools.partial(_mlstm_in_kernel, heads=heads, kscale=dqk ** -0.5)
    const = lambda i: (0, 0)
    row = lambda i: (i, 0)
    return pl.pallas_call(
        kern,
        grid=(m // tm,),
        in_specs=[
            pl.BlockSpec((tm, d), row),
            pl.BlockSpec((1, d), const),
            pl.BlockSpec((d, nqk), const),
            pl.BlockSpec((d, dv_all), const),
            pl.BlockSpec((d, dv_all), const),
            pl.BlockSpec((d, LANES), const),
            pl.BlockSpec((1, LANES), const),
        ],
        out_specs=[
            pl.BlockSpec((tm, nqk // 2), row),
            pl.BlockSpec((tm, nqk // 2), row),
            pl.BlockSpec((tm, dv_all), row),
            pl.BlockSpec((tm, dv_all), row),
            pl.BlockSpec((tm, LANES), row),
        ],
        out_shape=[
            jax.ShapeDtypeStruct((m, nqk // 2), BF16),
            jax.ShapeDtypeStruct((m, nqk // 2), BF16),
            jax.ShapeDtypeStruct((m, dv_all), BF16),
            jax.ShapeDtypeStruct((m, dv_all), BF16),
            jax.ShapeDtypeStruct((m, LANES), F32),
        ],
        compiler_params=_params("parallel"),
        name="mlstm_in_proj",
    )(x, gain.reshape(1, d), wqk, wv, wo, wg, bg)


def _mlstm_kernel(q_ref, k_ref, v_ref, op_ref, g_ref, nh_ref, y_ref, c_ref, m_ref,
                  *, heads, dqk, dv, chunk):
    @pl.when(pl.program_id(1) == 0)
    def _():
        c_ref[...] = jnp.zeros_like(c_ref)
        m_ref[...] = jnp.zeros_like(m_ref)

    L = chunk
    gates = g_ref[0]
    gates_t = gates.T
    tt = lax.broadcasted_iota(jnp.int32, (L, L), 0)
    ss = lax.broadcasted_iota(jnp.int32, (L, L), 1)
    causal = ss <= tt
    tri = causal.astype(F32)
    cum_c = jnp.dot(tri, gates, preferred_element_type=F32, precision=lax.Precision.HIGHEST)
    cum_r = lax.dot_general(gates_t, tri, _NT, preferred_element_type=F32,
                            precision=lax.Precision.HIGHEST)
    lane = lax.broadcasted_iota(jnp.int32, (1, 2 * dqk), 1)
    lane_v = lax.broadcasted_iota(jnp.int32, (L, dv), 1)
    ones_col = (lane_v == 0).astype(BF16)
    row2 = lax.broadcasted_iota(jnp.int32, (2 * dqk, 1), 0)

    for p in range(heads // 2):
        qg = q_ref[0, :, p * 2 * dqk:(p + 1) * 2 * dqk]
        kg = k_ref[0, :, p * 2 * dqk:(p + 1) * 2 * dqk]
        cp = c_ref[p]
        cp_b = cp.astype(BF16)
        w_cols, decays, vexts = [], [], []
        for half in range(2):
            h = 2 * p + half
            in_head = (lane >= half * dqk) & (lane < (half + 1) * dqk)
            qh = jnp.where(in_head, qg, jnp.zeros_like(qg))
            bc = cum_c[:, heads + h:heads + h + 1]
            ic = gates[:, h:h + 1]
            br = cum_r[heads + h:heads + h + 1, :]
            ir = gates_t[h:h + 1, :]
            m_prev = m_ref[h][:, 0:1]
            log_d = jnp.where(causal, bc + (ir - br), -jnp.inf)
            log_inter = bc + m_prev
            m_t = jnp.maximum(log_inter, jnp.max(log_d, axis=-1, keepdims=True))
            dmat = jnp.exp(log_d - m_t)
            w_inter = jnp.exp(log_inter - m_t)
            s = lax.dot_general(qh, kg, _NT, preferred_element_type=F32) * dmat
            vext = jnp.concatenate([v_ref[0, :, h * dv:(h + 1) * dv], ones_col], axis=1)
            num = (w_inter * jnp.dot(qh, cp_b, preferred_element_type=F32)
                   + jnp.dot(s.astype(BF16), vext, preferred_element_type=F32))
            den = num[:, dv:dv + 1]
            hout = num[:, :dv] / jnp.maximum(jnp.abs(den), jnp.exp(-m_t))
            hn = hout * lax.rsqrt(jnp.mean(hout * hout, axis=-1, keepdims=True) + RMS_EPS)
            og = jax.nn.sigmoid(op_ref[0, :, h * dv:(h + 1) * dv].astype(F32))
            y_ref[0, :, h * dv:(h + 1) * dv] = (
                og * (hn * nh_ref[:, h * dv:(h + 1) * dv])).astype(BF16)
            b_end = bc[L - 1:L, :]
            log_w = b_end - bc + ic
            m_new = jnp.maximum(b_end + m_prev, jnp.max(log_w, axis=0, keepdims=True))
            decays.append(jnp.exp(b_end + m_prev - m_new))
            w_cols.append(jnp.exp(log_w - m_new))
            vexts.append(vext)
            m_ref[h] = jnp.broadcast_to(m_new, (1, LANES))
        w_pair = jnp.where(lane < dqk, w_cols[0], w_cols[1])
        kw_t = (kg.astype(F32) * w_pair).T.astype(BF16)
        upd = jnp.where(row2 < dqk,
                        jnp.dot(kw_t, vexts[0], preferred_element_type=F32),
                        jnp.dot(kw_t, vexts[1], preferred_element_type=F32))
        decay = jnp.where(row2 < dqk, decays[0], decays[1])
        c_ref[p] = decay * cp + upd


def _mlstm_scan(q, k, v, opre, gates, norm_h, *, heads, chunk):
    bsz, seq, dv_all = v.shape
    dv = dv_all // heads
    dqk = q.shape[2] // heads
    kern = functools.partial(_mlstm_kernel, heads=heads, dqk=dqk, dv=dv, chunk=chunk)
    blk3 = lambda b, c: (b, c, 0)
    return pl.pallas_call(
        kern,
        grid=(bsz, seq // chunk),
        in_specs=[
            pl.BlockSpec((1, chunk, heads * dqk), blk3),
            pl.BlockSpec((1, chunk, heads * dqk), blk3),
            pl.BlockSpec((1, chunk, dv_all), blk3),
            pl.BlockSpec((1, chunk, dv_all), blk3),
            pl.BlockSpec((1, chunk, LANES), blk3),
            pl.BlockSpec((1, dv_all), lambda b, c: (0, 0)),
        ],
        out_specs=pl.BlockSpec((1, chunk, dv_all), blk3),
        out_shape=jax.ShapeDtypeStruct((bsz, seq, dv_all), BF16),
        scratch_shapes=[
            pltpu.VMEM((heads // 2, 2 * dqk, 2 * dv), F32),
            pltpu.VMEM((heads, 1, LANES), F32),
        ],
        compiler_params=_params("parallel", "arbitrary"),
        name="mlstm_scan",
    )(q, k, v, opre, gates, norm_h.reshape(1, dv_all))


def kernel(x, norm_mix_pre, norm_mix_post, norm_ffn_pre, norm_ffn_post, w_up, w_down,
           attn_w_qkv, attn_w_o, mlstm_w_in, mlstm_b_gates, mlstm_norm_h, mlstm_w_out):
    bsz, seq, d = x.shape
    m = bsz * seq
    heads = ATT_HEADS
    mh = MLSTM_HEADS
    dv = d // mh
    dqk = dv // 2

    wqkv = attn_w_qkv[0].astype(BF16)
    wq, wk, wvt = wqkv[:, :d], wqkv[:, d:2 * d], wqkv[:, 2 * d:].T
    q, k, km, vt = _qkv_proj(x, norm_mix_pre[0], wq, wk, wvt, heads=heads, blk=MOBA_BLOCK,
                             tm=2 * MOBA_BLOCK)
    slopes = jnp.exp2(-8.0 * jnp.arange(1, heads + 1, dtype=F32) / heads)
    att = _moba_attention(q, k, vt, km, slopes, heads=heads, blk=MOBA_BLOCK, top=MOBA_TOPK, hp=2)
    h = _proj_res_norm(att.reshape(m, d), attn_w_o[0].astype(BF16), x.reshape(m, d),
                       norm_mix_post[0], tm=512)
    h = _mlp(h, norm_ffn_pre[0], w_up[0].astype(BF16), w_down[0].astype(BF16),
             norm_ffn_post[0], tm=1024, tf=1024)

    w_in = mlstm_w_in[0]
    nqk = 2 * mh * dqk
    wqk = w_in[:, :nqk].astype(BF16)
    wv = w_in[:, nqk:nqk + d].astype(BF16)
    wo = w_in[:, nqk + d:nqk + 2 * d].astype(BF16)
    wg = jnp.pad(w_in[:, nqk + 2 * d:], ((0, 0), (0, LANES - 2 * mh))).astype(BF16)
    bg = jnp.pad(mlstm_b_gates[0], (0, LANES - 2 * mh)).reshape(1, LANES)
    qm, kmm, vm, opre, gates = _mlstm_in_proj(h, norm_mix_pre[1], wqk, wv, wo, wg, bg,
                                              heads=mh, dqk=dqk, tm=512)
    y = _mlstm_scan(qm.reshape(bsz, seq, -1), kmm.reshape(bsz, seq, -1),
                    vm.reshape(bsz, seq, d), opre.reshape(bsz, seq, d),
                    gates.reshape(bsz, seq, LANES), mlstm_norm_h[0], heads=mh, chunk=256)
    h = _proj_res_norm(y.reshape(m, d), mlstm_w_out[0].astype(BF16), h, norm_mix_post[1], tm=512)
    h = _mlp(h, norm_ffn_pre[1], w_up[1].astype(BF16), w_down[1].astype(BF16),
             norm_ffn_post[1], tm=1024, tf=1024)
    return h.reshape(bsz, seq, d)
```

```python
import functools

import jax
import jax.numpy as jnp
from jax import lax
from jax.experimental import pallas as pl
from jax.experimental.pallas import tpu as pltpu

F32 = jnp.float32
BF16 = jnp.bfloat16

RMS_EPS = 1e-6
ATT_HEADS = 8
MOBA_BLOCK = 256
MOBA_TOPK = 3
MLSTM_HEADS = 8
GATE_SOFTCAP = 15.0

LANES = 128
VMEM_LIMIT = 56 * 1024 * 1024
NEG_BIG = -1e30
POS_BIG = 1e30

_NT = (((1,), (1,)), ((), ()))
_TN = (((0,), (0,)), ((), ()))


def _params(*sem):
    return pltpu.CompilerParams(dimension_semantics=sem, vmem_limit_bytes=VMEM_LIMIT)


def _rms_scale(x, gain):
    ms = jnp.mean(x * x, axis=-1, keepdims=True)
    return x * lax.rsqrt(ms + RMS_EPS) * gain


def _qkv_kernel(x_ref, g_ref, wq_ref, wk_ref, wvt_ref, q_ref, k_ref, km_ref, vt_ref,
                *, scale, blk, heads):
    xn = _rms_scale(x_ref[...], g_ref[...]).astype(BF16)
    tm = xn.shape[0]
    dh = xn.shape[1] // heads
    q = jnp.dot(xn, wq_ref[...], preferred_element_type=F32)
    q_ref[...] = (q * scale).astype(BF16)
    k = jnp.dot(xn, wk_ref[...], preferred_element_type=F32)
    k_ref[...] = k.astype(BF16)
    for c in range(tm // blk):
        km_ref[c] = jnp.mean(k[c * blk:(c + 1) * blk], axis=0, keepdims=True)
    vt = lax.dot_general(wvt_ref[...], xn, _NT, preferred_element_type=F32)
    for h in range(heads):
        for c in range(tm // blk):
            vt_ref[0, h, c] = vt[h * dh:(h + 1) * dh, c * blk:(c + 1) * blk].astype(BF16)


def _qkv_proj(x, gain, wq, wk, wvt, *, heads, blk, tm):
    bsz, seq, d = x.shape
    dh = d // heads
    nblk = seq // blk
    m = bsz * seq
    cpt = tm // blk
    tiles_per_b = seq // tm
    kern = functools.partial(_qkv_kernel, scale=dh ** -0.5, blk=blk, heads=heads)
    const = lambda i: (0, 0)
    q, k, km, vt = pl.pallas_call(
        kern,
        grid=(m // tm,),
        in_specs=[
            pl.BlockSpec((tm, d), lambda i: (i, 0)),
            pl.BlockSpec((1, d), const),
            pl.BlockSpec((d, d), const),
            pl.BlockSpec((d, d), const),
            pl.BlockSpec((d, d), const),
        ],
        out_specs=[
            pl.BlockSpec((tm, d), lambda i: (i, 0)),
            pl.BlockSpec((tm, d), lambda i: (i, 0)),
            pl.BlockSpec((cpt, 1, d), lambda i: (i, 0, 0)),
            pl.BlockSpec((1, heads, cpt, dh, blk),
                         lambda i: (i // tiles_per_b, 0, i % tiles_per_b, 0, 0)),
        ],
        out_shape=[
            jax.ShapeDtypeStruct((m, d), BF16),
            jax.ShapeDtypeStruct((m, d), BF16),
            jax.ShapeDtypeStruct((m // blk, 1, d), F32),
            jax.ShapeDtypeStruct((bsz, heads, nblk, dh, blk), BF16),
        ],
        compiler_params=_params("parallel"),
        name="qkv_proj",
    )(x.reshape(m, d), gain.reshape(1, d), wq, wk, wvt)
    return (q.reshape(bsz, seq, d), k.reshape(bsz, seq, d),
            km.reshape(bsz, nblk, d), vt)


def _moba_items(nblk, cb):
    tiles, chunks = [], []
    for i in range(nblk):
        for c in range(-(-(i + 1) // cb)):
            tiles.append(i)
            chunks.append(c)
    if len(tiles) % 2:
        assert nblk >= 2 * cb
        tiles.append(0)
        chunks.append(1)
    n_items = len(tiles)
    tiles.append(0)
    chunks.append(0)
    return n_items, tiles, chunks


def _moba_kernel(slope_ref, tile_ref, chunk_ref, q_ref, k_ref, vt_ref, km_ref, o_ref,
                 sel_ref, bias_ref, m_ref, l_ref, acc_ref, sa_ref, sb_ref,
                 *, blk, nblk, top, cb, n_items, qc):
    slope = slope_ref[pl.program_id(1)]

    kr = lax.broadcasted_iota(jnp.int32, (blk, blk), 0)
    qq = lax.broadcasted_iota(jnp.int32, (blk, blk), 1)
    base = kr.astype(F32) * slope
    bias_ref[0] = base
    bias_ref[1] = jnp.where(kr <= qq, base, NEG_BIG)

    m_ref[...] = jnp.full(m_ref.shape, NEG_BIG, F32)
    l_ref[...] = jnp.zeros(l_ref.shape, F32)
    acc_ref[...] = jnp.zeros(acc_ref.shape, F32)

    km = km_ref[0]
    km_hi = km.astype(BF16)
    km_lo = (km - km_hi.astype(F32)).astype(BF16)
    row = lax.broadcasted_iota(jnp.int32, (nblk, qc), 0)
    rowf = row.astype(F32)
    tiles_per_qc = qc // blk

    def select(u, carry):
        q0 = pl.multiple_of(u * qc, qc)
        qs = q_ref[0, pl.ds(q0, qc), :]
        gate = (lax.dot_general(km_hi, qs, _NT, preferred_element_type=F32)
                + lax.dot_general(km_lo, qs, _NT, preferred_element_type=F32))
        qblk = u * tiles_per_qc + lax.broadcasted_iota(jnp.int32, (nblk, qc), 1) // blk
        past = row < qblk
        g = jnp.where(past, gate, -jnp.inf)
        picked = jnp.zeros((nblk, qc), F32)
        for _ in range(top):
            mx = jnp.max(g, axis=0, keepdims=True)
            first = jnp.min(jnp.where(g == mx, rowf, float(nblk)), axis=0, keepdims=True)
            hit = rowf == first
            picked = jnp.where(hit, 1.0, picked)
            g = jnp.where(hit, -jnp.inf, g)
        selv = jnp.where(((picked > 0.5) & past) | (row == qblk), 1.0, 0.0)
        for t in range(tiles_per_qc):
            sel_ref[u * tiles_per_qc + t] = selv[:, t * blk:(t + 1) * blk]
        return carry

    lax.fori_loop(0, (nblk * blk) // qc, select, 0)

    def scores(item, s_ref):
        i = tile_ref[item]
        c = chunk_ref[item]
        qi = q_ref[0, pl.ds(pl.multiple_of(i * blk, blk), blk), :]
        kc = k_ref[0, pl.ds(pl.multiple_of(c * (cb * blk), cb * blk), cb * blk), :]
        s_ref[...] = lax.dot_general(kc, qi, _NT, preferred_element_type=F32)

    def softmax_pv(item, s_ref):
        i = tile_ref[item]
        c = chunk_ref[item]
        m_old = m_ref[i]
        sels, cjs, mjs = [], [], []
        for jb in range(cb):
            j = c * cb + jb
            sb = s_ref[jb * blk:(jb + 1) * blk, :] + bias_ref[jnp.where(j == i, 1, 0)]
            s_ref[jb * blk:(jb + 1) * blk, :] = sb
            cj = ((j - i) * blk).astype(F32) * slope
            sel = sel_ref[i, pl.ds(j, 1), :] > 0.5
            mjs.append(jnp.where(sel, jnp.max(sb, axis=0, keepdims=True) + cj, NEG_BIG))
            sels.append(sel)
            cjs.append(cj)
        m_new = m_old
        for mj in mjs:
            m_new = jnp.maximum(m_new, mj)
        alpha = jnp.exp(m_old - m_new)
        l_new = alpha * l_ref[i]
        pv = None
        for jb in range(cb):
            shift = jnp.where(sels[jb], m_new - cjs[jb], POS_BIG)
            p = jnp.exp(s_ref[jb * blk:(jb + 1) * blk, :] - shift)
            l_new = l_new + jnp.sum(p, axis=0, keepdims=True)
            d = jnp.dot(vt_ref[0, 0, c * cb + jb], p.astype(BF16), preferred_element_type=F32)
            pv = d if pv is None else pv + d
        acc_ref[i] = alpha * acc_ref[i] + pv
        m_ref[i] = m_new
        l_ref[i] = l_new

    scores(0, sa_ref)

    def pair(kk, carry):
        t = 2 * kk
        scores(t + 1, sb_ref)
        softmax_pv(t, sa_ref)
        scores(t + 2, sa_ref)
        softmax_pv(t + 1, sb_ref)
        return carry

    lax.fori_loop(0, n_items // 2, pair, 0)

    def finish(i, carry):
        o = acc_ref[i] * (1.0 / l_ref[i])
        o_ref[0, pl.ds(pl.multiple_of(i * blk, blk), blk), :] = o.T.astype(BF16)
        return carry

    lax.fori_loop(0, nblk, finish, 0)


def _moba_attention(q, k, vt, km, slopes, *, heads, blk, top, cb):
    bsz, seq, d = q.shape
    dh = d // heads
    nblk = seq // blk
    assert nblk % cb == 0
    qc = min(seq, 4 * blk)
    n_items, tiles, chunks = _moba_items(nblk, cb)
    kern = functools.partial(_moba_kernel, blk=blk, nblk=nblk, top=min(top, nblk), cb=cb,
                             n_items=n_items, qc=qc)
    smem = pl.BlockSpec(memory_space=pltpu.SMEM)
    return pl.pallas_call(
        kern,
        grid=(bsz, heads),
        in_specs=[
            smem, smem, smem,
            pl.BlockSpec((1, seq, dh), lambda b, h: (b, 0, h)),
            pl.BlockSpec((1, seq, dh), lambda b, h: (b, 0, h)),
            pl.BlockSpec((1, 1, nblk, dh, blk), lambda b, h: (b, h, 0, 0, 0)),
            pl.BlockSpec((1, nblk, dh), lambda b, h: (b, 0, h)),
        ],
        out_specs=pl.BlockSpec((1, seq, dh), lambda b, h: (b, 0, h)),
        out_shape=jax.ShapeDtypeStruct((bsz, seq, d), BF16),
        scratch_shapes=[
            pltpu.VMEM((nblk, nblk, blk), F32),
            pltpu.VMEM((2, blk, blk), F32),
            pltpu.VMEM((nblk, 1, blk), F32),
            pltpu.VMEM((nblk, 1, blk), F32),
            pltpu.VMEM((nblk, dh, blk), F32),
            pltpu.VMEM((cb * blk, blk), F32),
            pltpu.VMEM((cb * blk, blk), F32),
        ],
        compiler_params=_params("parallel", "parallel"),
        name="moba_attention",
    )(slopes, jnp.asarray(tiles, jnp.int32), jnp.asarray(chunks, jnp.int32), q, k, vt, km)


def _proj_res_norm_kernel(a_ref, w_ref, h_ref, g_ref, o_ref):
    u = jnp.dot(a_ref[...], w_ref[...], preferred_element_type=F32)
    o_ref[...] = h_ref[...] + _rms_scale(u, g_ref[...])


def _proj_res_norm(a, w, h, gain, *, tm):
    m, kdim = a.shape
    d = w.shape[1]
    return pl.pallas_call(
        _proj_res_norm_kernel,
        grid=(m // tm,),
        in_specs=[
            pl.BlockSpec((tm, kdim), lambda i: (i, 0)),
            pl.BlockSpec((kdim, d), lambda i: (0, 0)),
            pl.BlockSpec((tm, d), lambda i: (i, 0)),
            pl.BlockSpec((1, d), lambda i: (0, 0)),
        ],
        out_specs=pl.BlockSpec((tm, d), lambda i: (i, 0)),
        out_shape=jax.ShapeDtypeStruct((m, d), F32),
        compiler_params=_params("parallel"),
        name="proj_res_norm",
    )(a, w, h, gain.reshape(1, d))


def _mlp_kernel(h_ref, g1_ref, wu_ref, wd_ref, g2_ref, o_ref, xn_ref, acc_ref):
    f = pl.program_id(1)

    @pl.when(f == 0)
    def _():
        xn_ref[...] = _rms_scale(h_ref[...], g1_ref[...]).astype(BF16)
        acc_ref[...] = jnp.zeros_like(acc_ref)

    a = jnp.dot(xn_ref[...], wu_ref[...], preferred_element_type=F32)
    a = jnp.maximum(a, 0.0)
    acc_ref[...] += jnp.dot((a * a).astype(BF16), wd_ref[...], preferred_element_type=F32)

    @pl.when(f == pl.num_programs(1) - 1)
    def _():
        o_ref[...] = h_ref[...] + _rms_scale(acc_ref[...], g2_ref[...])


def _mlp(h, g_pre, w_up, w_down, g_post, *, tm, tf):
    m, d = h.shape
    ff = w_up.shape[1]
    return pl.pallas_call(
        _mlp_kernel,
        grid=(m // tm, ff // tf),
        in_specs=[
            pl.BlockSpec((tm, d), lambda i, f: (i, 0)),
            pl.BlockSpec((1, d), lambda i, f: (0, 0)),
            pl.BlockSpec((d, tf), lambda i, f: (0, f)),
            pl.BlockSpec((tf, d), lambda i, f: (f, 0)),
            pl.BlockSpec((1, d), lambda i, f: (0, 0)),
        ],
        out_specs=pl.BlockSpec((tm, d), lambda i, f: (i, 0)),
        out_shape=jax.ShapeDtypeStruct((m, d), F32),
        scratch_shapes=[pltpu.VMEM((tm, d), BF16), pltpu.VMEM((tm, d), F32)],
        compiler_params=_params("parallel", "arbitrary"),
        name="mlp",
    )(h, g_pre.reshape(1, d), w_up, w_down, g_post.reshape(1, d))


def _mlstm_in_kernel(x_ref, g_ref, wqk_ref, wv_ref, wo_ref, wg_ref, bg_ref,
                     q_ref, k_ref, v_ref, op_ref, gates_ref, *, heads, kscale):
    xn = _rms_scale(x_ref[...], g_ref[...]).astype(BF16)
    qk = jnp.dot(xn, wqk_ref[...], preferred_element_type=F32)
    half = qk.shape[1] // 2
    q_ref[...] = qk[:, :half].astype(BF16)
    k_ref[...] = (qk[:, half:] * kscale).astype(BF16)
    v_ref[...] = jnp.dot(xn, wv_ref[...], preferred_element_type=F32).astype(BF16)
    op_ref[...] = jnp.dot(xn, wo_ref[...], preferred_element_type=F32).astype(BF16)
    z = jnp.dot(xn, wg_ref[...], preferred_element_type=F32) + bg_ref[...]
    z = GATE_SOFTCAP * jnp.tanh(z * (1.0 / GATE_SOFTCAP))
    log_f = jnp.minimum(z, 0.0) - jnp.log1p(jnp.exp(-jnp.abs(z)))
    lane = lax.broadcasted_iota(jnp.int32, z.shape, 1)
    gates_ref[...] = jnp.where(lane < heads, z, log_f)


def _mlstm_in_proj(x, gain, wqk, wv, wo, wg, bg, *, heads, dqk, tm):
    m, d = x.shape
    nqk = wqk.shape[1]
    dv_all = wv.shape[1]
    kern = functools.partial(_mlstm_in_kernel, heads=heads, kscale=dqk ** -0.5)
    const = lambda i: (0, 0)
    row = lambda i: (i, 0)
    return pl.pallas_call(
        kern,
        grid=(m // tm,),
        in_specs=[
            pl.BlockSpec((tm, d), row),
            pl.BlockSpec((1, d), const),
            pl.BlockSpec((d, nqk), const),
            pl.BlockSpec((d, dv_all), const),
            pl.BlockSpec((d, dv_all), const),
            pl.BlockSpec((d, LANES), const),
            pl.BlockSpec((1, LANES), const),
        ],
        out_specs=[
            pl.BlockSpec((tm, nqk // 2), row),
            pl.BlockSpec((tm, nqk // 2), row),
            pl.BlockSpec((tm, dv_all), row),
            pl.BlockSpec((tm, dv_all), row),
            pl.BlockSpec((tm, LANES), row),
        ],
        out_shape=[
            jax.ShapeDtypeStruct((m, nqk // 2), BF16),
            jax.ShapeDtypeStruct((m, nqk // 2), BF16),
            jax.ShapeDtypeStruct((m, dv_all), BF16),
            jax.ShapeDtypeStruct((m, dv_all), BF16),
            jax.ShapeDtypeStruct((m, LANES), F32),
        ],
        compiler_params=_params("parallel"),
        name="mlstm_in_proj",
    )(x, gain.reshape(1, d), wqk, wv, wo, wg, bg)


def _mlstm_kernel(q_ref, k_ref, v_ref, op_ref, g_ref, nh_ref, y_ref, c_ref, m_ref,
                  *, heads, dqk, dv, chunk):
    @pl.when(pl.program_id(1) == 0)
    def _():
        c_ref[...] = jnp.zeros_like(c_ref)
        m_ref[...] = jnp.zeros_like(m_ref)

    L = chunk
    gates = g_ref[0]
    gates_t = gates.T
    tt = lax.broadcasted_iota(jnp.int32, (L, L), 0)
    ss = lax.broadcasted_iota(jnp.int32, (L, L), 1)
    causal = ss <= tt
    tri = causal.astype(F32)
    cum_c = jnp.dot(tri, gates, preferred_element_type=F32, precision=lax.Precision.HIGHEST)
    cum_r = lax.dot_general(gates_t, tri, _NT, preferred_element_type=F32,
                            precision=lax.Precision.HIGHEST)
    lane = lax.broadcasted_iota(jnp.int32, (1, 2 * dqk), 1)
    lane_v = lax.broadcasted_iota(jnp.int32, (L, dv), 1)
    ones_col = (lane_v == 0).astype(BF16)
    row2 = lax.broadcasted_iota(jnp.int32, (2 * dqk, 1), 0)

    for p in range(heads // 2):
        qg = q_ref[0, :, p * 2 * dqk:(p + 1) * 2 * dqk]
        kg = k_ref[0, :, p * 2 * dqk:(p + 1) * 2 * dqk]
        cp = c_ref[p]
        cp_b = cp.astype(BF16)
        w_cols, decays, vexts = [], [], []
        for half in range(2):
            h = 2 * p + half
            in_head = (lane >= half * dqk) & (lane < (half + 1) * dqk)
            qh = jnp.where(in_head, qg, jnp.zeros_like(qg))
            bc = cum_c[:, heads + h:heads + h + 1]
            ic = gates[:, h:h + 1]
            br = cum_r[heads + h:heads + h + 1, :]
            ir = gates_t[h:h + 1, :]
            m_prev = m_ref[h][:, 0:1]
            log_d = jnp.where(causal, bc + (ir - br), -jnp.inf)
            log_inter = bc + m_prev
            m_t = jnp.maximum(log_inter, jnp.max(log_d, axis=-1, keepdims=True))
            dmat = jnp.exp(log_d - m_t)
            w_inter = jnp.exp(log_inter - m_t)
            s = lax.dot_general(qh, kg, _NT, preferred_element_type=F32) * dmat
            vext = jnp.concatenate([v_ref[0, :, h * dv:(h + 1) * dv], ones_col], axis=1)
            num = (w_inter * jnp.dot(qh, cp_b, preferred_element_type=F32)
                   + jnp.dot(s.astype(BF16), vext, preferred_element_type=F32))
            den = num[:, dv:dv + 1]
            hout = num[:, :dv] / jnp.maximum(jnp.abs(den), jnp.exp(-m_t))
            hn = hout * lax.rsqrt(jnp.mean(hout * hout, axis=-1, keepdims=True) + RMS_EPS)
            og = jax.nn.sigmoid(op_ref[0, :, h * dv:(h + 1) * dv].astype(F32))
            y_ref[0, :, h * dv:(h + 1) * dv] = (
                og * (hn * nh_ref[:, h * dv:(h + 1) * dv])).astype(BF16)
            b_end = bc[L - 1:L, :]
            log_w = b_end - bc + ic
            m_new = jnp.maximum(b_end + m_prev, jnp.max(log_w, axis=0, keepdims=True))
            decays.append(jnp.exp(b_end + m_prev - m_new))
            w_cols.append(jnp.exp(log_w - m_new))
            vexts.append(vext)
            m_ref[h] = jnp.broadcast_to(m_new, (1, LANES))
        w_pair = jnp.where(lane < dqk, w_cols[0], w_cols[1])
        kw_t = (kg.astype(F32) * w_pair).T.astype(BF16)
        upd = jnp.where(row2 < dqk,
                        jnp.dot(kw_t, vexts[0], preferred_element_type=F32),
                        jnp.dot(kw_t, vexts[1], preferred_element_type=F32))
        decay = jnp.where(row2 < dqk, decays[0], decays[1])
        c_ref[p] = decay * cp + upd


def _mlstm_scan(q, k, v, opre, gates, norm_h, *, heads, chunk):
    bsz, seq, dv_all = v.shape
    dv = dv_all // heads
    dqk = q.shape[2] // heads
    kern = functools.partial(_mlstm_kernel, heads=heads, dqk=dqk, dv=dv, chunk=chunk)
    blk3 = lambda b, c: (b, c, 0)
    return pl.pallas_call(
        kern,
        grid=(bsz, seq // chunk),
        in_specs=[
            pl.BlockSpec((1, chunk, heads * dqk), blk3),
            pl.BlockSpec((1, chunk, heads * dqk), blk3),
            pl.BlockSpec((1, chunk, dv_all), blk3),
            pl.BlockSpec((1, chunk, dv_all), blk3),
            pl.BlockSpec((1, chunk, LANES), blk3),
            pl.BlockSpec((1, dv_all), lambda b, c: (0, 0)),
        ],
        out_specs=pl.BlockSpec((1, chunk, dv_all), blk3),
        out_shape=jax.ShapeDtypeStruct((bsz, seq, dv_all), BF16),
        scratch_shapes=[
            pltpu.VMEM((heads // 2, 2 * dqk, 2 * dv), F32),
            pltpu.VMEM((heads, 1, LANES), F32),
        ],
        compiler_params=_params("parallel", "arbitrary"),
        name="mlstm_scan",
    )(q, k, v, opre, gates, norm_h.reshape(1, dv_all))


def kernel(x, norm_mix_pre, norm_mix_post, norm_ffn_pre, norm_ffn_post, w_up, w_down,
           attn_w_qkv, attn_w_o, mlstm_w_in, mlstm_b_gates, mlstm_norm_h, mlstm_w_out):
    bsz, seq, d = x.shape
    m = bsz * seq
    heads = ATT_HEADS
    mh = MLSTM_HEADS
    dv = d // mh
    dqk = dv // 2

    wqkv = attn_w_qkv[0].astype(BF16)
    wq, wk, wvt = wqkv[:, :d], wqkv[:, d:2 * d], wqkv[:, 2 * d:].T
    q, k, km, vt = _qkv_proj(x, norm_mix_pre[0], wq, wk, wvt, heads=heads, blk=MOBA_BLOCK,
                             tm=2 * MOBA_BLOCK)
    slopes = jnp.exp2(-8.0 * jnp.arange(1, heads + 1, dtype=F32) / heads)
    att = _moba_attention(q, k, vt, km, slopes, heads=heads, blk=MOBA_BLOCK, top=MOBA_TOPK, cb=4)
    h = _proj_res_norm(att.reshape(m, d), attn_w_o[0].astype(BF16), x.reshape(m, d),
                       norm_mix_post[0], tm=512)
    h = _mlp(h, norm_ffn_pre[0], w_up[0].astype(BF16), w_down[0].astype(BF16),
             norm_ffn_post[0], tm=1024, tf=1024)

    w_in = mlstm_w_in[0]
    nqk = 2 * mh * dqk
    wqk = w_in[:, :nqk].astype(BF16)
    wv = w_in[:, nqk:nqk + d].astype(BF16)
    wo = w_in[:, nqk + d:nqk + 2 * d].astype(BF16)
    wg = jnp.pad(w_in[:, nqk + 2 * d:], ((0, 0), (0, LANES - 2 * mh))).astype(BF16)
    bg = jnp.pad(mlstm_b_gates[0], (0, LANES - 2 * mh)).reshape(1, LANES)
    qm, kmm, vm, opre, gates = _mlstm_in_proj(h, norm_mix_pre[1], wqk, wv, wo, wg, bg,
                                              heads=mh, dqk=dqk, tm=512)
    y = _mlstm_scan(qm.reshape(bsz, seq, -1), kmm.reshape(bsz, seq, -1),
                    vm.reshape(bsz, seq, d), opre.reshape(bsz, seq, d),
                    gates.reshape(bsz, seq, LANES), mlstm_norm_h[0], heads=mh, chunk=256)
    h = _proj_res_norm(y.reshape(m, d), mlstm_w_out[0].astype(BF16), h, norm_mix_post[1], tm=512)
    h = _mlp(h, norm_ffn_pre[1], w_up[1].astype(BF16), w_down[1].astype(BF16),
             norm_ffn_post[1], tm=1024, tf=1024)
    return h.reshape(bsz, seq, d)
```

```python
import functools

import jax
import jax.numpy as jnp
from jax import lax
from jax.experimental import pallas as pl
from jax.experimental.pallas import tpu as pltpu

F32 = jnp.float32
BF16 = jnp.bfloat16

RMS_EPS = 1e-6
ATT_HEADS = 8
MOBA_BLOCK = 256
MOBA_TOPK = 3
MLSTM_HEADS = 8
GATE_SOFTCAP = 15.0

LANES = 128
VMEM_LIMIT = 56 * 1024 * 1024
NEG_BIG = -1e30
POS_BIG = 1e30
LOG2E = 1.4426950408889634
BF16_ROWS = 16
MOBA_DEPTH = 4

_NT = (((1,), (1,)), ((), ()))
_TN = (((0,), (0,)), ((), ()))


def _params(*sem):
    return pltpu.CompilerParams(dimension_semantics=sem, vmem_limit_bytes=VMEM_LIMIT)


def _rms_scale(x, gain):
    ms = jnp.mean(x * x, axis=-1, keepdims=True)
    return x * lax.rsqrt(ms + RMS_EPS) * gain


def _qkv_kernel(x_ref, g_ref, wq_ref, wk_ref, wvt_ref, q_ref, k_ref, km_ref, vt_ref,
                *, scale, blk, heads):
    xn = _rms_scale(x_ref[...], g_ref[...]).astype(BF16)
    tm = xn.shape[0]
    dh = xn.shape[1] // heads
    q = jnp.dot(xn, wq_ref[...], preferred_element_type=F32)
    q_ref[...] = (q * scale).astype(BF16)
    k = jnp.dot(xn, wk_ref[...], preferred_element_type=F32)
    k_ref[...] = k.astype(BF16)
    for c in range(tm // blk):
        km_ref[c] = jnp.mean(k[c * blk:(c + 1) * blk], axis=0, keepdims=True)
    vt = lax.dot_general(wvt_ref[...], xn, _NT, preferred_element_type=F32)
    ones_row = (lax.broadcasted_iota(jnp.int32, (BF16_ROWS, blk), 0) == 0).astype(BF16)
    for h in range(heads):
        for c in range(tm // blk):
            vt_ref[0, h, c, :dh, :] = vt[h * dh:(h + 1) * dh, c * blk:(c + 1) * blk].astype(BF16)
            vt_ref[0, h, c, dh:, :] = ones_row


def _qkv_proj(x, gain, wq, wk, wvt, *, heads, blk, tm):
    bsz, seq, d = x.shape
    dh = d // heads
    nblk = seq // blk
    m = bsz * seq
    cpt = tm // blk
    tiles_per_b = seq // tm
    kern = functools.partial(_qkv_kernel, scale=dh ** -0.5 * LOG2E, blk=blk, heads=heads)
    const = lambda i: (0, 0)
    q, k, km, vt = pl.pallas_call(
        kern,
        grid=(m // tm,),
        in_specs=[
            pl.BlockSpec((tm, d), lambda i: (i, 0)),
            pl.BlockSpec((1, d), const),
            pl.BlockSpec((d, d), const),
            pl.BlockSpec((d, d), const),
            pl.BlockSpec((d, d), const),
        ],
        out_specs=[
            pl.BlockSpec((tm, d), lambda i: (i, 0)),
            pl.BlockSpec((tm, d), lambda i: (i, 0)),
            pl.BlockSpec((cpt, 1, d), lambda i: (i, 0, 0)),
            pl.BlockSpec((1, heads, cpt, dh + BF16_ROWS, blk),
                         lambda i: (i // tiles_per_b, 0, i % tiles_per_b, 0, 0)),
        ],
        out_shape=[
            jax.ShapeDtypeStruct((m, d), BF16),
            jax.ShapeDtypeStruct((m, d), BF16),
            jax.ShapeDtypeStruct((m // blk, 1, d), F32),
            jax.ShapeDtypeStruct((bsz, heads, nblk, dh + BF16_ROWS, blk), BF16),
        ],
        compiler_params=_params("parallel"),
        name="qkv_proj",
    )(x.reshape(m, d), gain.reshape(1, d), wq, wk, wvt)
    return (q.reshape(bsz, seq, d), k.reshape(bsz, seq, d),
            km.reshape(bsz, nblk, d), vt)


def _moba_items(nblk, cb):
    tiles, chunks = [], []
    for i in range(nblk):
        for c in range(-(-(i + 1) // cb)):
            tiles.append(i)
            chunks.append(c)
    while len(tiles) % MOBA_DEPTH:
        assert nblk >= 2 * cb
        tiles.append(0)
        chunks.append(1)
    n_items = len(tiles)
    tiles += [0] * (MOBA_DEPTH - 1)
    chunks += [0] * (MOBA_DEPTH - 1)
    return n_items, tiles, chunks


def _moba_kernel(slope_ref, tile_ref, chunk_ref, q_ref, k_ref, vt_ref, km_ref, o_ref,
                 sel_ref, bias_ref, m_ref, acc_ref, cmax_ref, *s_refs,
                 blk, nblk, top, cb, n_items, qc, dh):
    slope = slope_ref[pl.program_id(1)] * LOG2E

    kr = lax.broadcasted_iota(jnp.int32, (blk, blk), 0)
    qq = lax.broadcasted_iota(jnp.int32, (blk, blk), 1)
    base = kr.astype(F32) * slope
    bias_ref[0] = base
    bias_ref[1] = jnp.where(kr <= qq, base, NEG_BIG)

    m_ref[...] = jnp.full(m_ref.shape, NEG_BIG, F32)
    acc_ref[...] = jnp.zeros(acc_ref.shape, F32)

    km = km_ref[0]
    km_hi = km.astype(BF16)
    km_lo = (km - km_hi.astype(F32)).astype(BF16)
    row = lax.broadcasted_iota(jnp.int32, (nblk, qc), 0)
    rowf = row.astype(F32)
    tiles_per_qc = qc // blk

    def select(u, carry):
        q0 = pl.multiple_of(u * qc, qc)
        qs = q_ref[0, pl.ds(q0, qc), :]
        gate = (lax.dot_general(km_hi, qs, _NT, preferred_element_type=F32)
                + lax.dot_general(km_lo, qs, _NT, preferred_element_type=F32))
        qblk = u * tiles_per_qc + lax.broadcasted_iota(jnp.int32, (nblk, qc), 1) // blk
        past = row < qblk
        g = jnp.where(past, gate, -jnp.inf)
        picked = jnp.zeros((nblk, qc), F32)
        for _ in range(top):
            mx = jnp.max(g, axis=0, keepdims=True)
            first = jnp.min(jnp.where(g == mx, rowf, float(nblk)), axis=0, keepdims=True)
            hit = rowf == first
            picked = jnp.where(hit, 1.0, picked)
            g = jnp.where(hit, -jnp.inf, g)
        selv = jnp.where(((picked > 0.5) & past) | (row == qblk), 1.0, 0.0)
        for t in range(tiles_per_qc):
            sel_ref[u * tiles_per_qc + t] = selv[:, t * blk:(t + 1) * blk]
        return carry

    lax.fori_loop(0, (nblk * blk) // qc, select, 0)

    def scores(item, slot):
        i = tile_ref[item]
        c = chunk_ref[item]
        qi = q_ref[0, pl.ds(pl.multiple_of(i * blk, blk), blk), :]
        for jb in range(cb):
            j = c * cb + jb
            kb = k_ref[0, pl.ds(pl.multiple_of(j * blk, blk), blk), :]
            sb = (lax.dot_general(kb, qi, _NT, preferred_element_type=F32)
                  + bias_ref[jnp.where(j == i, 1, 0)])
            s_refs[slot][jb * blk:(jb + 1) * blk, :] = sb
            cmax_ref[slot, jb] = jnp.max(sb, axis=0, keepdims=True)

    def softmax_pv(item, slot):
        i = tile_ref[item]
        c = chunk_ref[item]
        m_old = m_ref[i]
        sels, cjs = [], []
        m_new = m_old
        for jb in range(cb):
            j = c * cb + jb
            cj = ((j - i) * blk).astype(F32) * slope
            sel = sel_ref[i, pl.ds(j, 1), :] > 0.5
            m_new = jnp.maximum(m_new, jnp.where(sel, cmax_ref[slot, jb] + cj, NEG_BIG))
            sels.append(sel)
            cjs.append(cj)
        alpha = jnp.exp2(m_old - m_new)
        pv = None
        for jb in range(cb):
            shift = jnp.where(sels[jb], m_new - cjs[jb], POS_BIG)
            p = jnp.exp2(s_refs[slot][jb * blk:(jb + 1) * blk, :] - shift)
            d = jnp.dot(vt_ref[0, 0, c * cb + jb], p.astype(BF16), preferred_element_type=F32)
            pv = d if pv is None else pv + d
        acc_ref[i] = alpha * acc_ref[i] + pv
        m_ref[i] = m_new

    for slot in range(MOBA_DEPTH - 1):
        scores(slot, slot)

    def group(kk, carry):
        t = MOBA_DEPTH * kk
        for u in range(MOBA_DEPTH):
            scores(t + u + MOBA_DEPTH - 1, (u + MOBA_DEPTH - 1) % MOBA_DEPTH)
            softmax_pv(t + u, u)
        return carry

    lax.fori_loop(0, n_items // MOBA_DEPTH, group, 0)

    def finish(i, carry):
        acc = acc_ref[i]
        o = acc[:dh] * (1.0 / acc[dh:dh + 1])
        o_ref[0, pl.ds(pl.multiple_of(i * blk, blk), blk), :] = o.T.astype(BF16)
        return carry

    lax.fori_loop(0, nblk, finish, 0)


def _moba_attention(q, k, vt, km, slopes, *, heads, blk, top, cb):
    bsz, seq, d = q.shape
    dh = d // heads
    nblk = seq // blk
    assert nblk % cb == 0
    qc = min(seq, 4 * blk)
    n_items, tiles, chunks = _moba_items(nblk, cb)
    kern = functools.partial(_moba_kernel, blk=blk, nblk=nblk, top=min(top, nblk), cb=cb,
                             n_items=n_items, qc=qc, dh=dh)
    smem = pl.BlockSpec(memory_space=pltpu.SMEM)
    return pl.pallas_call(
        kern,
        grid=(bsz, heads),
        in_specs=[
            smem, smem, smem,
            pl.BlockSpec((1, seq, dh), lambda b, h: (b, 0, h)),
            pl.BlockSpec((1, seq, dh), lambda b, h: (b, 0, h)),
            pl.BlockSpec((1, 1, nblk, dh + BF16_ROWS, blk), lambda b, h: (b, h, 0, 0, 0)),
            pl.BlockSpec((1, nblk, dh), lambda b, h: (b, 0, h)),
        ],
        out_specs=pl.BlockSpec((1, seq, dh), lambda b, h: (b, 0, h)),
        out_shape=jax.ShapeDtypeStruct((bsz, seq, d), BF16),
        scratch_shapes=[
            pltpu.VMEM((nblk, nblk, blk), F32),
            pltpu.VMEM((2, blk, blk), F32),
            pltpu.VMEM((nblk, 1, blk), F32),
            pltpu.VMEM((nblk, dh + BF16_ROWS, blk), F32),
            pltpu.VMEM((MOBA_DEPTH, cb, 1, blk), F32),
        ] + [pltpu.VMEM((cb * blk, blk), F32)] * MOBA_DEPTH,
        compiler_params=_params("parallel", "parallel"),
        name="moba_attention",
    )(slopes, jnp.asarray(tiles, jnp.int32), jnp.asarray(chunks, jnp.int32), q, k, vt, km)


def _proj_res_norm_kernel(a_ref, w_ref, h_ref, g_ref, o_ref):
    u = jnp.dot(a_ref[...], w_ref[...], preferred_element_type=F32)
    o_ref[...] = h_ref[...] + _rms_scale(u, g_ref[...])


def _proj_res_norm(a, w, h, gain, *, tm):
    m, kdim = a.shape
    d = w.shape[1]
    return pl.pallas_call(
        _proj_res_norm_kernel,
        grid=(m // tm,),
        in_specs=[
            pl.BlockSpec((tm, kdim), lambda i: (i, 0)),
            pl.BlockSpec((kdim, d), lambda i: (0, 0)),
            pl.BlockSpec((tm, d), lambda i: (i, 0)),
            pl.BlockSpec((1, d), lambda i: (0, 0)),
        ],
        out_specs=pl.BlockSpec((tm, d), lambda i: (i, 0)),
        out_shape=jax.ShapeDtypeStruct((m, d), F32),
        compiler_params=_params("parallel"),
        name="proj_res_norm",
    )(a, w, h, gain.reshape(1, d))


def _mlp_kernel(h_ref, g1_ref, wu_ref, wd_ref, g2_ref, o_ref, xn_ref, acc_ref):
    f = pl.program_id(1)

    @pl.when(f == 0)
    def _():
        xn_ref[...] = _rms_scale(h_ref[...], g1_ref[...]).astype(BF16)
        acc_ref[...] = jnp.zeros_like(acc_ref)

    a = jnp.dot(xn_ref[...], wu_ref[...], preferred_element_type=F32)
    a = jnp.maximum(a, 0.0)
    acc_ref[...] += jnp.dot((a * a).astype(BF16), wd_ref[...], preferred_element_type=F32)

    @pl.when(f == pl.num_programs(1) - 1)
    def _():
        o_ref[...] = h_ref[...] + _rms_scale(acc_ref[...], g2_ref[...])


def _mlp(h, g_pre, w_up, w_down, g_post, *, tm, tf):
    m, d = h.shape
    ff = w_up.shape[1]
    return pl.pallas_call(
        _mlp_kernel,
        grid=(m // tm, ff // tf),
        in_specs=[
            pl.BlockSpec((tm, d), lambda i, f: (i, 0)),
            pl.BlockSpec((1, d), lambda i, f: (0, 0)),
            pl.BlockSpec((d, tf), lambda i, f: (0, f)),
            pl.BlockSpec((tf, d), lambda i, f: (f, 0)),
            pl.BlockSpec((1, d), lambda i, f: (0, 0)),
        ],
        out_specs=pl.BlockSpec((tm, d), lambda i, f: (i, 0)),
        out_shape=jax.ShapeDtypeStruct((m, d), F32),
        scratch_shapes=[pltpu.VMEM((tm, d), BF16), pltpu.VMEM((tm, d), F32)],
        compiler_params=_params("parallel", "arbitrary"),
        name="mlp",
    )(h, g_pre.reshape(1, d), w_up, w_down, g_post.reshape(1, d))


def _mlstm_in_kernel(x_ref, g_ref, wqk_ref, wv_ref, wo_ref, wg_ref, bg_ref,
                     q_ref, k_ref, v_ref, op_ref, gates_ref, *, heads, kscale):
    xn = _rms_scale(x_ref[...], g_ref[...]).astype(BF16)
    qk = jnp.dot(xn, wqk_ref[...], preferred_element_type=F32)
    half = qk.shape[1] // 2
    q_ref[...] = qk[:, :half].astype(BF16)
    k_ref[...] = (qk[:, half:] * kscale).astype(BF16)
    v_ref[...] = jnp.dot(xn, wv_ref[...], preferred_element_type=F32).astype(BF16)
    op_ref[...] = jnp.dot(xn, wo_ref[...], preferred_element_type=F32).astype(BF16)
    z = jnp.dot(xn, wg_ref[...], preferred_element_type=F32) + bg_ref[...]
    z = GATE_SOFTCAP * jnp.tanh(z * (1.0 / GATE_SOFTCAP))
    log_f = jnp.minimum(z, 0.0) - jnp.log1p(jnp.exp(-jnp.abs(z)))
    lane = lax.broadcasted_iota(jnp.int32, z.shape, 1)
    gates_ref[...] = jnp.where(lane < heads, z, log_f)


def _mlstm_in_proj(x, gain, wqk, wv, wo, wg, bg, *, heads, dqk, tm):
    m, d = x.shape
    nqk = wqk.shape[1]
    dv_all = wv.shape[1]
    kern = functools.partial(_mlstm_in_kernel, heads=heads, kscale=dqk ** -0.5)
    const = lambda i: (0, 0)
    row = lambda i: (i, 0)
    return pl.pallas_call(
        kern,
        grid=(m // tm,),
        in_specs=[
            pl.BlockSpec((tm, d), row),
            pl.BlockSpec((1, d), const),
            pl.BlockSpec((d, nqk), const),
            pl.BlockSpec((d, dv_all), const),
            pl.BlockSpec((d, dv_all), const),
            pl.BlockSpec((d, LANES), const),
            pl.BlockSpec((1, LANES), const),
        ],
        out_specs=[
            pl.BlockSpec((tm, nqk // 2), row),
            pl.BlockSpec((tm, nqk // 2), row),
            pl.BlockSpec((tm, dv_all), row),
            pl.BlockSpec((tm, dv_all), row),
            pl.BlockSpec((tm, LANES), row),
        ],
        out_shape=[
            jax.ShapeDtypeStruct((m, nqk // 2), BF16),
            jax.ShapeDtypeStruct((m, nqk // 2), BF16),
            jax.ShapeDtypeStruct((m, dv_all), BF16),
            jax.ShapeDtypeStruct((m, dv_all), BF16),
            jax.ShapeDtypeStruct((m, LANES), F32),
        ],
        compiler_params=_params("parallel"),
        name="mlstm_in_proj",
    )(x, gain.reshape(1, d), wqk, wv, wo, wg, bg)


def _mlstm_kernel(q_ref, k_ref, v_ref, op_ref, g_ref, nh_ref, y_ref, c_ref, m_ref,
                  *, heads, dqk, dv, chunk):
    @pl.when(pl.program_id(1) == 0)
    def _():
        c_ref[...] = jnp.zeros_like(c_ref)
        m_ref[...] = jnp.zeros_like(m_ref)

    L = chunk
    gates = g_ref[0]
    gates_t = gates.T
    tt = lax.broadcasted_iota(jnp.int32, (L, L), 0)
    ss = lax.broadcasted_iota(jnp.int32, (L, L), 1)
    causal = ss <= tt
    tri = causal.astype(F32)
    cum_c = jnp.dot(tri, gates, preferred_element_type=F32, precision=lax.Precision.HIGHEST)
    cum_r = lax.dot_general(gates_t, tri, _NT, preferred_element_type=F32,
                            precision=lax.Precision.HIGHEST)
    lane = lax.broadcasted_iota(jnp.int32, (1, 2 * dqk), 1)
    lane_v = lax.broadcasted_iota(jnp.int32, (L, dv), 1)
    ones_col = (lane_v == 0).astype(BF16)
    row2 = lax.broadcasted_iota(jnp.int32, (2 * dqk, 1), 0)

    for p in range(heads // 2):
        qg = q_ref[0, :, p * 2 * dqk:(p + 1) * 2 * dqk]
        kg = k_ref[0, :, p * 2 * dqk:(p + 1) * 2 * dqk]
        cp = c_ref[p]
        cp_b = cp.astype(BF16)
        w_cols, decays, vexts = [], [], []
        for half in range(2):
            h = 2 * p + half
            in_head = (lane >= half * dqk) & (lane < (half + 1) * dqk)
            qh = jnp.where(in_head, qg, jnp.zeros_like(qg))
            bc = cum_c[:, heads + h:heads + h + 1]
            ic = gates[:, h:h + 1]
            br = cum_r[heads + h:heads + h + 1, :]
            ir = gates_t[h:h + 1, :]
            m_prev = m_ref[h][:, 0:1]
            log_d = jnp.where(causal, bc + (ir - br), -jnp.inf)
            log_inter = bc + m_prev
            m_t = jnp.maximum(log_inter, jnp.max(log_d, axis=-1, keepdims=True))
            dmat = jnp.exp(log_d - m_t)
            w_inter = jnp.exp(log_inter - m_t)
            s = lax.dot_general(qh, kg, _NT, preferred_element_type=F32) * dmat
            vext = jnp.concatenate([v_ref[0, :, h * dv:(h + 1) * dv], ones_col], axis=1)
            num = (w_inter * jnp.dot(qh, cp_b, preferred_element_type=F32)
                   + jnp.dot(s.astype(BF16), vext, preferred_element_type=F32))
            den = num[:, dv:dv + 1]
            hout = num[:, :dv] / jnp.maximum(jnp.abs(den), jnp.exp(-m_t))
            hn = hout * lax.rsqrt(jnp.mean(hout * hout, axis=-1, keepdims=True) + RMS_EPS)
            og = jax.nn.sigmoid(op_ref[0, :, h * dv:(h + 1) * dv].astype(F32))
            y_ref[0, :, h * dv:(h + 1) * dv] = (
                og * (hn * nh_ref[:, h * dv:(h + 1) * dv])).astype(BF16)
            b_end = bc[L - 1:L, :]
            log_w = b_end - bc + ic
            m_new = jnp.maximum(b_end + m_prev, jnp.max(log_w, axis=0, keepdims=True))
            decays.append(jnp.exp(b_end + m_prev - m_new))
            w_cols.append(jnp.exp(log_w - m_new))
            vexts.append(vext)
            m_ref[h] = jnp.broadcast_to(m_new, (1, LANES))
        w_pair = jnp.where(lane < dqk, w_cols[0], w_cols[1])
        kw_t = (kg.astype(F32) * w_pair).T.astype(BF16)
        upd = jnp.where(row2 < dqk,
                        jnp.dot(kw_t, vexts[0], preferred_element_type=F32),
                        jnp.dot(kw_t, vexts[1], preferred_element_type=F32))
        decay = jnp.where(row2 < dqk, decays[0], decays[1])
        c_ref[p] = decay * cp + upd


def _mlstm_scan(q, k, v, opre, gates, norm_h, *, heads, chunk):
    bsz, seq, dv_all = v.shape
    dv = dv_all // heads
    dqk = q.shape[2] // heads
    kern = functools.partial(_mlstm_kernel, heads=heads, dqk=dqk, dv=dv, chunk=chunk)
    blk3 = lambda b, c: (b, c, 0)
    return pl.pallas_call(
        kern,
        grid=(bsz, seq // chunk),
        in_specs=[
            pl.BlockSpec((1, chunk, heads * dqk), blk3),
            pl.BlockSpec((1, chunk, heads * dqk), blk3),
            pl.BlockSpec((1, chunk, dv_all), blk3),
            pl.BlockSpec((1, chunk, dv_all), blk3),
            pl.BlockSpec((1, chunk, LANES), blk3),
            pl.BlockSpec((1, dv_all), lambda b, c: (0, 0)),
        ],
        out_specs=pl.BlockSpec((1, chunk, dv_all), blk3),
        out_shape=jax.ShapeDtypeStruct((bsz, seq, dv_all), BF16),
        scratch_shapes=[
            pltpu.VMEM((heads // 2, 2 * dqk, 2 * dv), F32),
            pltpu.VMEM((heads, 1, LANES), F32),
        ],
        compiler_params=_params("parallel", "arbitrary"),
        name="mlstm_scan",
    )(q, k, v, opre, gates, norm_h.reshape(1, dv_all))


def kernel(x, norm_mix_pre, norm_mix_post, norm_ffn_pre, norm_ffn_post, w_up, w_down,
           attn_w_qkv, attn_w_o, mlstm_w_in, mlstm_b_gates, mlstm_norm_h, mlstm_w_out):
    bsz, seq, d = x.shape
    m = bsz * seq
    heads = ATT_HEADS
    mh = MLSTM_HEADS
    dv = d // mh
    dqk = dv // 2

    wqkv = attn_w_qkv[0].astype(BF16)
    wq, wk, wvt = wqkv[:, :d], wqkv[:, d:2 * d], wqkv[:, 2 * d:].T
    q, k, km, vt = _qkv_proj(x, norm_mix_pre[0], wq, wk, wvt, heads=heads, blk=MOBA_BLOCK,
                             tm=2 * MOBA_BLOCK)
    slopes = jnp.exp2(-8.0 * jnp.arange(1, heads + 1, dtype=F32) / heads)
    att = _moba_attention(q, k, vt, km, slopes, heads=heads, blk=MOBA_BLOCK, top=MOBA_TOPK, cb=4)
    h = _proj_res_norm(att.reshape(m, d), attn_w_o[0].astype(BF16), x.reshape(m, d),
                       norm_mix_post[0], tm=512)
    h = _mlp(h, norm_ffn_pre[0], w_up[0].astype(BF16), w_down[0].astype(BF16),
             norm_ffn_post[0], tm=1024, tf=1024)

    w_in = mlstm_w_in[0]
    nqk = 2 * mh * dqk
    wqk = w_in[:, :nqk].astype(BF16)
    wv = w_in[:, nqk:nqk + d].astype(BF16)
    wo = w_in[:, nqk + d:nqk + 2 * d].astype(BF16)
    wg = jnp.pad(w_in[:, nqk + 2 * d:], ((0, 0), (0, LANES - 2 * mh))).astype(BF16)
    bg = jnp.pad(mlstm_b_gates[0], (0, LANES - 2 * mh)).reshape(1, LANES)
    qm, kmm, vm, opre, gates = _mlstm_in_proj(h, norm_mix_pre[1], wqk, wv, wo, wg, bg,
                                              heads=mh, dqk=dqk, tm=512)
    y = _mlstm_scan(qm.reshape(bsz, seq, -1), kmm.reshape(bsz, seq, -1),
                    vm.reshape(bsz, seq, d), opre.reshape(bsz, seq, d),
                    gates.reshape(bsz, seq, LANES), mlstm_norm_h[0], heads=mh, chunk=256)
    h = _proj_res_norm(y.reshape(m, d), mlstm_w_out[0].astype(BF16), h, norm_mix_post[1], tm=512)
    h = _mlp(h, norm_ffn_pre[1], w_up[1].astype(BF16), w_down[1].astype(BF16),
             norm_ffn_post[1], tm=1024, tf=1024)
    return h.reshape(bsz, seq, d)
```

```python
import functools

import jax
import jax.numpy as jnp
from jax import lax
from jax.experimental import pallas as pl
from jax.experimental.pallas import tpu as pltpu

F32 = jnp.float32
BF16 = jnp.bfloat16

RMS_EPS = 1e-6
ATT_HEADS = 8
MOBA_BLOCK = 256
MOBA_TOPK = 3
MLSTM_HEADS = 8
GATE_SOFTCAP = 15.0

LANES = 128
VMEM_LIMIT = 56 * 1024 * 1024
NEG_BIG = -1e30
POS_BIG = 1e30
LOG2E = 1.4426950408889634
BF16_ROWS = 16
MOBA_DEPTH = 4

_NT = (((1,), (1,)), ((), ()))


def _params(*sem):
    return pltpu.CompilerParams(dimension_semantics=sem, vmem_limit_bytes=VMEM_LIMIT)


def _rms_scale(x, gain):
    ms = jnp.mean(x * x, axis=-1, keepdims=True)
    return x * lax.rsqrt(ms + RMS_EPS) * gain


def _qkv_kernel(x_ref, g_ref, wq_ref, wk_ref, wvt_ref, q_ref, k_ref, km_ref, vt_ref,
                *, scale, blk, heads):
    xn = _rms_scale(x_ref[...], g_ref[...]).astype(BF16)
    tm = xn.shape[0]
    dh = xn.shape[1] // heads
    q = jnp.dot(xn, wq_ref[...], preferred_element_type=F32)
    q_ref[...] = (q * scale).astype(BF16)
    k = jnp.dot(xn, wk_ref[...], preferred_element_type=F32)
    k_ref[...] = k.astype(BF16)
    for c in range(tm // blk):
        km_ref[c] = jnp.mean(k[c * blk:(c + 1) * blk], axis=0, keepdims=True)
    vt = lax.dot_general(wvt_ref[...], xn, _NT, preferred_element_type=F32)
    ones_row = (lax.broadcasted_iota(jnp.int32, (BF16_ROWS, blk), 0) == 0).astype(BF16)
    for h in range(heads):
        for c in range(tm // blk):
            vt_ref[0, h, c, :dh, :] = vt[h * dh:(h + 1) * dh, c * blk:(c + 1) * blk].astype(BF16)
            vt_ref[0, h, c, dh:, :] = ones_row


def _qkv_proj(x, gain, wq, wk, wvt, *, heads, blk, tm):
    bsz, seq, d = x.shape
    dh = d // heads
    nblk = seq // blk
    m = bsz * seq
    cpt = tm // blk
    tiles_per_b = seq // tm
    kern = functools.partial(_qkv_kernel, scale=dh ** -0.5 * LOG2E, blk=blk, heads=heads)
    const = lambda i: (0, 0)
    q, k, km, vt = pl.pallas_call(
        kern,
        grid=(m // tm,),
        in_specs=[
            pl.BlockSpec((tm, d), lambda i: (i, 0)),
            pl.BlockSpec((1, d), const),
            pl.BlockSpec((d, d), const),
            pl.BlockSpec((d, d), const),
            pl.BlockSpec((d, d), const),
        ],
        out_specs=[
            pl.BlockSpec((tm, d), lambda i: (i, 0)),
            pl.BlockSpec((tm, d), lambda i: (i, 0)),
            pl.BlockSpec((cpt, 1, d), lambda i: (i, 0, 0)),
            pl.BlockSpec((1, heads, cpt, dh + BF16_ROWS, blk),
                         lambda i: (i // tiles_per_b, 0, i % tiles_per_b, 0, 0)),
        ],
        out_shape=[
            jax.ShapeDtypeStruct((m, d), BF16),
            jax.ShapeDtypeStruct((m, d), BF16),
            jax.ShapeDtypeStruct((m // blk, 1, d), F32),
            jax.ShapeDtypeStruct((bsz, heads, nblk, dh + BF16_ROWS, blk), BF16),
        ],
        compiler_params=_params("parallel"),
        name="qkv_proj",
    )(x.reshape(m, d), gain.reshape(1, d), wq, wk, wvt)
    return (q.reshape(bsz, seq, d), k.reshape(bsz, seq, d),
            km.reshape(bsz, nblk, d), vt)


def _moba_items(nblk, cb):
    tiles, chunks = [], []
    for i in range(nblk):
        for c in range(-(-(i + 1) // cb)):
            tiles.append(i)
            chunks.append(c)
    while len(tiles) % MOBA_DEPTH:
        assert nblk >= 2 * cb
        tiles.append(0)
        chunks.append(1)
    n_items = len(tiles)
    tiles += [0] * (MOBA_DEPTH - 1)
    chunks += [0] * (MOBA_DEPTH - 1)
    return n_items, tiles, chunks


def _moba_kernel(slope_ref, tile_ref, chunk_ref, q_ref, k_ref, vt_ref, km_ref, o_ref,
                 sel_ref, bias_ref, m_ref, acc_ref, cmax_ref, *s_refs,
                 blk, nblk, top, cb, n_items, qc, dh):
    slope = slope_ref[pl.program_id(1)] * LOG2E

    kr = lax.broadcasted_iota(jnp.int32, (blk, blk), 0)
    qq = lax.broadcasted_iota(jnp.int32, (blk, blk), 1)
    base = kr.astype(F32) * slope
    bias_ref[0] = base
    bias_ref[1] = jnp.where(kr <= qq, base, NEG_BIG)

    m_ref[...] = jnp.full(m_ref.shape, NEG_BIG, F32)
    acc_ref[...] = jnp.zeros(acc_ref.shape, F32)

    km = km_ref[0]
    km_hi = km.astype(BF16)
    km_lo = (km - km_hi.astype(F32)).astype(BF16)
    row = lax.broadcasted_iota(jnp.int32, (nblk, qc), 0)
    rowf = row.astype(F32)
    tiles_per_qc = qc // blk

    def select(u, carry):
        q0 = pl.multiple_of(u * qc, qc)
        qs = q_ref[0, pl.ds(q0, qc), :]
        gate = (lax.dot_general(km_hi, qs, _NT, preferred_element_type=F32)
                + lax.dot_general(km_lo, qs, _NT, preferred_element_type=F32))
        qblk = u * tiles_per_qc + lax.broadcasted_iota(jnp.int32, (nblk, qc), 1) // blk
        past = row < qblk
        g = jnp.where(past, gate, -jnp.inf)
        picked = jnp.zeros((nblk, qc), F32)
        for _ in range(top):
            mx = jnp.max(g, axis=0, keepdims=True)
            first = jnp.min(jnp.where(g == mx, rowf, float(nblk)), axis=0, keepdims=True)
            hit = rowf == first
            picked = jnp.where(hit, 1.0, picked)
            g = jnp.where(hit, -jnp.inf, g)
        selv = jnp.where(((picked > 0.5) & past) | (row == qblk), 1.0, 0.0)
        for t in range(tiles_per_qc):
            sel_ref[u * tiles_per_qc + t] = selv[:, t * blk:(t + 1) * blk]
        return carry

    lax.fori_loop(0, (nblk * blk) // qc, select, 0)

    def scores(item, slot):
        i = tile_ref[item]
        c = chunk_ref[item]
        qi = q_ref[0, pl.ds(pl.multiple_of(i * blk, blk), blk), :]
        for jb in range(cb):
            j = c * cb + jb
            kb = k_ref[0, pl.ds(pl.multiple_of(j * blk, blk), blk), :]
            sb = (lax.dot_general(kb, qi, _NT, preferred_element_type=F32)
                  + bias_ref[jnp.where(j == i, 1, 0)])
            s_refs[slot][jb * blk:(jb + 1) * blk, :] = sb
            cmax_ref[slot, jb] = jnp.max(sb, axis=0, keepdims=True)

    def softmax_pv(item, slot):
        i = tile_ref[item]
        c = chunk_ref[item]
        m_old = m_ref[i]
        sels, cjs = [], []
        m_new = m_old
        for jb in range(cb):
            j = c * cb + jb
            cj = ((j - i) * blk).astype(F32) * slope
            sel = sel_ref[i, pl.ds(j, 1), :] > 0.5
            m_new = jnp.maximum(m_new, jnp.where(sel, cmax_ref[slot, jb] + cj, NEG_BIG))
            sels.append(sel)
            cjs.append(cj)
        alpha = jnp.exp2(m_old - m_new)
        pv = None
        for jb in range(cb):
            shift = jnp.where(sels[jb], m_new - cjs[jb], POS_BIG)
            p = jnp.exp2(s_refs[slot][jb * blk:(jb + 1) * blk, :] - shift)
            d = jnp.dot(vt_ref[0, 0, c * cb + jb], p.astype(BF16), preferred_element_type=F32)
            pv = d if pv is None else pv + d
        acc_ref[i] = alpha * acc_ref[i] + pv
        m_ref[i] = m_new

    for slot in range(MOBA_DEPTH - 1):
        scores(slot, slot)

    def group(kk, carry):
        t = MOBA_DEPTH * kk
        for u in range(MOBA_DEPTH):
            scores(t + u + MOBA_DEPTH - 1, (u + MOBA_DEPTH - 1) % MOBA_DEPTH)
            softmax_pv(t + u, u)
        return carry

    lax.fori_loop(0, n_items // MOBA_DEPTH, group, 0)

    def finish(i, carry):
        acc = acc_ref[i]
        o = acc[:dh] * (1.0 / acc[dh:dh + 1])
        o_ref[0, pl.ds(pl.multiple_of(i * blk, blk), blk), :] = o.T.astype(BF16)
        return carry

    lax.fori_loop(0, nblk, finish, 0)


def _moba_attention(q, k, vt, km, slopes, *, heads, blk, top, cb):
    bsz, seq, d = q.shape
    dh = d // heads
    nblk = seq // blk
    assert nblk % cb == 0
    qc = min(seq, 4 * blk)
    n_items, tiles, chunks = _moba_items(nblk, cb)
    kern = functools.partial(_moba_kernel, blk=blk, nblk=nblk, top=min(top, nblk), cb=cb,
                             n_items=n_items, qc=qc, dh=dh)
    smem = pl.BlockSpec(memory_space=pltpu.SMEM)
    return pl.pallas_call(
        kern,
        grid=(bsz, heads),
        in_specs=[
            smem, smem, smem,
            pl.BlockSpec((1, seq, dh), lambda b, h: (b, 0, h)),
            pl.BlockSpec((1, seq, dh), lambda b, h: (b, 0, h)),
            pl.BlockSpec((1, 1, nblk, dh + BF16_ROWS, blk), lambda b, h: (b, h, 0, 0, 0)),
            pl.BlockSpec((1, nblk, dh), lambda b, h: (b, 0, h)),
        ],
        out_specs=pl.BlockSpec((1, seq, dh), lambda b, h: (b, 0, h)),
        out_shape=jax.ShapeDtypeStruct((bsz, seq, d), BF16),
        scratch_shapes=[
            pltpu.VMEM((nblk, nblk, blk), F32),
            pltpu.VMEM((2, blk, blk), F32),
            pltpu.VMEM((nblk, 1, blk), F32),
            pltpu.VMEM((nblk, dh + BF16_ROWS, blk), F32),
            pltpu.VMEM((MOBA_DEPTH, cb, 1, blk), F32),
        ] + [pltpu.VMEM((cb * blk, blk), F32)] * MOBA_DEPTH,
        compiler_params=_params("parallel", "parallel"),
        name="moba_attention",
    )(slopes, jnp.asarray(tiles, jnp.int32), jnp.asarray(chunks, jnp.int32), q, k, vt, km)


def _proj_res_norm_kernel(a_ref, w_ref, h_ref, g_ref, o_ref):
    u = jnp.dot(a_ref[...], w_ref[...], preferred_element_type=F32)
    o_ref[...] = h_ref[...] + _rms_scale(u, g_ref[...])


def _proj_res_norm(a, w, h, gain, *, tm):
    m, kdim = a.shape
    d = w.shape[1]
    return pl.pallas_call(
        _proj_res_norm_kernel,
        grid=(m // tm,),
        in_specs=[
            pl.BlockSpec((tm, kdim), lambda i: (i, 0)),
            pl.BlockSpec((kdim, d), lambda i: (0, 0)),
            pl.BlockSpec((tm, d), lambda i: (i, 0)),
            pl.BlockSpec((1, d), lambda i: (0, 0)),
        ],
        out_specs=pl.BlockSpec((tm, d), lambda i: (i, 0)),
        out_shape=jax.ShapeDtypeStruct((m, d), F32),
        compiler_params=_params("parallel"),
        name="proj_res_norm",
    )(a, w, h, gain.reshape(1, d))


def _mlp_kernel(h_ref, g1_ref, wu_ref, wd_ref, g2_ref, o_ref, xn_ref, acc_ref):
    f = pl.program_id(1)

    @pl.when(f == 0)
    def _():
        xn_ref[...] = _rms_scale(h_ref[...], g1_ref[...]).astype(BF16)
        acc_ref[...] = jnp.zeros_like(acc_ref)

    a = jnp.dot(xn_ref[...], wu_ref[...], preferred_element_type=F32)
    a = jnp.maximum(a, 0.0)
    acc_ref[...] += jnp.dot((a * a).astype(BF16), wd_ref[...], preferred_element_type=F32)

    @pl.when(f == pl.num_programs(1) - 1)
    def _():
        o_ref[...] = h_ref[...] + _rms_scale(acc_ref[...], g2_ref[...])


def _mlp(h, g_pre, w_up, w_down, g_post, *, tm, tf):
    m, d = h.shape
    ff = w_up.shape[1]
    return pl.pallas_call(
        _mlp_kernel,
        grid=(m // tm, ff // tf),
        in_specs=[
            pl.BlockSpec((tm, d), lambda i, f: (i, 0)),
            pl.BlockSpec((1, d), lambda i, f: (0, 0)),
            pl.BlockSpec((d, tf), lambda i, f: (0, f)),
            pl.BlockSpec((tf, d), lambda i, f: (f, 0)),
            pl.BlockSpec((1, d), lambda i, f: (0, 0)),
        ],
        out_specs=pl.BlockSpec((tm, d), lambda i, f: (i, 0)),
        out_shape=jax.ShapeDtypeStruct((m, d), F32),
        scratch_shapes=[pltpu.VMEM((tm, d), BF16), pltpu.VMEM((tm, d), F32)],
        compiler_params=_params("parallel", "arbitrary"),
        name="mlp",
    )(h, g_pre.reshape(1, d), w_up, w_down, g_post.reshape(1, d))


def _mlstm_in_kernel(x_ref, g_ref, wqt_ref, wk_ref, wvt_ref, wo_ref, wg_ref, bg_ref,
                     qt_ref, k_ref, vt_ref, op_ref, gates_ref, *, heads, kscale):
    xn = _rms_scale(x_ref[...], g_ref[...]).astype(BF16)
    qt_ref[0] = lax.dot_general(wqt_ref[...], xn, _NT, preferred_element_type=F32).astype(BF16)
    k_ref[...] = (jnp.dot(xn, wk_ref[...], preferred_element_type=F32) * kscale).astype(BF16)
    vt_ref[0] = lax.dot_general(wvt_ref[...], xn, _NT, preferred_element_type=F32).astype(BF16)
    op_ref[...] = jnp.dot(xn, wo_ref[...], preferred_element_type=F32).astype(BF16)
    z = jnp.dot(xn, wg_ref[...], preferred_element_type=F32) + bg_ref[...]
    z = GATE_SOFTCAP * jnp.tanh(z * (1.0 / GATE_SOFTCAP))
    log_f = jnp.minimum(z, 0.0) - jnp.log1p(jnp.exp(-jnp.abs(z)))
    lane = lax.broadcasted_iota(jnp.int32, z.shape, 1)
    gates_ref[...] = jnp.where(lane < heads, z, log_f)


def _mlstm_in_proj(x, gain, wqt, wk, wvt, wo, wg, bg, *, bsz, heads, dqk, tm):
    m, d = x.shape
    seq = m // bsz
    nq = wqt.shape[0]
    dv_all = wvt.shape[0]
    tiles_per_b = seq // tm
    kern = functools.partial(_mlstm_in_kernel, heads=heads, kscale=dqk ** -0.5)
    const = lambda i: (0, 0)
    row = lambda i: (i, 0)
    tcol = lambda i: (i // tiles_per_b, 0, i % tiles_per_b)
    return pl.pallas_call(
        kern,
        grid=(m // tm,),
        in_specs=[
            pl.BlockSpec((tm, d), row),
            pl.BlockSpec((1, d), const),
            pl.BlockSpec((nq, d), const),
            pl.BlockSpec((d, nq), const),
            pl.BlockSpec((dv_all, d), const),
            pl.BlockSpec((d, dv_all), const),
            pl.BlockSpec((d, LANES), const),
            pl.BlockSpec((1, LANES), const),
        ],
        out_specs=[
            pl.BlockSpec((1, nq, tm), tcol),
            pl.BlockSpec((tm, nq), row),
            pl.BlockSpec((1, dv_all, tm), tcol),
            pl.BlockSpec((tm, dv_all), row),
            pl.BlockSpec((tm, LANES), row),
        ],
        out_shape=[
            jax.ShapeDtypeStruct((bsz, nq, seq), BF16),
            jax.ShapeDtypeStruct((m, nq), BF16),
            jax.ShapeDtypeStruct((bsz, dv_all, seq), BF16),
            jax.ShapeDtypeStruct((m, dv_all), BF16),
            jax.ShapeDtypeStruct((m, LANES), F32),
        ],
        compiler_params=_params("parallel"),
        name="mlstm_in_proj",
    )(x, gain.reshape(1, d), wqt, wk, wvt, wo, wg, bg)


def _mlstm_kernel(qt_ref, k_ref, vt_ref, op_ref, g_ref, nh_ref, y_ref, c_ref, m_ref,
                  *, heads, dqk, dv, chunk):
    @pl.when(pl.program_id(1) == 0)
    def _():
        c_ref[...] = jnp.zeros_like(c_ref)
        m_ref[...] = jnp.zeros_like(m_ref)

    L = chunk
    gates_t = g_ref[0].T
    ss = lax.broadcasted_iota(jnp.int32, (L, L), 0)
    tt = lax.broadcasted_iota(jnp.int32, (L, L), 1)
    causal = ss <= tt
    i_rows = gates_t[0:heads]
    f_rows = gates_t[heads:2 * heads]
    b_rows = jnp.dot(f_rows, causal.astype(F32), preferred_element_type=F32,
                     precision=lax.Precision.HIGHEST)
    u_rows = i_rows - b_rows
    u_cols = jnp.concatenate([u_rows, jnp.zeros((LANES - heads, L), F32)], axis=0).T
    ones_rows = (lax.broadcasted_iota(jnp.int32, (BF16_ROWS, L), 0) == 0).astype(BF16)
    sub = lax.broadcasted_iota(jnp.int32, (2 * dqk, 1), 0)
    lane = lax.broadcasted_iota(jnp.int32, (1, 2 * dqk), 1)

    for p in range(heads // 2):
        qt_pair = qt_ref[0, p * 2 * dqk:(p + 1) * 2 * dqk, :]
        kg = k_ref[0, :, p * 2 * dqk:(p + 1) * 2 * dqk]
        ct = c_ref[p]
        ct_b = ct.astype(BF16)
        upds, decays = [], []
        for half in range(2):
            h = 2 * p + half
            in_head = (sub >= half * dqk) & (sub < (half + 1) * dqk)
            qt_h = jnp.where(in_head, qt_pair, jnp.zeros_like(qt_pair))
            b_row = b_rows[h:h + 1]
            i_row = i_rows[h:h + 1]
            m_prev = m_ref[h][:, 0:1]
            log_d = jnp.where(causal, u_cols[:, h:h + 1] + b_row, -jnp.inf)
            log_inter = b_row + m_prev
            m_t = jnp.maximum(log_inter, jnp.max(log_d, axis=0, keepdims=True))
            dmat = jnp.exp(log_d - m_t)
            w_inter = jnp.exp(log_inter - m_t)
            s_t = jnp.dot(kg, qt_h, preferred_element_type=F32) * dmat
            vext = jnp.concatenate([vt_ref[0, h * dv:(h + 1) * dv, :], ones_rows], axis=0)
            num = (w_inter * jnp.dot(ct_b, qt_h, preferred_element_type=F32)
                   + jnp.dot(vext, s_t.astype(BF16), preferred_element_type=F32))
            den = num[dv:dv + 1]
            hout = num[:dv] * (1.0 / jnp.maximum(jnp.abs(den), jnp.exp(-m_t)))
            hn = hout * lax.rsqrt(jnp.mean(hout * hout, axis=0, keepdims=True) + RMS_EPS)
            og = jax.nn.sigmoid(op_ref[0, :, h * dv:(h + 1) * dv].astype(F32))
            y_ref[0, :, h * dv:(h + 1) * dv] = (
                og * (hn.T * nh_ref[:, h * dv:(h + 1) * dv])).astype(BF16)
            b_end = b_row[:, L - 1:L]
            log_w = b_end - b_row + i_row
            m_new = jnp.maximum(b_end + m_prev, jnp.max(log_w, axis=1, keepdims=True))
            decays.append(jnp.exp(b_end + m_prev - m_new))
            vw = (vext.astype(F32) * jnp.exp(log_w - m_new)).astype(BF16)
            upds.append(jnp.dot(vw, kg, preferred_element_type=F32))
            m_ref[h] = jnp.broadcast_to(m_new, (1, LANES))
        first = lane < dqk
        c_ref[p] = jnp.where(first, decays[0], decays[1]) * ct + jnp.where(first, upds[0], upds[1])


def _mlstm_scan(qt, k, vt, opre, gates, norm_h, *, heads, chunk):
    bsz, dv_all, seq = vt.shape
    dv = dv_all // heads
    dqk = qt.shape[1] // heads
    kern = functools.partial(_mlstm_kernel, heads=heads, dqk=dqk, dv=dv, chunk=chunk)
    rows = lambda b, c: (b, c, 0)
    cols = lambda b, c: (b, 0, c)
    return pl.pallas_call(
        kern,
        grid=(bsz, seq // chunk),
        in_specs=[
            pl.BlockSpec((1, heads * dqk, chunk), cols),
            pl.BlockSpec((1, chunk, heads * dqk), rows),
            pl.BlockSpec((1, dv_all, chunk), cols),
            pl.BlockSpec((1, chunk, dv_all), rows),
            pl.BlockSpec((1, chunk, LANES), rows),
            pl.BlockSpec((1, dv_all), lambda b, c: (0, 0)),
        ],
        out_specs=pl.BlockSpec((1, chunk, dv_all), rows),
        out_shape=jax.ShapeDtypeStruct((bsz, seq, dv_all), BF16),
        scratch_shapes=[
            pltpu.VMEM((heads // 2, dv + BF16_ROWS, 2 * dqk), F32),
            pltpu.VMEM((heads, 1, LANES), F32),
        ],
        compiler_params=_params("parallel", "arbitrary"),
        name="mlstm_scan",
    )(qt, k, vt, opre, gates, norm_h.reshape(1, dv_all))


def kernel(x, norm_mix_pre, norm_mix_post, norm_ffn_pre, norm_ffn_post, w_up, w_down,
           attn_w_qkv, attn_w_o, mlstm_w_in, mlstm_b_gates, mlstm_norm_h, mlstm_w_out):
    bsz, seq, d = x.shape
    m = bsz * seq
    heads = ATT_HEADS
    mh = MLSTM_HEADS
    dv = d // mh
    dqk = dv // 2

    wqkv = attn_w_qkv[0].astype(BF16)
    wq, wk, wvt = wqkv[:, :d], wqkv[:, d:2 * d], wqkv[:, 2 * d:].T
    q, k, km, vt = _qkv_proj(x, norm_mix_pre[0], wq, wk, wvt, heads=heads, blk=MOBA_BLOCK,
                             tm=2 * MOBA_BLOCK)
    slopes = jnp.exp2(-8.0 * jnp.arange(1, heads + 1, dtype=F32) / heads)
    att = _moba_attention(q, k, vt, km, slopes, heads=heads, blk=MOBA_BLOCK, top=MOBA_TOPK, cb=4)
    h = _proj_res_norm(att.reshape(m, d), attn_w_o[0].astype(BF16), x.reshape(m, d),
                       norm_mix_post[0], tm=512)
    h = _mlp(h, norm_ffn_pre[0], w_up[0].astype(BF16), w_down[0].astype(BF16),
             norm_ffn_post[0], tm=1024, tf=1024)

    w_in = mlstm_w_in[0].astype(BF16)
    nq = mh * dqk
    wqt = w_in[:, :nq].T
    wk = w_in[:, nq:2 * nq]
    wvt = w_in[:, 2 * nq:2 * nq + d].T
    wo = w_in[:, 2 * nq + d:2 * nq + 2 * d]
    wg = jnp.pad(w_in[:, 2 * nq + 2 * d:], ((0, 0), (0, LANES - 2 * mh)))
    bg = jnp.pad(mlstm_b_gates[0], (0, LANES - 2 * mh)).reshape(1, LANES)
    qt, kmm, vmt, opre, gates = _mlstm_in_proj(h, norm_mix_pre[1], wqt, wk, wvt, wo, wg, bg,
                                               bsz=bsz, heads=mh, dqk=dqk, tm=512)
    y = _mlstm_scan(qt, kmm.reshape(bsz, seq, nq), vmt, opre.reshape(bsz, seq, d),
                    gates.reshape(bsz, seq, LANES), mlstm_norm_h[0], heads=mh, chunk=256)
    h = _proj_res_norm(y.reshape(m, d), mlstm_w_out[0].astype(BF16), h, norm_mix_post[1], tm=512)
    h = _mlp(h, norm_ffn_pre[1], w_up[1].astype(BF16), w_down[1].astype(BF16),
             norm_ffn_post[1], tm=1024, tf=1024)
    return h.reshape(bsz, seq, d)
```

```python
import functools

import jax
import jax.numpy as jnp
from jax import lax
from jax.experimental import pallas as pl
from jax.experimental.pallas import tpu as pltpu

F32 = jnp.float32
BF16 = jnp.bfloat16

RMS_EPS = 1e-6
ATT_HEADS = 8
MOBA_BLOCK = 256
MOBA_TOPK = 3
MLSTM_HEADS = 8
GATE_SOFTCAP = 15.0

LANES = 128
VMEM_LIMIT = 56 * 1024 * 1024
NEG_BIG = -1e30
POS_BIG = 1e30
LOG2E = 1.4426950408889634
BF16_ROWS = 16
MOBA_DEPTH = 4
MOBA_ROUNDS = 2

_NT = (((1,), (1,)), ((), ()))


def _params(*sem):
    return pltpu.CompilerParams(dimension_semantics=sem, vmem_limit_bytes=VMEM_LIMIT)


def _rms_scale(x, gain):
    ms = jnp.mean(x * x, axis=-1, keepdims=True)
    return x * lax.rsqrt(ms + RMS_EPS) * gain


def _qkv_kernel(x_ref, g_ref, wq_ref, wk_ref, wvt_ref, q_ref, k_ref, km_ref, vt_ref,
                *, scale, blk, heads):
    xn = _rms_scale(x_ref[...], g_ref[...]).astype(BF16)
    tm = xn.shape[0]
    dh = xn.shape[1] // heads
    q = jnp.dot(xn, wq_ref[...], preferred_element_type=F32)
    q_ref[...] = (q * scale).astype(BF16)
    k = jnp.dot(xn, wk_ref[...], preferred_element_type=F32)
    k_ref[...] = k.astype(BF16)
    for c in range(tm // blk):
        km_ref[c] = jnp.mean(k[c * blk:(c + 1) * blk], axis=0, keepdims=True)
    vt = lax.dot_general(wvt_ref[...], xn, _NT, preferred_element_type=F32)
    ones_row = (lax.broadcasted_iota(jnp.int32, (BF16_ROWS, blk), 0) == 0).astype(BF16)
    for h in range(heads):
        for c in range(tm // blk):
            vt_ref[0, h, c, :dh, :] = vt[h * dh:(h + 1) * dh, c * blk:(c + 1) * blk].astype(BF16)
            vt_ref[0, h, c, dh:, :] = ones_row


def _qkv_proj(x, gain, wq, wk, wvt, *, heads, blk, tm):
    bsz, seq, d = x.shape
    dh = d // heads
    nblk = seq // blk
    m = bsz * seq
    cpt = tm // blk
    tiles_per_b = seq // tm
    kern = functools.partial(_qkv_kernel, scale=dh ** -0.5 * LOG2E, blk=blk, heads=heads)
    const = lambda i: (0, 0)
    q, k, km, vt = pl.pallas_call(
        kern,
        grid=(m // tm,),
        in_specs=[
            pl.BlockSpec((tm, d), lambda i: (i, 0)),
            pl.BlockSpec((1, d), const),
            pl.BlockSpec((d, d), const),
            pl.BlockSpec((d, d), const),
            pl.BlockSpec((d, d), const),
        ],
        out_specs=[
            pl.BlockSpec((tm, d), lambda i: (i, 0)),
            pl.BlockSpec((tm, d), lambda i: (i, 0)),
            pl.BlockSpec((cpt, 1, d), lambda i: (i, 0, 0)),
            pl.BlockSpec((1, heads, cpt, dh + BF16_ROWS, blk),
                         lambda i: (i // tiles_per_b, 0, i % tiles_per_b, 0, 0)),
        ],
        out_shape=[
            jax.ShapeDtypeStruct((m, d), BF16),
            jax.ShapeDtypeStruct((m, d), BF16),
            jax.ShapeDtypeStruct((m // blk, 1, d), F32),
            jax.ShapeDtypeStruct((bsz, heads, nblk, dh + BF16_ROWS, blk), BF16),
        ],
        compiler_params=_params("parallel"),
        name="qkv_proj",
    )(x.reshape(m, d), gain.reshape(1, d), wq, wk, wvt)
    return (q.reshape(bsz, seq, d), k.reshape(bsz, seq, d),
            km.reshape(bsz, nblk, d), vt)


def _moba_items(nblk, cb):
    tiles, chunks = [], []
    for i in range(nblk):
        for c in range(-(-(i + 1) // cb)):
            tiles.append(i)
            chunks.append(c)
    while len(tiles) % (MOBA_DEPTH * MOBA_ROUNDS):
        assert nblk >= 2 * cb
        tiles.append(0)
        chunks.append(1)
    n_items = len(tiles)
    tiles += [0] * (MOBA_DEPTH - 1)
    chunks += [0] * (MOBA_DEPTH - 1)
    return n_items, tiles, chunks


def _moba_kernel(slope_ref, tile_ref, chunk_ref, q_ref, k_ref, vt_ref, km_ref, o_ref,
                 sel_ref, bias_ref, m_ref, acc_ref, cmax_ref, *s_refs,
                 blk, nblk, top, cb, n_items, qc, dh):
    slope = slope_ref[pl.program_id(1)] * LOG2E

    kr = lax.broadcasted_iota(jnp.int32, (blk, blk), 0)
    qq = lax.broadcasted_iota(jnp.int32, (blk, blk), 1)
    base = kr.astype(F32) * slope
    bias_ref[0] = base
    bias_ref[1] = jnp.where(kr <= qq, base, NEG_BIG)

    m_ref[...] = jnp.full(m_ref.shape, NEG_BIG, F32)
    acc_ref[...] = jnp.zeros(acc_ref.shape, F32)

    km = km_ref[0]
    km_hi = km.astype(BF16)
    km_lo = (km - km_hi.astype(F32)).astype(BF16)
    row = lax.broadcasted_iota(jnp.int32, (nblk, qc), 0)
    rowf = row.astype(F32)
    tiles_per_qc = qc // blk

    def select(u, carry):
        q0 = pl.multiple_of(u * qc, qc)
        qs = q_ref[0, pl.ds(q0, qc), :]
        gate = (lax.dot_general(km_hi, qs, _NT, preferred_element_type=F32)
                + lax.dot_general(km_lo, qs, _NT, preferred_element_type=F32))
        qblk = u * tiles_per_qc + lax.broadcasted_iota(jnp.int32, (nblk, qc), 1) // blk
        past = row < qblk
        g = jnp.where(past, gate, -jnp.inf)
        picked = jnp.zeros((nblk, qc), F32)
        for _ in range(top):
            mx = jnp.max(g, axis=0, keepdims=True)
            first = jnp.min(jnp.where(g == mx, rowf, float(nblk)), axis=0, keepdims=True)
            hit = rowf == first
            picked = jnp.where(hit, 1.0, picked)
            g = jnp.where(hit, -jnp.inf, g)
        selv = jnp.where(((picked > 0.5) & past) | (row == qblk), 1.0, 0.0)
        for t in range(tiles_per_qc):
            sel_ref[u * tiles_per_qc + t] = selv[:, t * blk:(t + 1) * blk]
        return carry

    lax.fori_loop(0, (nblk * blk) // qc, select, 0, unroll=2)

    def scores(item, slot):
        i = tile_ref[item]
        c = chunk_ref[item]
        qi = q_ref[0, pl.ds(pl.multiple_of(i * blk, blk), blk), :]
        for jb in range(cb):
            j = c * cb + jb
            kb = k_ref[0, pl.ds(pl.multiple_of(j * blk, blk), blk), :]
            sb = (lax.dot_general(kb, qi, _NT, preferred_element_type=F32)
                  + bias_ref[jnp.where(j == i, 1, 0)])
            s_refs[slot][jb * blk:(jb + 1) * blk, :] = sb
            cmax_ref[slot, jb] = jnp.max(sb, axis=0, keepdims=True)

    def softmax_pv(item, slot):
        i = tile_ref[item]
        c = chunk_ref[item]
        m_old = m_ref[i]
        sels, cjs = [], []
        m_new = m_old
        for jb in range(cb):
            j = c * cb + jb
            cj = ((j - i) * blk).astype(F32) * slope
            sel = sel_ref[i, pl.ds(j, 1), :] > 0.5
            m_new = jnp.maximum(m_new, jnp.where(sel, cmax_ref[slot, jb] + cj, NEG_BIG))
            sels.append(sel)
            cjs.append(cj)
        alpha = jnp.exp2(m_old - m_new)
        pv = None
        for jb in range(cb):
            shift = jnp.where(sels[jb], m_new - cjs[jb], POS_BIG)
            p = jnp.exp2(s_refs[slot][jb * blk:(jb + 1) * blk, :] - shift)
            d = jnp.dot(vt_ref[0, 0, c * cb + jb], p.astype(BF16), preferred_element_type=F32)
            pv = d if pv is None else pv + d
        acc_ref[i] = alpha * acc_ref[i] + pv
        m_ref[i] = m_new

    for slot in range(MOBA_DEPTH - 1):
        scores(slot, slot)

    def group(kk, carry):
        for u in range(MOBA_DEPTH * MOBA_ROUNDS):
            t = MOBA_DEPTH * MOBA_ROUNDS * kk + u
            scores(t + MOBA_DEPTH - 1, (u + MOBA_DEPTH - 1) % MOBA_DEPTH)
            softmax_pv(t, u % MOBA_DEPTH)
        return carry

    lax.fori_loop(0, n_items // (MOBA_DEPTH * MOBA_ROUNDS), group, 0)

    def finish(i, carry):
        acc = acc_ref[i]
        o = acc[:dh] * (1.0 / acc[dh:dh + 1])
        o_ref[0, pl.ds(pl.multiple_of(i * blk, blk), blk), :] = o.T.astype(BF16)
        return carry

    lax.fori_loop(0, nblk, finish, 0, unroll=4)


def _moba_attention(q, k, vt, km, slopes, *, heads, blk, top, cb):
    bsz, seq, d = q.shape
    dh = d // heads
    nblk = seq // blk
    assert nblk % cb == 0
    qc = min(seq, 4 * blk)
    n_items, tiles, chunks = _moba_items(nblk, cb)
    kern = functools.partial(_moba_kernel, blk=blk, nblk=nblk, top=min(top, nblk), cb=cb,
                             n_items=n_items, qc=qc, dh=dh)
    smem = pl.BlockSpec(memory_space=pltpu.SMEM)
    return pl.pallas_call(
        kern,
        grid=(bsz, heads),
        in_specs=[
            smem, smem, smem,
            pl.BlockSpec((1, seq, dh), lambda b, h: (b, 0, h)),
            pl.BlockSpec((1, seq, dh), lambda b, h: (b, 0, h)),
            pl.BlockSpec((1, 1, nblk, dh + BF16_ROWS, blk), lambda b, h: (b, h, 0, 0, 0)),
            pl.BlockSpec((1, nblk, dh), lambda b, h: (b, 0, h)),
        ],
        out_specs=pl.BlockSpec((1, seq, dh), lambda b, h: (b, 0, h)),
        out_shape=jax.ShapeDtypeStruct((bsz, seq, d), BF16),
        scratch_shapes=[
            pltpu.VMEM((nblk, nblk, blk), F32),
            pltpu.VMEM((2, blk, blk), F32),
            pltpu.VMEM((nblk, 1, blk), F32),
            pltpu.VMEM((nblk, dh + BF16_ROWS, blk), F32),
            pltpu.VMEM((MOBA_DEPTH, cb, 1, blk), F32),
        ] + [pltpu.VMEM((cb * blk, blk), F32)] * MOBA_DEPTH,
        compiler_params=_params("parallel", "parallel"),
        name="moba_attention",
    )(slopes, jnp.asarray(tiles, jnp.int32), jnp.asarray(chunks, jnp.int32), q, k, vt, km)


def _proj_res_norm_kernel(a_ref, w_ref, h_ref, g_ref, o_ref):
    u = jnp.dot(a_ref[...], w_ref[...], preferred_element_type=F32)
    o_ref[...] = h_ref[...] + _rms_scale(u, g_ref[...])


def _proj_res_norm(a, w, h, gain, *, tm):
    m, kdim = a.shape
    d = w.shape[1]
    return pl.pallas_call(
        _proj_res_norm_kernel,
        grid=(m // tm,),
        in_specs=[
            pl.BlockSpec((tm, kdim), lambda i: (i, 0)),
            pl.BlockSpec((kdim, d), lambda i: (0, 0)),
            pl.BlockSpec((tm, d), lambda i: (i, 0)),
            pl.BlockSpec((1, d), lambda i: (0, 0)),
        ],
        out_specs=pl.BlockSpec((tm, d), lambda i: (i, 0)),
        out_shape=jax.ShapeDtypeStruct((m, d), F32),
        compiler_params=_params("parallel"),
        name="proj_res_norm",
    )(a, w, h, gain.reshape(1, d))


def _mlp_kernel(h_ref, g1_ref, wu_ref, wd_ref, g2_ref, o_ref, xn_ref, acc_ref):
    f = pl.program_id(1)

    @pl.when(f == 0)
    def _():
        xn_ref[...] = _rms_scale(h_ref[...], g1_ref[...]).astype(BF16)
        acc_ref[...] = jnp.zeros_like(acc_ref)

    a = jnp.dot(xn_ref[...], wu_ref[...], preferred_element_type=F32)
    a = jnp.maximum(a, 0.0)
    acc_ref[...] += jnp.dot((a * a).astype(BF16), wd_ref[...], preferred_element_type=F32)

    @pl.when(f == pl.num_programs(1) - 1)
    def _():
        o_ref[...] = h_ref[...] + _rms_scale(acc_ref[...], g2_ref[...])


def _mlp(h, g_pre, w_up, w_down, g_post, *, layer, tm, tf):
    m, d = h.shape
    ff = w_up.shape[2]
    return pl.pallas_call(
        _mlp_kernel,
        grid=(m // tm, ff // tf),
        in_specs=[
            pl.BlockSpec((tm, d), lambda i, f: (i, 0)),
            pl.BlockSpec((1, d), lambda i, f: (0, 0)),
            pl.BlockSpec((None, d, tf), lambda i, f: (layer, 0, f)),
            pl.BlockSpec((None, tf, d), lambda i, f: (layer, f, 0)),
            pl.BlockSpec((1, d), lambda i, f: (0, 0)),
        ],
        out_specs=pl.BlockSpec((tm, d), lambda i, f: (i, 0)),
        out_shape=jax.ShapeDtypeStruct((m, d), F32),
        scratch_shapes=[pltpu.VMEM((tm, d), BF16), pltpu.VMEM((tm, d), F32)],
        compiler_params=_params("parallel", "arbitrary"),
        name="mlp",
    )(h, g_pre.reshape(1, d), w_up, w_down, g_post.reshape(1, d))


def _mlstm_in_kernel(x_ref, g_ref, wqt_ref, wk_ref, wvt_ref, wo_ref, wg_ref, bg_ref,
                     qt_ref, k_ref, vt_ref, op_ref, gates_ref, *, heads, kscale):
    xn = _rms_scale(x_ref[...], g_ref[...]).astype(BF16)
    qt_ref[0] = lax.dot_general(wqt_ref[...], xn, _NT, preferred_element_type=F32).astype(BF16)
    k_ref[...] = (jnp.dot(xn, wk_ref[...], preferred_element_type=F32) * kscale).astype(BF16)
    vt_ref[0] = lax.dot_general(wvt_ref[...], xn, _NT, preferred_element_type=F32).astype(BF16)
    op_ref[...] = jnp.dot(xn, wo_ref[...], preferred_element_type=F32).astype(BF16)
    z = jnp.dot(xn, wg_ref[...], preferred_element_type=F32) + bg_ref[...]
    z = GATE_SOFTCAP * jnp.tanh(z * (1.0 / GATE_SOFTCAP))
    log_f = jnp.minimum(z, 0.0) - jnp.log1p(jnp.exp(-jnp.abs(z)))
    lane = lax.broadcasted_iota(jnp.int32, z.shape, 1)
    gates_ref[...] = jnp.where(lane < heads, z, log_f)


def _mlstm_in_proj(x, gain, wqt, wk, wvt, wo, wg, bg, *, bsz, heads, dqk, tm):
    m, d = x.shape
    seq = m // bsz
    nq = wqt.shape[0]
    dv_all = wvt.shape[0]
    tiles_per_b = seq // tm
    kern = functools.partial(_mlstm_in_kernel, heads=heads, kscale=dqk ** -0.5)
    const = lambda i: (0, 0)
    row = lambda i: (i, 0)
    tcol = lambda i: (i // tiles_per_b, 0, i % tiles_per_b)
    return pl.pallas_call(
        kern,
        grid=(m // tm,),
        in_specs=[
            pl.BlockSpec((tm, d), row),
            pl.BlockSpec((1, d), const),
            pl.BlockSpec((nq, d), const),
            pl.BlockSpec((d, nq), const),
            pl.BlockSpec((dv_all, d), const),
            pl.BlockSpec((d, dv_all), const),
            pl.BlockSpec((d, LANES), const),
            pl.BlockSpec((1, LANES), const),
        ],
        out_specs=[
            pl.BlockSpec((1, nq, tm), tcol),
            pl.BlockSpec((tm, nq), row),
            pl.BlockSpec((1, dv_all, tm), tcol),
            pl.BlockSpec((tm, dv_all), row),
            pl.BlockSpec((tm, LANES), row),
        ],
        out_shape=[
            jax.ShapeDtypeStruct((bsz, nq, seq), BF16),
            jax.ShapeDtypeStruct((m, nq), BF16),
            jax.ShapeDtypeStruct((bsz, dv_all, seq), BF16),
            jax.ShapeDtypeStruct((m, dv_all), BF16),
            jax.ShapeDtypeStruct((m, LANES), F32),
        ],
        compiler_params=_params("parallel"),
        name="mlstm_in_proj",
    )(x, gain.reshape(1, d), wqt, wk, wvt, wo, wg, bg)


def _mlstm_kernel(qt_ref, k_ref, vt_ref, op_ref, g_ref, nh_ref, y_ref, c_ref, m_ref,
                  *, heads, dqk, dv, chunk):
    @pl.when(pl.program_id(1) == 0)
    def _():
        c_ref[...] = jnp.zeros_like(c_ref)
        m_ref[...] = jnp.zeros_like(m_ref)

    L = chunk
    gates_t = g_ref[0].T
    ss = lax.broadcasted_iota(jnp.int32, (L, L), 0)
    tt = lax.broadcasted_iota(jnp.int32, (L, L), 1)
    causal = ss <= tt
    i_rows = gates_t[0:heads]
    f_rows = gates_t[heads:2 * heads]
    b_rows = jnp.dot(f_rows, causal.astype(F32), preferred_element_type=F32,
                     precision=lax.Precision.HIGHEST)
    u_rows = i_rows - b_rows
    u_cols = jnp.concatenate([u_rows, jnp.zeros((LANES - heads, L), F32)], axis=0).T
    ones_rows = (lax.broadcasted_iota(jnp.int32, (BF16_ROWS, L), 0) == 0).astype(BF16)
    sub = lax.broadcasted_iota(jnp.int32, (2 * dqk, 1), 0)
    lane = lax.broadcasted_iota(jnp.int32, (1, 2 * dqk), 1)

    for p in range(heads // 2):
        qt_pair = qt_ref[0, p * 2 * dqk:(p + 1) * 2 * dqk, :]
        kg = k_ref[0, :, p * 2 * dqk:(p + 1) * 2 * dqk]
        ct = c_ref[p]
        ct_b = ct.astype(BF16)
        upds, decays = [], []
        for half in range(2):
            h = 2 * p + half
            in_head = (sub >= half * dqk) & (sub < (half + 1) * dqk)
            qt_h = jnp.where(in_head, qt_pair, jnp.zeros_like(qt_pair))
            b_row = b_rows[h:h + 1]
            i_row = i_rows[h:h + 1]
            m_prev = m_ref[h][:, 0:1]
            log_d = jnp.where(causal, u_cols[:, h:h + 1] + b_row, -jnp.inf)
            log_inter = b_row + m_prev
            m_t = jnp.maximum(log_inter, jnp.max(log_d, axis=0, keepdims=True))
            dmat = jnp.exp(log_d - m_t)
            w_inter = jnp.exp(log_inter - m_t)
            s_t = jnp.dot(kg, qt_h, preferred_element_type=F32) * dmat
            vext = jnp.concatenate([vt_ref[0, h * dv:(h + 1) * dv, :], ones_rows], axis=0)
            num = (w_inter * jnp.dot(ct_b, qt_h, preferred_element_type=F32)
                   + jnp.dot(vext, s_t.astype(BF16), preferred_element_type=F32))
            den = num[dv:dv + 1]
            hout = num[:dv] * (1.0 / jnp.maximum(jnp.abs(den), jnp.exp(-m_t)))
            hn = hout * lax.rsqrt(jnp.mean(hout * hout, axis=0, keepdims=True) + RMS_EPS)
            og = jax.nn.sigmoid(op_ref[0, :, h * dv:(h + 1) * dv].astype(F32))
            y_ref[0, :, h * dv:(h + 1) * dv] = (
                og * (hn.T * nh_ref[:, h * dv:(h + 1) * dv])).astype(BF16)
            b_end = b_row[:, L - 1:L]
            log_w = b_end - b_row + i_row
            m_new = jnp.maximum(b_end + m_prev, jnp.max(log_w, axis=1, keepdims=True))
            decays.append(jnp.exp(b_end + m_prev - m_new))
            vw = (vext.astype(F32) * jnp.exp(log_w - m_new)).astype(BF16)
            upds.append(jnp.dot(vw, kg, preferred_element_type=F32))
            m_ref[h] = jnp.broadcast_to(m_new, (1, LANES))
        first = lane < dqk
        c_ref[p] = jnp.where(first, decays[0], decays[1]) * ct + jnp.where(first, upds[0], upds[1])


def _mlstm_scan(qt, k, vt, opre, gates, norm_h, *, heads, chunk):
    bsz, dv_all, seq = vt.shape
    dv = dv_all // heads
    dqk = qt.shape[1] // heads
    kern = functools.partial(_mlstm_kernel, heads=heads, dqk=dqk, dv=dv, chunk=chunk)
    rows = lambda b, c: (b, c, 0)
    cols = lambda b, c: (b, 0, c)
    return pl.pallas_call(
        kern,
        grid=(bsz, seq // chunk),
        in_specs=[
            pl.BlockSpec((1, heads * dqk, chunk), cols),
            pl.BlockSpec((1, chunk, heads * dqk), rows),
            pl.BlockSpec((1, dv_all, chunk), cols),
            pl.BlockSpec((1, chunk, dv_all), rows),
            pl.BlockSpec((1, chunk, LANES), rows),
            pl.BlockSpec((1, dv_all), lambda b, c: (0, 0)),
        ],
        out_specs=pl.BlockSpec((1, chunk, dv_all), rows),
        out_shape=jax.ShapeDtypeStruct((bsz, seq, dv_all), BF16),
        scratch_shapes=[
            pltpu.VMEM((heads // 2, dv + BF16_ROWS, 2 * dqk), F32),
            pltpu.VMEM((heads, 1, LANES), F32),
        ],
        compiler_params=_params("parallel", "arbitrary"),
        name="mlstm_scan",
    )(qt, k, vt, opre, gates, norm_h.reshape(1, dv_all))


def kernel(x, norm_mix_pre, norm_mix_post, norm_ffn_pre, norm_ffn_post, w_up, w_down,
           attn_w_qkv, attn_w_o, mlstm_w_in, mlstm_b_gates, mlstm_norm_h, mlstm_w_out):
    bsz, seq, d = x.shape
    m = bsz * seq
    heads = ATT_HEADS
    mh = MLSTM_HEADS
    dv = d // mh
    dqk = dv // 2

    wqkv = attn_w_qkv[0].astype(BF16)
    wq, wk, wvt = wqkv[:, :d], wqkv[:, d:2 * d], wqkv[:, 2 * d:].T
    q, k, km, vt = _qkv_proj(x, norm_mix_pre[0], wq, wk, wvt, heads=heads, blk=MOBA_BLOCK,
                             tm=2 * MOBA_BLOCK)
    slopes = jnp.exp2(-8.0 * jnp.arange(1, heads + 1, dtype=F32) / heads)
    att = _moba_attention(q, k, vt, km, slopes, heads=heads, blk=MOBA_BLOCK, top=MOBA_TOPK, cb=4)
    h = _proj_res_norm(att.reshape(m, d), attn_w_o[0].astype(BF16), x.reshape(m, d),
                       norm_mix_post[0], tm=512)
    w_up_b = w_up.astype(BF16)
    w_down_b = w_down.astype(BF16)
    h = _mlp(h, norm_ffn_pre[0], w_up_b, w_down_b, norm_ffn_post[0], layer=0, tm=1024, tf=1024)

    w_in = mlstm_w_in[0].astype(BF16)
    nq = mh * dqk
    wqt = w_in[:, :nq].T
    wk = w_in[:, nq:2 * nq]
    wvt = w_in[:, 2 * nq:2 * nq + d].T
    wo = w_in[:, 2 * nq + d:2 * nq + 2 * d]
    wg = jnp.pad(w_in[:, 2 * nq + 2 * d:], ((0, 0), (0, LANES - 2 * mh)))
    bg = jnp.pad(mlstm_b_gates[0], (0, LANES - 2 * mh)).reshape(1, LANES)
    qt, kmm, vmt, opre, gates = _mlstm_in_proj(h, norm_mix_pre[1], wqt, wk, wvt, wo, wg, bg,
                                               bsz=bsz, heads=mh, dqk=dqk, tm=512)
    y = _mlstm_scan(qt, kmm.reshape(bsz, seq, nq), vmt, opre.reshape(bsz, seq, d),
                    gates.reshape(bsz, seq, LANES), mlstm_norm_h[0], heads=mh, chunk=256)
    h = _proj_res_norm(y.reshape(m, d), mlstm_w_out[0].astype(BF16), h, norm_mix_post[1], tm=512)
    h = _mlp(h, norm_ffn_pre[1], w_up_b, w_down_b, norm_ffn_post[1], layer=1, tm=1024, tf=1024)
    return h.reshape(bsz, seq, d)
```

```python
import functools

import jax
import jax.numpy as jnp
from jax import lax
from jax.experimental import pallas as pl
from jax.experimental.pallas import tpu as pltpu

F32 = jnp.float32
BF16 = jnp.bfloat16

RMS_EPS = 1e-6
ATT_HEADS = 8
MOBA_BLOCK = 256
MOBA_TOPK = 3
MLSTM_HEADS = 8
GATE_SOFTCAP = 15.0

LANES = 128
VMEM_LIMIT = 56 * 1024 * 1024
NEG_BIG = -1e30
POS_BIG = 1e30
LOG2E = 1.4426950408889634
BF16_ROWS = 16
MOBA_DEPTH = 8
MOBA_ROUNDS = 2

_NT = (((1,), (1,)), ((), ()))


def _params(*sem):
    return pltpu.CompilerParams(dimension_semantics=sem, vmem_limit_bytes=VMEM_LIMIT)


def _rms_scale(x, gain):
    ms = jnp.mean(x * x, axis=-1, keepdims=True)
    return x * lax.rsqrt(ms + RMS_EPS) * gain


def _qkv_kernel(x_ref, g_ref, wq_ref, wk_ref, wvt_ref, q_ref, k_ref, km_ref, vt_ref,
                *, scale, blk, heads):
    xn = _rms_scale(x_ref[...], g_ref[...]).astype(BF16)
    tm = xn.shape[0]
    dh = xn.shape[1] // heads
    q = jnp.dot(xn, wq_ref[...], preferred_element_type=F32)
    q_ref[...] = (q * scale).astype(BF16)
    k = jnp.dot(xn, wk_ref[...], preferred_element_type=F32)
    k_ref[...] = k.astype(BF16)
    for c in range(tm // blk):
        km_ref[c] = jnp.mean(k[c * blk:(c + 1) * blk], axis=0, keepdims=True)
    vt = lax.dot_general(wvt_ref[...], xn, _NT, preferred_element_type=F32)
    ones_row = (lax.broadcasted_iota(jnp.int32, (BF16_ROWS, blk), 0) == 0).astype(BF16)
    for h in range(heads):
        for c in range(tm // blk):
            vt_ref[0, h, c, :dh, :] = vt[h * dh:(h + 1) * dh, c * blk:(c + 1) * blk].astype(BF16)
            vt_ref[0, h, c, dh:, :] = ones_row


def _qkv_proj(x, gain, wq, wk, wvt, *, heads, blk, tm):
    bsz, seq, d = x.shape
    dh = d // heads
    nblk = seq // blk
    m = bsz * seq
    cpt = tm // blk
    tiles_per_b = seq // tm
    kern = functools.partial(_qkv_kernel, scale=dh ** -0.5 * LOG2E, blk=blk, heads=heads)
    const = lambda i: (0, 0)
    q, k, km, vt = pl.pallas_call(
        kern,
        grid=(m // tm,),
        in_specs=[
            pl.BlockSpec((tm, d), lambda i: (i, 0)),
            pl.BlockSpec((1, d), const),
            pl.BlockSpec((d, d), const),
            pl.BlockSpec((d, d), const),
            pl.BlockSpec((d, d), const),
        ],
        out_specs=[
            pl.BlockSpec((tm, d), lambda i: (i, 0)),
            pl.BlockSpec((tm, d), lambda i: (i, 0)),
            pl.BlockSpec((cpt, 1, d), lambda i: (i, 0, 0)),
            pl.BlockSpec((1, heads, cpt, dh + BF16_ROWS, blk),
                         lambda i: (i // tiles_per_b, 0, i % tiles_per_b, 0, 0)),
        ],
        out_shape=[
            jax.ShapeDtypeStruct((m, d), BF16),
            jax.ShapeDtypeStruct((m, d), BF16),
            jax.ShapeDtypeStruct((m // blk, 1, d), F32),
            jax.ShapeDtypeStruct((bsz, heads, nblk, dh + BF16_ROWS, blk), BF16),
        ],
        compiler_params=_params("parallel"),
        name="qkv_proj",
    )(x.reshape(m, d), gain.reshape(1, d), wq, wk, wvt)
    return (q.reshape(bsz, seq, d), k.reshape(bsz, seq, d),
            km.reshape(bsz, nblk, d), vt)


def _moba_items(nblk, cb):
    tiles, chunks = [], []
    for i in range(nblk):
        for c in range(-(-(i + 1) // cb)):
            tiles.append(i)
            chunks.append(c)
    while len(tiles) % (MOBA_DEPTH * MOBA_ROUNDS):
        assert nblk >= 2 * cb
        tiles.append(0)
        chunks.append(1)
    n_items = len(tiles)
    tiles += [0] * (MOBA_DEPTH - 1)
    chunks += [0] * (MOBA_DEPTH - 1)
    return n_items, tiles, chunks


def _moba_kernel(slope_ref, tile_ref, chunk_ref, q_ref, k_ref, vt_ref, km_ref, o_ref,
                 sel_ref, bias_ref, m_ref, acc_ref, cmax_ref, *s_refs,
                 blk, nblk, top, cb, n_items, qc, dh):
    slope = slope_ref[pl.program_id(1)] * LOG2E

    kr = lax.broadcasted_iota(jnp.int32, (blk, blk), 0)
    qq = lax.broadcasted_iota(jnp.int32, (blk, blk), 1)
    base = kr.astype(F32) * slope
    bias_ref[0] = base
    bias_ref[1] = jnp.where(kr <= qq, base, NEG_BIG)

    m_ref[...] = jnp.full(m_ref.shape, NEG_BIG, F32)
    acc_ref[...] = jnp.zeros(acc_ref.shape, F32)

    km = km_ref[0]
    km_hi = km.astype(BF16)
    km_lo = (km - km_hi.astype(F32)).astype(BF16)
    row = lax.broadcasted_iota(jnp.int32, (nblk, qc), 0)
    rowf = row.astype(F32)
    tiles_per_qc = qc // blk

    def select(u, carry):
        q0 = pl.multiple_of(u * qc, qc)
        qs = q_ref[0, pl.ds(q0, qc), :]
        gate = (lax.dot_general(km_hi, qs, _NT, preferred_element_type=F32)
                + lax.dot_general(km_lo, qs, _NT, preferred_element_type=F32))
        qblk = u * tiles_per_qc + lax.broadcasted_iota(jnp.int32, (nblk, qc), 1) // blk
        past = row < qblk
        g = jnp.where(past, gate, -jnp.inf)
        picked = jnp.zeros((nblk, qc), F32)
        for _ in range(top):
            mx = jnp.max(g, axis=0, keepdims=True)
            first = jnp.min(jnp.where(g == mx, rowf, float(nblk)), axis=0, keepdims=True)
            hit = rowf == first
            picked = jnp.where(hit, 1.0, picked)
            g = jnp.where(hit, -jnp.inf, g)
        selv = jnp.where(((picked > 0.5) & past) | (row == qblk), 1.0, 0.0)
        for t in range(tiles_per_qc):
            sel_ref[u * tiles_per_qc + t] = selv[:, t * blk:(t + 1) * blk]
        return carry

    lax.fori_loop(0, (nblk * blk) // qc, select, 0, unroll=2)

    def scores(item, slot):
        i = tile_ref[item]
        c = chunk_ref[item]
        qi = q_ref[0, pl.ds(pl.multiple_of(i * blk, blk), blk), :]
        for jb in range(cb):
            j = c * cb + jb
            kb = k_ref[0, pl.ds(pl.multiple_of(j * blk, blk), blk), :]
            sb = (lax.dot_general(kb, qi, _NT, preferred_element_type=F32)
                  + bias_ref[jnp.where(j == i, 1, 0)])
            s_refs[slot][jb * blk:(jb + 1) * blk, :] = sb
            cmax_ref[slot, jb] = jnp.max(sb, axis=0, keepdims=True)

    def softmax_pv(item, slot):
        i = tile_ref[item]
        c = chunk_ref[item]
        m_old = m_ref[i]
        sels, cjs = [], []
        m_new = m_old
        for jb in range(cb):
            j = c * cb + jb
            cj = ((j - i) * blk).astype(F32) * slope
            sel = sel_ref[i, pl.ds(j, 1), :] > 0.5
            m_new = jnp.maximum(m_new, jnp.where(sel, cmax_ref[slot, jb] + cj, NEG_BIG))
            sels.append(sel)
            cjs.append(cj)
        alpha = jnp.exp2(m_old - m_new)
        pv = None
        for jb in range(cb):
            shift = jnp.where(sels[jb], m_new - cjs[jb], POS_BIG)
            p = jnp.exp2(s_refs[slot][jb * blk:(jb + 1) * blk, :] - shift)
            d = jnp.dot(vt_ref[0, 0, c * cb + jb], p.astype(BF16), preferred_element_type=F32)
            pv = d if pv is None else pv + d
        acc_ref[i] = alpha * acc_ref[i] + pv
        m_ref[i] = m_new

    for slot in range(MOBA_DEPTH - 1):
        scores(slot, slot)

    def group(kk, carry):
        for u in range(MOBA_DEPTH * MOBA_ROUNDS):
            t = MOBA_DEPTH * MOBA_ROUNDS * kk + u
            scores(t + MOBA_DEPTH - 1, (u + MOBA_DEPTH - 1) % MOBA_DEPTH)
            softmax_pv(t, u % MOBA_DEPTH)
        return carry

    lax.fori_loop(0, n_items // (MOBA_DEPTH * MOBA_ROUNDS), group, 0)

    def finish(i, carry):
        acc = acc_ref[i]
        o = acc[:dh] * (1.0 / acc[dh:dh + 1])
        o_ref[0, pl.ds(pl.multiple_of(i * blk, blk), blk), :] = o.T.astype(BF16)
        return carry

    lax.fori_loop(0, nblk, finish, 0, unroll=4)


def _moba_attention(q, k, vt, km, slopes, *, heads, blk, top, cb):
    bsz, seq, d = q.shape
    dh = d // heads
    nblk = seq // blk
    assert nblk % cb == 0
    qc = min(seq, 4 * blk)
    n_items, tiles, chunks = _moba_items(nblk, cb)
    kern = functools.partial(_moba_kernel, blk=blk, nblk=nblk, top=min(top, nblk), cb=cb,
                             n_items=n_items, qc=qc, dh=dh)
    smem = pl.BlockSpec(memory_space=pltpu.SMEM)
    return pl.pallas_call(
        kern,
        grid=(bsz, heads),
        in_specs=[
            smem, smem, smem,
            pl.BlockSpec((1, seq, dh), lambda b, h: (b, 0, h)),
            pl.BlockSpec((1, seq, dh), lambda b, h: (b, 0, h)),
            pl.BlockSpec((1, 1, nblk, dh + BF16_ROWS, blk), lambda b, h: (b, h, 0, 0, 0)),
            pl.BlockSpec((1, nblk, dh), lambda b, h: (b, 0, h)),
        ],
        out_specs=pl.BlockSpec((1, seq, dh), lambda b, h: (b, 0, h)),
        out_shape=jax.ShapeDtypeStruct((bsz, seq, d), BF16),
        scratch_shapes=[
            pltpu.VMEM((nblk, nblk, blk), F32),
            pltpu.VMEM((2, blk, blk), F32),
            pltpu.VMEM((nblk, 1, blk), F32),
            pltpu.VMEM((nblk, dh + BF16_ROWS, blk), F32),
            pltpu.VMEM((MOBA_DEPTH, cb, 1, blk), F32),
        ] + [pltpu.VMEM((cb * blk, blk), F32)] * MOBA_DEPTH,
        compiler_params=_params("parallel", "parallel"),
        name="moba_attention",
    )(slopes, jnp.asarray(tiles, jnp.int32), jnp.asarray(chunks, jnp.int32), q, k, vt, km)


def _proj_res_norm_kernel(a_ref, w_ref, h_ref, g_ref, o_ref):
    u = jnp.dot(a_ref[...], w_ref[...], preferred_element_type=F32)
    o_ref[...] = h_ref[...] + _rms_scale(u, g_ref[...])


def _proj_res_norm(a, w, h, gain, *, tm):
    m, kdim = a.shape
    d = w.shape[1]
    return pl.pallas_call(
        _proj_res_norm_kernel,
        grid=(m // tm,),
        in_specs=[
            pl.BlockSpec((tm, kdim), lambda i: (i, 0)),
            pl.BlockSpec((kdim, d), lambda i: (0, 0)),
            pl.BlockSpec((tm, d), lambda i: (i, 0)),
            pl.BlockSpec((1, d), lambda i: (0, 0)),
        ],
        out_specs=pl.BlockSpec((tm, d), lambda i: (i, 0)),
        out_shape=jax.ShapeDtypeStruct((m, d), F32),
        compiler_params=_params("parallel"),
        name="proj_res_norm",
    )(a, w, h, gain.reshape(1, d))


def _mlp_kernel(h_ref, g1_ref, wu_ref, wd_ref, g2_ref, o_ref, xn_ref, acc_ref):
    f = pl.program_id(1)

    @pl.when(f == 0)
    def _():
        xn_ref[...] = _rms_scale(h_ref[...], g1_ref[...]).astype(BF16)
        acc_ref[...] = jnp.zeros_like(acc_ref)

    a = jnp.dot(xn_ref[...], wu_ref[...], preferred_element_type=F32)
    a = jnp.maximum(a, 0.0)
    acc_ref[...] += jnp.dot((a * a).astype(BF16), wd_ref[...], preferred_element_type=F32)

    @pl.when(f == pl.num_programs(1) - 1)
    def _():
        o_ref[...] = h_ref[...] + _rms_scale(acc_ref[...], g2_ref[...])


def _mlp(h, g_pre, w_up, w_down, g_post, *, layer, tm, tf):
    m, d = h.shape
    ff = w_up.shape[2]
    return pl.pallas_call(
        _mlp_kernel,
        grid=(m // tm, ff // tf),
        in_specs=[
            pl.BlockSpec((tm, d), lambda i, f: (i, 0)),
            pl.BlockSpec((1, d), lambda i, f: (0, 0)),
            pl.BlockSpec((None, d, tf), lambda i, f: (layer, 0, f)),
            pl.BlockSpec((None, tf, d), lambda i, f: (layer, f, 0)),
            pl.BlockSpec((1, d), lambda i, f: (0, 0)),
        ],
        out_specs=pl.BlockSpec((tm, d), lambda i, f: (i, 0)),
        out_shape=jax.ShapeDtypeStruct((m, d), F32),
        scratch_shapes=[pltpu.VMEM((tm, d), BF16), pltpu.VMEM((tm, d), F32)],
        compiler_params=_params("parallel", "arbitrary"),
        name="mlp",
    )(h, g_pre.reshape(1, d), w_up, w_down, g_post.reshape(1, d))


def _mlstm_in_kernel(x_ref, g_ref, wqt_ref, wk_ref, wvt_ref, wo_ref, wg_ref, bg_ref,
                     qt_ref, k_ref, vt_ref, op_ref, gates_ref, *, heads, kscale):
    xn = _rms_scale(x_ref[...], g_ref[...]).astype(BF16)
    qt_ref[0] = lax.dot_general(wqt_ref[...], xn, _NT, preferred_element_type=F32).astype(BF16)
    k_ref[...] = (jnp.dot(xn, wk_ref[...], preferred_element_type=F32) * kscale).astype(BF16)
    vt_ref[0] = lax.dot_general(wvt_ref[...], xn, _NT, preferred_element_type=F32).astype(BF16)
    op_ref[...] = jnp.dot(xn, wo_ref[...], preferred_element_type=F32).astype(BF16)
    z = jnp.dot(xn, wg_ref[...], preferred_element_type=F32) + bg_ref[...]
    z = GATE_SOFTCAP * jnp.tanh(z * (1.0 / GATE_SOFTCAP))
    log_f = jnp.minimum(z, 0.0) - jnp.log1p(jnp.exp(-jnp.abs(z)))
    lane = lax.broadcasted_iota(jnp.int32, z.shape, 1)
    gates_ref[...] = jnp.where(lane < heads, z, log_f)


def _mlstm_in_proj(x, gain, wqt, wk, wvt, wo, wg, bg, *, bsz, heads, dqk, tm):
    m, d = x.shape
    seq = m // bsz
    nq = wqt.shape[0]
    dv_all = wvt.shape[0]
    tiles_per_b = seq // tm
    kern = functools.partial(_mlstm_in_kernel, heads=heads, kscale=dqk ** -0.5)
    const = lambda i: (0, 0)
    row = lambda i: (i, 0)
    tcol = lambda i: (i // tiles_per_b, 0, i % tiles_per_b)
    return pl.pallas_call(
        kern,
        grid=(m // tm,),
        in_specs=[
            pl.BlockSpec((tm, d), row),
            pl.BlockSpec((1, d), const),
            pl.BlockSpec((nq, d), const),
            pl.BlockSpec((d, nq), const),
            pl.BlockSpec((dv_all, d), const),
            pl.BlockSpec((d, dv_all), const),
            pl.BlockSpec((d, LANES), const),
            pl.BlockSpec((1, LANES), const),
        ],
        out_specs=[
            pl.BlockSpec((1, nq, tm), tcol),
            pl.BlockSpec((tm, nq), row),
            pl.BlockSpec((1, dv_all, tm), tcol),
            pl.BlockSpec((tm, dv_all), row),
            pl.BlockSpec((tm, LANES), row),
        ],
        out_shape=[
            jax.ShapeDtypeStruct((bsz, nq, seq), BF16),
            jax.ShapeDtypeStruct((m, nq), BF16),
            jax.ShapeDtypeStruct((bsz, dv_all, seq), BF16),
            jax.ShapeDtypeStruct((m, dv_all), BF16),
            jax.ShapeDtypeStruct((m, LANES), F32),
        ],
        compiler_params=_params("parallel"),
        name="mlstm_in_proj",
    )(x, gain.reshape(1, d), wqt, wk, wvt, wo, wg, bg)


def _mlstm_kernel(qt_ref, k_ref, vt_ref, op_ref, g_ref, nh_ref, y_ref, c_ref, m_ref,
                  *, heads, dqk, dv, chunk):
    @pl.when(pl.program_id(1) == 0)
    def _():
        c_ref[...] = jnp.zeros_like(c_ref)
        m_ref[...] = jnp.zeros_like(m_ref)

    L = chunk
    gates_t = g_ref[0].T
    ss = lax.broadcasted_iota(jnp.int32, (L, L), 0)
    tt = lax.broadcasted_iota(jnp.int32, (L, L), 1)
    causal = ss <= tt
    i_rows = gates_t[0:heads]
    f_rows = gates_t[heads:2 * heads]
    b_rows = jnp.dot(f_rows, causal.astype(F32), preferred_element_type=F32,
                     precision=lax.Precision.HIGHEST)
    u_rows = i_rows - b_rows
    u_cols = jnp.concatenate([u_rows, jnp.zeros((LANES - heads, L), F32)], axis=0).T
    ones_rows = (lax.broadcasted_iota(jnp.int32, (BF16_ROWS, L), 0) == 0).astype(BF16)
    sub = lax.broadcasted_iota(jnp.int32, (2 * dqk, 1), 0)
    lane = lax.broadcasted_iota(jnp.int32, (1, 2 * dqk), 1)

    for p in range(heads // 2):
        qt_pair = qt_ref[0, p * 2 * dqk:(p + 1) * 2 * dqk, :]
        kg = k_ref[0, :, p * 2 * dqk:(p + 1) * 2 * dqk]
        ct = c_ref[p]
        ct_b = ct.astype(BF16)
        upds, decays = [], []
        for half in range(2):
            h = 2 * p + half
            in_head = (sub >= half * dqk) & (sub < (half + 1) * dqk)
            qt_h = jnp.where(in_head, qt_pair, jnp.zeros_like(qt_pair))
            b_row = b_rows[h:h + 1]
            i_row = i_rows[h:h + 1]
            m_prev = m_ref[h][:, 0:1]
            log_d = jnp.where(causal, u_cols[:, h:h + 1] + b_row, -jnp.inf)
            log_inter = b_row + m_prev
            m_t = jnp.maximum(log_inter, jnp.max(log_d, axis=0, keepdims=True))
            dmat = jnp.exp(log_d - m_t)
            w_inter = jnp.exp(log_inter - m_t)
            s_t = jnp.dot(kg, qt_h, preferred_element_type=F32) * dmat
            vext = jnp.concatenate([vt_ref[0, h * dv:(h + 1) * dv, :], ones_rows], axis=0)
            num = (w_inter * jnp.dot(ct_b, qt_h, preferred_element_type=F32)
                   + jnp.dot(vext, s_t.astype(BF16), preferred_element_type=F32))
            den = num[dv:dv + 1]
            hout = num[:dv] * (1.0 / jnp.maximum(jnp.abs(den), jnp.exp(-m_t)))
            hn = hout * lax.rsqrt(jnp.mean(hout * hout, axis=0, keepdims=True) + RMS_EPS)
            og = jax.nn.sigmoid(op_ref[0, :, h * dv:(h + 1) * dv].astype(F32))
            y_ref[0, :, h * dv:(h + 1) * dv] = (
                og * (hn.T * nh_ref[:, h * dv:(h + 1) * dv])).astype(BF16)
            b_end = b_row[:, L - 1:L]
            log_w = b_end - b_row + i_row
            m_new = jnp.maximum(b_end + m_prev, jnp.max(log_w, axis=1, keepdims=True))
            decays.append(jnp.exp(b_end + m_prev - m_new))
            vw = (vext.astype(F32) * jnp.exp(log_w - m_new)).astype(BF16)
            upds.append(jnp.dot(vw, kg, preferred_element_type=F32))
            m_ref[h] = jnp.broadcast_to(m_new, (1, LANES))
        first = lane < dqk
        c_ref[p] = jnp.where(first, decays[0], decays[1]) * ct + jnp.where(first, upds[0], upds[1])


def _mlstm_scan(qt, k, vt, opre, gates, norm_h, *, heads, chunk):
    bsz, dv_all, seq = vt.shape
    dv = dv_all // heads
    dqk = qt.shape[1] // heads
    kern = functools.partial(_mlstm_kernel, heads=heads, dqk=dqk, dv=dv, chunk=chunk)
    rows = lambda b, c: (b, c, 0)
    cols = lambda b, c: (b, 0, c)
    return pl.pallas_call(
        kern,
        grid=(bsz, seq // chunk),
        in_specs=[
            pl.BlockSpec((1, heads * dqk, chunk), cols),
            pl.BlockSpec((1, chunk, heads * dqk), rows),
            pl.BlockSpec((1, dv_all, chunk), cols),
            pl.BlockSpec((1, chunk, dv_all), rows),
            pl.BlockSpec((1, chunk, LANES), rows),
            pl.BlockSpec((1, dv_all), lambda b, c: (0, 0)),
        ],
        out_specs=pl.BlockSpec((1, chunk, dv_all), rows),
        out_shape=jax.ShapeDtypeStruct((bsz, seq, dv_all), BF16),
        scratch_shapes=[
            pltpu.VMEM((heads // 2, dv + BF16_ROWS, 2 * dqk), F32),
            pltpu.VMEM((heads, 1, LANES), F32),
        ],
        compiler_params=_params("parallel", "arbitrary"),
        name="mlstm_scan",
    )(qt, k, vt, opre, gates, norm_h.reshape(1, dv_all))


def kernel(x, norm_mix_pre, norm_mix_post, norm_ffn_pre, norm_ffn_post, w_up, w_down,
           attn_w_qkv, attn_w_o, mlstm_w_in, mlstm_b_gates, mlstm_norm_h, mlstm_w_out):
    bsz, seq, d = x.shape
    m = bsz * seq
    heads = ATT_HEADS
    mh = MLSTM_HEADS
    dv = d // mh
    dqk = dv // 2

    wqkv = attn_w_qkv[0].astype(BF16)
    wq, wk, wvt = wqkv[:, :d], wqkv[:, d:2 * d], wqkv[:, 2 * d:].T
    q, k, km, vt = _qkv_proj(x, norm_mix_pre[0], wq, wk, wvt, heads=heads, blk=MOBA_BLOCK,
                             tm=2 * MOBA_BLOCK)
    slopes = jnp.exp2(-8.0 * jnp.arange(1, heads + 1, dtype=F32) / heads)
    att = _moba_attention(q, k, vt, km, slopes, heads=heads, blk=MOBA_BLOCK, top=MOBA_TOPK, cb=2)
    h = _proj_res_norm(att.reshape(m, d), attn_w_o[0].astype(BF16), x.reshape(m, d),
                       norm_mix_post[0], tm=1024)
    w_up_b = w_up.astype(BF16)
    w_down_b = w_down.astype(BF16)
    h = _mlp(h, norm_ffn_pre[0], w_up_b, w_down_b, norm_ffn_post[0], layer=0, tm=1024, tf=1024)

    w_in = mlstm_w_in[0].astype(BF16)
    nq = mh * dqk
    wqt = w_in[:, :nq].T
    wk = w_in[:, nq:2 * nq]
    wvt = w_in[:, 2 * nq:2 * nq + d].T
    wo = w_in[:, 2 * nq + d:2 * nq + 2 * d]
    wg = jnp.pad(w_in[:, 2 * nq + 2 * d:], ((0, 0), (0, LANES - 2 * mh)))
    bg = jnp.pad(mlstm_b_gates[0], (0, LANES - 2 * mh)).reshape(1, LANES)
    qt, kmm, vmt, opre, gates = _mlstm_in_proj(h, norm_mix_pre[1], wqt, wk, wvt, wo, wg, bg,
                                               bsz=bsz, heads=mh, dqk=dqk, tm=512)
    y = _mlstm_scan(qt, kmm.reshape(bsz, seq, nq), vmt, opre.reshape(bsz, seq, d),
                    gates.reshape(bsz, seq, LANES), mlstm_norm_h[0], heads=mh, chunk=256)
    h = _proj_res_norm(y.reshape(m, d), mlstm_w_out[0].astype(BF16), h, norm_mix_post[1], tm=256)
    h = _mlp(h, norm_ffn_pre[1], w_up_b, w_down_b, norm_ffn_post[1], layer=1, tm=512, tf=2048)
    return h.reshape(bsz, seq, d)
```

```python
import functools

import jax
import jax.numpy as jnp
from jax import lax
from jax.experimental import pallas as pl
from jax.experimental.pallas import tpu as pltpu

F32 = jnp.float32
BF16 = jnp.bfloat16

RMS_EPS = 1e-6
ATT_HEADS = 8
MOBA_BLOCK = 256
MOBA_TOPK = 3
MLSTM_HEADS = 8
GATE_SOFTCAP = 15.0

LANES = 128
VMEM_LIMIT = 56 * 1024 * 1024
NEG_BIG = -1e30
POS_BIG = 1e30
LOG2E = 1.4426950408889634
BF16_ROWS = 16
MOBA_DEPTH = 8
MOBA_ROUNDS = 2

_NT = (((1,), (1,)), ((), ()))


def _params(*sem):
    return pltpu.CompilerParams(dimension_semantics=sem, vmem_limit_bytes=VMEM_LIMIT)


def _rms_scale(x, gain):
    ms = jnp.mean(x * x, axis=-1, keepdims=True)
    return x * lax.rsqrt(ms + RMS_EPS) * gain


def _qkv_kernel(x_ref, g_ref, wq_ref, wk_ref, wvt_ref, q_ref, k_ref, km_ref, vt_ref,
                *, scale, blk, heads):
    xn = _rms_scale(x_ref[...], g_ref[...]).astype(BF16)
    tm = xn.shape[0]
    dh = xn.shape[1] // heads
    q = jnp.dot(xn, wq_ref[...], preferred_element_type=F32)
    q_ref[...] = (q * scale).astype(BF16)
    k = jnp.dot(xn, wk_ref[...], preferred_element_type=F32)
    k_ref[...] = k.astype(BF16)
    for c in range(tm // blk):
        km_ref[c] = jnp.mean(k[c * blk:(c + 1) * blk], axis=0, keepdims=True)
    vt = lax.dot_general(wvt_ref[...], xn, _NT, preferred_element_type=F32)
    ones_row = (lax.broadcasted_iota(jnp.int32, (BF16_ROWS, blk), 0) == 0).astype(BF16)
    for h in range(heads):
        for c in range(tm // blk):
            vt_ref[0, h, c, :dh, :] = vt[h * dh:(h + 1) * dh, c * blk:(c + 1) * blk].astype(BF16)
            vt_ref[0, h, c, dh:, :] = ones_row


def _qkv_proj(x, gain, wq, wk, wvt, *, heads, blk, tm):
    bsz, seq, d = x.shape
    dh = d // heads
    nblk = seq // blk
    m = bsz * seq
    cpt = tm // blk
    tiles_per_b = seq // tm
    kern = functools.partial(_qkv_kernel, scale=dh ** -0.5 * LOG2E, blk=blk, heads=heads)
    const = lambda i: (0, 0)
    q, k, km, vt = pl.pallas_call(
        kern,
        grid=(m // tm,),
        in_specs=[
            pl.BlockSpec((tm, d), lambda i: (i, 0)),
            pl.BlockSpec((1, d), const),
            pl.BlockSpec((d, d), const),
            pl.BlockSpec((d, d), const),
            pl.BlockSpec((d, d), const),
        ],
        out_specs=[
            pl.BlockSpec((tm, d), lambda i: (i, 0)),
            pl.BlockSpec((tm, d), lambda i: (i, 0)),
            pl.BlockSpec((cpt, 1, d), lambda i: (i, 0, 0)),
            pl.BlockSpec((1, heads, cpt, dh + BF16_ROWS, blk),
                         lambda i: (i // tiles_per_b, 0, i % tiles_per_b, 0, 0)),
        ],
        out_shape=[
            jax.ShapeDtypeStruct((m, d), BF16),
            jax.ShapeDtypeStruct((m, d), BF16),
            jax.ShapeDtypeStruct((m // blk, 1, d), F32),
            jax.ShapeDtypeStruct((bsz, heads, nblk, dh + BF16_ROWS, blk), BF16),
        ],
        compiler_params=_params("parallel"),
        name="qkv_proj",
    )(x.reshape(m, d), gain.reshape(1, d), wq, wk, wvt)
    return (q.reshape(bsz, seq, d), k.reshape(bsz, seq, d),
            km.reshape(bsz, nblk, d), vt)


def _moba_items(nblk, cb):
    tiles, chunks = [], []
    for i in range(nblk):
        for c in range(-(-(i + 1) // cb)):
            tiles.append(i)
            chunks.append(c)
    while len(tiles) % (MOBA_DEPTH * MOBA_ROUNDS):
        assert nblk >= 2 * cb
        tiles.append(0)
        chunks.append(1)
    n_items = len(tiles)
    tiles += [0] * (MOBA_DEPTH - 1)
    chunks += [0] * (MOBA_DEPTH - 1)
    return n_items, tiles, chunks


def _moba_kernel(slope_ref, tile_ref, chunk_ref, q_ref, k_ref, vt_ref, km_ref, o_ref,
                 sel_ref, bias_ref, m_ref, acc_ref, cmax_ref, *s_refs,
                 blk, nblk, top, cb, n_items, qc, dh):
    slope = slope_ref[pl.program_id(1)] * LOG2E

    kr = lax.broadcasted_iota(jnp.int32, (blk, blk), 0)
    qq = lax.broadcasted_iota(jnp.int32, (blk, blk), 1)
    base = kr.astype(F32) * slope
    bias_ref[0] = base
    bias_ref[1] = jnp.where(kr <= qq, base, NEG_BIG)

    m_ref[...] = jnp.full(m_ref.shape, NEG_BIG, F32)
    acc_ref[...] = jnp.zeros(acc_ref.shape, F32)

    km = km_ref[0]
    km_hi = km.astype(BF16)
    km_lo = (km - km_hi.astype(F32)).astype(BF16)
    row = lax.broadcasted_iota(jnp.int32, (nblk, qc), 0)
    rowf = row.astype(F32)
    tiles_per_qc = qc // blk

    def select(u, carry):
        q0 = pl.multiple_of(u * qc, qc)
        qs = q_ref[0, pl.ds(q0, qc), :]
        gate = (lax.dot_general(km_hi, qs, _NT, preferred_element_type=F32)
                + lax.dot_general(km_lo, qs, _NT, preferred_element_type=F32))
        qblk = u * tiles_per_qc + lax.broadcasted_iota(jnp.int32, (nblk, qc), 1) // blk
        past = row < qblk
        g = jnp.where(past, gate, -jnp.inf)
        picked = jnp.zeros((nblk, qc), F32)
        for _ in range(top):
            mx = jnp.max(g, axis=0, keepdims=True)
            first = jnp.min(jnp.where(g == mx, rowf, float(nblk)), axis=0, keepdims=True)
            hit = rowf == first
            picked = jnp.where(hit, 1.0, picked)
            g = jnp.where(hit, -jnp.inf, g)
        selv = jnp.where(((picked > 0.5) & past) | (row == qblk), 1.0, 0.0)
        for t in range(tiles_per_qc):
            sel_ref[u * tiles_per_qc + t] = selv[:, t * blk:(t + 1) * blk]
        return carry

    lax.fori_loop(0, (nblk * blk) // qc, select, 0, unroll=2)

    def scores(item, slot):
        i = tile_ref[item]
        c = chunk_ref[item]
        qi = q_ref[0, pl.ds(pl.multiple_of(i * blk, blk), blk), :]
        for jb in range(cb):
            j = c * cb + jb
            kb = k_ref[0, pl.ds(pl.multiple_of(j * blk, blk), blk), :]
            sb = (lax.dot_general(kb, qi, _NT, preferred_element_type=F32)
                  + bias_ref[jnp.where(j == i, 1, 0)])
            s_refs[slot][jb * blk:(jb + 1) * blk, :] = sb
            cmax_ref[slot, jb] = jnp.max(sb, axis=0, keepdims=True)

    def softmax_pv(item, slot):
        i = tile_ref[item]
        c = chunk_ref[item]
        m_old = m_ref[i]
        sels, cjs = [], []
        m_new = m_old
        for jb in range(cb):
            j = c * cb + jb
            cj = ((j - i) * blk).astype(F32) * slope
            sel = sel_ref[i, pl.ds(j, 1), :] > 0.5
            m_new = jnp.maximum(m_new, jnp.where(sel, cmax_ref[slot, jb] + cj, NEG_BIG))
            sels.append(sel)
            cjs.append(cj)
        alpha = jnp.exp2(m_old - m_new)
        pv = None
        for jb in range(cb):
            shift = jnp.where(sels[jb], m_new - cjs[jb], POS_BIG)
            p = jnp.exp2(s_refs[slot][jb * blk:(jb + 1) * blk, :] - shift)
            d = jnp.dot(vt_ref[0, 0, c * cb + jb], p.astype(BF16), preferred_element_type=F32)
            pv = d if pv is None else pv + d
        acc_ref[i] = alpha * acc_ref[i] + pv
        m_ref[i] = m_new

    for slot in range(MOBA_DEPTH - 1):
        scores(slot, slot)

    def group(kk, carry):
        for u in range(MOBA_DEPTH * MOBA_ROUNDS):
            t = MOBA_DEPTH * MOBA_ROUNDS * kk + u
            scores(t + MOBA_DEPTH - 1, (u + MOBA_DEPTH - 1) % MOBA_DEPTH)
            softmax_pv(t, u % MOBA_DEPTH)
        return carry

    lax.fori_loop(0, n_items // (MOBA_DEPTH * MOBA_ROUNDS), group, 0)

    def finish(i, carry):
        acc = acc_ref[i]
        o = acc[:dh] * (1.0 / acc[dh:dh + 1])
        o_ref[0, pl.ds(pl.multiple_of(i * blk, blk), blk), :] = o.T.astype(BF16)
        return carry

    lax.fori_loop(0, nblk, finish, 0, unroll=4)


def _moba_attention(q, k, vt, km, slopes, *, heads, blk, top, cb):
    bsz, seq, d = q.shape
    dh = d // heads
    nblk = seq // blk
    assert nblk % cb == 0
    qc = min(seq, 4 * blk)
    n_items, tiles, chunks = _moba_items(nblk, cb)
    kern = functools.partial(_moba_kernel, blk=blk, nblk=nblk, top=min(top, nblk), cb=cb,
                             n_items=n_items, qc=qc, dh=dh)
    smem = pl.BlockSpec(memory_space=pltpu.SMEM)
    return pl.pallas_call(
        kern,
        grid=(bsz, heads),
        in_specs=[
            smem, smem, smem,
            pl.BlockSpec((1, seq, dh), lambda b, h: (b, 0, h)),
            pl.BlockSpec((1, seq, dh), lambda b, h: (b, 0, h)),
            pl.BlockSpec((1, 1, nblk, dh + BF16_ROWS, blk), lambda b, h: (b, h, 0, 0, 0)),
            pl.BlockSpec((1, nblk, dh), lambda b, h: (b, 0, h)),
        ],
        out_specs=pl.BlockSpec((1, seq, dh), lambda b, h: (b, 0, h)),
        out_shape=jax.ShapeDtypeStruct((bsz, seq, d), BF16),
        scratch_shapes=[
            pltpu.VMEM((nblk, nblk, blk), F32),
            pltpu.VMEM((2, blk, blk), F32),
            pltpu.VMEM((nblk, 1, blk), F32),
            pltpu.VMEM((nblk, dh + BF16_ROWS, blk), F32),
            pltpu.VMEM((MOBA_DEPTH, cb, 1, blk), F32),
        ] + [pltpu.VMEM((cb * blk, blk), F32)] * MOBA_DEPTH,
        compiler_params=_params("parallel", "parallel"),
        name="moba_attention",
    )(slopes, jnp.asarray(tiles, jnp.int32), jnp.asarray(chunks, jnp.int32), q, k, vt, km)


def _proj_mlp_kernel(a_ref, h_ref, wo_ref, gm_ref, g1_ref, wu_ref, wd_ref, g2_ref, o_ref, *, tf):
    u = jnp.dot(a_ref[...], wo_ref[...], preferred_element_type=F32)
    h1 = h_ref[...] + _rms_scale(u, gm_ref[...])
    xn = _rms_scale(h1, g1_ref[...]).astype(BF16)
    ff = wu_ref.shape[1]
    acc = None
    for c in range(ff // tf):
        t = jnp.dot(xn, wu_ref[:, c * tf:(c + 1) * tf], preferred_element_type=F32)
        t = jnp.maximum(t, 0.0)
        d = jnp.dot((t * t).astype(BF16), wd_ref[c * tf:(c + 1) * tf, :], preferred_element_type=F32)
        acc = d if acc is None else acc + d
    o_ref[...] = h1 + _rms_scale(acc, g2_ref[...])


def _proj_mlp(a, h, w_o, g_mix, g_pre, w_up, w_down, g_post, *, layer, tm, tf):
    m, d = h.shape
    kdim = a.shape[1]
    ff = w_up.shape[2]
    resident = pl.Buffered(1)
    row = lambda i: (i, 0)
    const = lambda i: (0, 0)
    return pl.pallas_call(
        functools.partial(_proj_mlp_kernel, tf=tf),
        grid=(m // tm,),
        in_specs=[
            pl.BlockSpec((tm, kdim), row),
            pl.BlockSpec((tm, d), row),
            pl.BlockSpec((kdim, d), const, pipeline_mode=resident),
            pl.BlockSpec((1, d), const),
            pl.BlockSpec((1, d), const),
            pl.BlockSpec((None, d, ff), lambda i: (layer, 0, 0), pipeline_mode=resident),
            pl.BlockSpec((None, ff, d), lambda i: (layer, 0, 0), pipeline_mode=resident),
            pl.BlockSpec((1, d), const),
        ],
        out_specs=pl.BlockSpec((tm, d), row),
        out_shape=jax.ShapeDtypeStruct((m, d), F32),
        compiler_params=_params("parallel"),
        name="proj_mlp",
    )(a, h, w_o, g_mix.reshape(1, d), g_pre.reshape(1, d), w_up, w_down, g_post.reshape(1, d))


def _mlstm_in_kernel(x_ref, g_ref, wqt_ref, wk_ref, wvt_ref, wo_ref, wg_ref, bg_ref,
                     qt_ref, k_ref, vt_ref, op_ref, gates_ref, *, heads, kscale):
    xn = _rms_scale(x_ref[...], g_ref[...]).astype(BF16)
    qt_ref[0] = lax.dot_general(wqt_ref[...], xn, _NT, preferred_element_type=F32).astype(BF16)
    k_ref[...] = (jnp.dot(xn, wk_ref[...], preferred_element_type=F32) * kscale).astype(BF16)
    vt_ref[0] = lax.dot_general(wvt_ref[...], xn, _NT, preferred_element_type=F32).astype(BF16)
    op_ref[...] = jnp.dot(xn, wo_ref[...], preferred_element_type=F32).astype(BF16)
    z = jnp.dot(xn, wg_ref[...], preferred_element_type=F32) + bg_ref[...]
    z = GATE_SOFTCAP * jnp.tanh(z * (1.0 / GATE_SOFTCAP))
    log_f = jnp.minimum(z, 0.0) - jnp.log1p(jnp.exp(-jnp.abs(z)))
    lane = lax.broadcasted_iota(jnp.int32, z.shape, 1)
    gates_ref[...] = jnp.where(lane < heads, z, log_f)


def _mlstm_in_proj(x, gain, wqt, wk, wvt, wo, wg, bg, *, bsz, heads, dqk, tm):
    m, d = x.shape
    seq = m // bsz
    nq = wqt.shape[0]
    dv_all = wvt.shape[0]
    tiles_per_b = seq // tm
    kern = functools.partial(_mlstm_in_kernel, heads=heads, kscale=dqk ** -0.5)
    const = lambda i: (0, 0)
    row = lambda i: (i, 0)
    tcol = lambda i: (i // tiles_per_b, 0, i % tiles_per_b)
    return pl.pallas_call(
        kern,
        grid=(m // tm,),
        in_specs=[
            pl.BlockSpec((tm, d), row),
            pl.BlockSpec((1, d), const),
            pl.BlockSpec((nq, d), const),
            pl.BlockSpec((d, nq), const),
            pl.BlockSpec((dv_all, d), const),
            pl.BlockSpec((d, dv_all), const),
            pl.BlockSpec((d, LANES), const),
            pl.BlockSpec((1, LANES), const),
        ],
        out_specs=[
            pl.BlockSpec((1, nq, tm), tcol),
            pl.BlockSpec((tm, nq), row),
            pl.BlockSpec((1, dv_all, tm), tcol),
            pl.BlockSpec((tm, dv_all), row),
            pl.BlockSpec((tm, LANES), row),
        ],
        out_shape=[
            jax.ShapeDtypeStruct((bsz, nq, seq), BF16),
            jax.ShapeDtypeStruct((m, nq), BF16),
            jax.ShapeDtypeStruct((bsz, dv_all, seq), BF16),
            jax.ShapeDtypeStruct((m, dv_all), BF16),
            jax.ShapeDtypeStruct((m, LANES), F32),
        ],
        compiler_params=_params("parallel"),
        name="mlstm_in_proj",
    )(x, gain.reshape(1, d), wqt, wk, wvt, wo, wg, bg)


def _mlstm_kernel(qt_ref, k_ref, vt_ref, op_ref, g_ref, nh_ref, y_ref, c_ref, m_ref,
                  *, heads, dqk, dv, chunk):
    @pl.when(pl.program_id(1) == 0)
    def _():
        c_ref[...] = jnp.zeros_like(c_ref)
        m_ref[...] = jnp.zeros_like(m_ref)

    L = chunk
    gates_t = g_ref[0].T
    ss = lax.broadcasted_iota(jnp.int32, (L, L), 0)
    tt = lax.broadcasted_iota(jnp.int32, (L, L), 1)
    causal = ss <= tt
    i_rows = gates_t[0:heads]
    f_rows = gates_t[heads:2 * heads]
    tri = causal.astype(BF16)
    f_hi = f_rows.astype(BF16)
    f_r1 = f_rows - f_hi.astype(F32)
    f_mid = f_r1.astype(BF16)
    f_lo = (f_r1 - f_mid.astype(F32)).astype(BF16)
    b_rows = (jnp.dot(f_hi, tri, preferred_element_type=F32)
              + jnp.dot(f_mid, tri, preferred_element_type=F32)
              + jnp.dot(f_lo, tri, preferred_element_type=F32))
    u_rows = i_rows - b_rows
    u_cols = jnp.concatenate([u_rows, jnp.zeros((LANES - heads, L), F32)], axis=0).T
    ones_rows = (lax.broadcasted_iota(jnp.int32, (BF16_ROWS, L), 0) == 0).astype(BF16)
    sub = lax.broadcasted_iota(jnp.int32, (2 * dqk, 1), 0)
    lane = lax.broadcasted_iota(jnp.int32, (1, 2 * dqk), 1)

    for p in range(heads // 2):
        qt_pair = qt_ref[0, p * 2 * dqk:(p + 1) * 2 * dqk, :]
        kg = k_ref[0, :, p * 2 * dqk:(p + 1) * 2 * dqk]
        ct = c_ref[p]
        ct_b = ct.astype(BF16)
        upds, decays = [], []
        for half in range(2):
            h = 2 * p + half
            in_head = (sub >= half * dqk) & (sub < (half + 1) * dqk)
            qt_h = jnp.where(in_head, qt_pair, jnp.zeros_like(qt_pair))
            b_row = b_rows[h:h + 1]
            i_row = i_rows[h:h + 1]
            m_prev = m_ref[h][:, 0:1]
            log_d = jnp.where(causal, u_cols[:, h:h + 1] + b_row, -jnp.inf)
            log_inter = b_row + m_prev
            m_t = jnp.maximum(log_inter, jnp.max(log_d, axis=0, keepdims=True))
            dmat = jnp.exp(log_d - m_t)
            w_inter = jnp.exp(log_inter - m_t)
            s_t = jnp.dot(kg, qt_h, preferred_element_type=F32) * dmat
            vext = jnp.concatenate([vt_ref[0, h * dv:(h + 1) * dv, :], ones_rows], axis=0)
            num = (w_inter * jnp.dot(ct_b, qt_h, preferred_element_type=F32)
                   + jnp.dot(vext, s_t.astype(BF16), preferred_element_type=F32))
            den = num[dv:dv + 1]
            hout = num[:dv] * (1.0 / jnp.maximum(jnp.abs(den), jnp.exp(-m_t)))
            hn = hout * lax.rsqrt(jnp.mean(hout * hout, axis=0, keepdims=True) + RMS_EPS)
            og = jax.nn.sigmoid(op_ref[0, :, h * dv:(h + 1) * dv].astype(F32))
            y_ref[0, :, h * dv:(h + 1) * dv] = (
                og * (hn.T * nh_ref[:, h * dv:(h + 1) * dv])).astype(BF16)
            b_end = b_row[:, L - 1:L]
            log_w = b_end - b_row + i_row
            m_new = jnp.maximum(b_end + m_prev, jnp.max(log_w, axis=1, keepdims=True))
            decays.append(jnp.exp(b_end + m_prev - m_new))
            vw = (vext.astype(F32) * jnp.exp(log_w - m_new)).astype(BF16)
            upds.append(jnp.dot(vw, kg, preferred_element_type=F32))
            m_ref[h] = jnp.broadcast_to(m_new, (1, LANES))
        first = lane < dqk
        c_ref[p] = jnp.where(first, decays[0], decays[1]) * ct + jnp.where(first, upds[0], upds[1])


def _mlstm_scan(qt, k, vt, opre, gates, norm_h, *, heads, chunk):
    bsz, dv_all, seq = vt.shape
    dv = dv_all // heads
    dqk = qt.shape[1] // heads
    kern = functools.partial(_mlstm_kernel, heads=heads, dqk=dqk, dv=dv, chunk=chunk)
    rows = lambda b, c: (b, c, 0)
    cols = lambda b, c: (b, 0, c)
    return pl.pallas_call(
        kern,
        grid=(bsz, seq // chunk),
        in_specs=[
            pl.BlockSpec((1, heads * dqk, chunk), cols),
            pl.BlockSpec((1, chunk, heads * dqk), rows),
            pl.BlockSpec((1, dv_all, chunk), cols),
            pl.BlockSpec((1, chunk, dv_all), rows),
            pl.BlockSpec((1, chunk, LANES), rows),
            pl.BlockSpec((1, dv_all), lambda b, c: (0, 0)),
        ],
        out_specs=pl.BlockSpec((1, chunk, dv_all), rows),
        out_shape=jax.ShapeDtypeStruct((bsz, seq, dv_all), BF16),
        scratch_shapes=[
            pltpu.VMEM((heads // 2, dv + BF16_ROWS, 2 * dqk), F32),
            pltpu.VMEM((heads, 1, LANES), F32),
        ],
        compiler_params=_params("parallel", "arbitrary"),
        name="mlstm_scan",
    )(qt, k, vt, opre, gates, norm_h.reshape(1, dv_all))


def kernel(x, norm_mix_pre, norm_mix_post, norm_ffn_pre, norm_ffn_post, w_up, w_down,
           attn_w_qkv, attn_w_o, mlstm_w_in, mlstm_b_gates, mlstm_norm_h, mlstm_w_out):
    bsz, seq, d = x.shape
    m = bsz * seq
    heads = ATT_HEADS
    mh = MLSTM_HEADS
    dv = d // mh
    dqk = dv // 2

    wqkv = attn_w_qkv[0].astype(BF16)
    wq, wk, wvt = wqkv[:, :d], wqkv[:, d:2 * d], wqkv[:, 2 * d:].T
    q, k, km, vt = _qkv_proj(x, norm_mix_pre[0], wq, wk, wvt, heads=heads, blk=MOBA_BLOCK,
                             tm=2 * MOBA_BLOCK)
    slopes = jnp.exp2(-8.0 * jnp.arange(1, heads + 1, dtype=F32) / heads)
    att = _moba_attention(q, k, vt, km, slopes, heads=heads, blk=MOBA_BLOCK, top=MOBA_TOPK, cb=2)
    w_up_b = w_up.astype(BF16)
    w_down_b = w_down.astype(BF16)
    h = _proj_mlp(att.reshape(m, d), x.reshape(m, d), attn_w_o[0].astype(BF16), norm_mix_post[0],
                  norm_ffn_pre[0], w_up_b, w_down_b, norm_ffn_post[0], layer=0, tm=512, tf=1024)

    w_in = mlstm_w_in[0].astype(BF16)
    nq = mh * dqk
    wqt = w_in[:, :nq].T
    wk = w_in[:, nq:2 * nq]
    wvt = w_in[:, 2 * nq:2 * nq + d].T
    wo = w_in[:, 2 * nq + d:2 * nq + 2 * d]
    wg = jnp.pad(w_in[:, 2 * nq + 2 * d:], ((0, 0), (0, LANES - 2 * mh)))
    bg = jnp.pad(mlstm_b_gates[0], (0, LANES - 2 * mh)).reshape(1, LANES)
    qt, kmm, vmt, opre, gates = _mlstm_in_proj(h, norm_mix_pre[1], wqt, wk, wvt, wo, wg, bg,
                                               bsz=bsz, heads=mh, dqk=dqk, tm=512)
    y = _mlstm_scan(qt, kmm.reshape(bsz, seq, nq), vmt, opre.reshape(bsz, seq, d),
                    gates.reshape(bsz, seq, LANES), mlstm_norm_h[0], heads=mh, chunk=256)
    h = _proj_mlp(y.reshape(m, d), h, mlstm_w_out[0].astype(BF16), norm_mix_post[1],
                  norm_ffn_pre[1], w_up_b, w_down_b, norm_ffn_post[1], layer=1, tm=1024, tf=1024)
    return h.reshape(bsz, seq, d)
```

```python
import functools

import jax
import jax.numpy as jnp
from jax import lax
from jax.experimental import pallas as pl
from jax.experimental.pallas import tpu as pltpu

F32 = jnp.float32
BF16 = jnp.bfloat16

RMS_EPS = 1e-6
ATT_HEADS = 8
MOBA_BLOCK = 256
MOBA_TOPK = 3
MLSTM_HEADS = 8
GATE_SOFTCAP = 15.0

LANES = 128
VMEM_LIMIT = 56 * 1024 * 1024
NEG_BIG = -1e30
POS_BIG = 1e30
LOG2E = 1.4426950408889634
BF16_ROWS = 16
MOBA_DEPTH = 8
MOBA_ROUNDS = 2

_NT = (((1,), (1,)), ((), ()))


def _params(*sem):
    return pltpu.CompilerParams(dimension_semantics=sem, vmem_limit_bytes=VMEM_LIMIT)


def _rms_scale(x, gain):
    ms = jnp.mean(x * x, axis=-1, keepdims=True)
    return x * lax.rsqrt(ms + RMS_EPS) * gain


def _qkv_kernel(x_ref, g_ref, w_ref, q_ref, k_ref, km_ref, vt_ref, wq_ref, wk_ref, wvt_ref,
                *, scale, blk, heads):
    d = x_ref.shape[1]

    @pl.when(pl.program_id(0) == 0)
    def _():
        wq_ref[...] = w_ref[:, :d].astype(BF16)
        wk_ref[...] = w_ref[:, d:2 * d].astype(BF16)
        wvt_ref[...] = w_ref[:, 2 * d:].T.astype(BF16)

    xn = _rms_scale(x_ref[...], g_ref[...]).astype(BF16)
    tm = xn.shape[0]
    dh = d // heads
    q = jnp.dot(xn, wq_ref[...], preferred_element_type=F32)
    q_ref[...] = (q * scale).astype(BF16)
    k = jnp.dot(xn, wk_ref[...], preferred_element_type=F32)
    k_ref[...] = k.astype(BF16)
    for c in range(tm // blk):
        km_ref[c] = jnp.mean(k[c * blk:(c + 1) * blk], axis=0, keepdims=True)
    vt = lax.dot_general(wvt_ref[...], xn, _NT, preferred_element_type=F32)
    ones_row = (lax.broadcasted_iota(jnp.int32, (BF16_ROWS, blk), 0) == 0).astype(BF16)
    for h in range(heads):
        for c in range(tm // blk):
            vt_ref[0, h, c, :dh, :] = vt[h * dh:(h + 1) * dh, c * blk:(c + 1) * blk].astype(BF16)
            vt_ref[0, h, c, dh:, :] = ones_row


def _qkv_proj(x, gain, w_qkv, *, heads, blk, tm):
    bsz, seq, d = x.shape
    dh = d // heads
    nblk = seq // blk
    m = bsz * seq
    cpt = tm // blk
    tiles_per_b = seq // tm
    kern = functools.partial(_qkv_kernel, scale=dh ** -0.5 * LOG2E, blk=blk, heads=heads)
    const = lambda i: (0, 0)
    q, k, km, vt = pl.pallas_call(
        kern,
        grid=(m // tm,),
        in_specs=[
            pl.BlockSpec((tm, d), lambda i: (i, 0)),
            pl.BlockSpec((1, d), const),
            pl.BlockSpec((d, 3 * d), const, pipeline_mode=pl.Buffered(1)),
        ],
        out_specs=[
            pl.BlockSpec((tm, d), lambda i: (i, 0)),
            pl.BlockSpec((tm, d), lambda i: (i, 0)),
            pl.BlockSpec((cpt, 1, d), lambda i: (i, 0, 0)),
            pl.BlockSpec((1, heads, cpt, dh + BF16_ROWS, blk),
                         lambda i: (i // tiles_per_b, 0, i % tiles_per_b, 0, 0)),
        ],
        out_shape=[
            jax.ShapeDtypeStruct((m, d), BF16),
            jax.ShapeDtypeStruct((m, d), BF16),
            jax.ShapeDtypeStruct((m // blk, 1, d), F32),
            jax.ShapeDtypeStruct((bsz, heads, nblk, dh + BF16_ROWS, blk), BF16),
        ],
        scratch_shapes=[pltpu.VMEM((d, d), BF16)] * 3,
        compiler_params=_params("arbitrary"),
        name="qkv_proj",
    )(x.reshape(m, d), gain.reshape(1, d), w_qkv)
    return (q.reshape(bsz, seq, d), k.reshape(bsz, seq, d),
            km.reshape(bsz, nblk, d), vt)


def _moba_items(nblk, cb):
    tiles, chunks = [], []
    for i in range(nblk):
        for c in range(-(-(i + 1) // cb)):
            tiles.append(i)
            chunks.append(c)
    while len(tiles) % (MOBA_DEPTH * MOBA_ROUNDS):
        assert nblk >= 2 * cb
        tiles.append(0)
        chunks.append(1)
    n_items = len(tiles)
    tiles += [0] * (MOBA_DEPTH - 1)
    chunks += [0] * (MOBA_DEPTH - 1)
    return n_items, tiles, chunks


def _moba_kernel(slope_ref, tile_ref, chunk_ref, q_ref, k_ref, vt_ref, km_ref, o_ref,
                 sel_ref, bias_ref, m_ref, acc_ref, cmax_ref, *s_refs,
                 blk, nblk, top, cb, n_items, qc, dh):
    slope = slope_ref[pl.program_id(1)] * LOG2E

    kr = lax.broadcasted_iota(jnp.int32, (blk, blk), 0)
    qq = lax.broadcasted_iota(jnp.int32, (blk, blk), 1)
    base = kr.astype(F32) * slope
    bias_ref[0] = base
    bias_ref[1] = jnp.where(kr <= qq, base, NEG_BIG)

    m_ref[...] = jnp.full(m_ref.shape, NEG_BIG, F32)
    acc_ref[...] = jnp.zeros(acc_ref.shape, F32)

    km = km_ref[0]
    km_hi = km.astype(BF16)
    km_lo = (km - km_hi.astype(F32)).astype(BF16)
    row = lax.broadcasted_iota(jnp.int32, (nblk, qc), 0)
    rowf = row.astype(F32)
    tiles_per_qc = qc // blk

    def select(u, carry):
        q0 = pl.multiple_of(u * qc, qc)
        qs = q_ref[0, pl.ds(q0, qc), :]
        gate = (lax.dot_general(km_hi, qs, _NT, preferred_element_type=F32)
                + lax.dot_general(km_lo, qs, _NT, preferred_element_type=F32))
        qblk = u * tiles_per_qc + lax.broadcasted_iota(jnp.int32, (nblk, qc), 1) // blk
        past = row < qblk
        g = jnp.where(past, gate, -jnp.inf)
        picked = jnp.zeros((nblk, qc), F32)
        for _ in range(top):
            mx = jnp.max(g, axis=0, keepdims=True)
            first = jnp.min(jnp.where(g == mx, rowf, float(nblk)), axis=0, keepdims=True)
            hit = rowf == first
            picked = jnp.where(hit, 1.0, picked)
            g = jnp.where(hit, -jnp.inf, g)
        selv = jnp.where(((picked > 0.5) & past) | (row == qblk), 1.0, 0.0)
        for t in range(tiles_per_qc):
            sel_ref[u * tiles_per_qc + t] = selv[:, t * blk:(t + 1) * blk]
        return carry

    lax.fori_loop(0, (nblk * blk) // qc, select, 0, unroll=2)

    def scores(item, slot):
        i = tile_ref[item]
        c = chunk_ref[item]
        qi = q_ref[0, pl.ds(pl.multiple_of(i * blk, blk), blk), :]
        for jb in range(cb):
            j = c * cb + jb
            kb = k_ref[0, pl.ds(pl.multiple_of(j * blk, blk), blk), :]
            sb = (lax.dot_general(kb, qi, _NT, preferred_element_type=F32)
                  + bias_ref[jnp.where(j == i, 1, 0)])
            s_refs[slot][jb * blk:(jb + 1) * blk, :] = sb
            cmax_ref[slot, jb] = jnp.max(sb, axis=0, keepdims=True)

    def softmax_pv(item, slot):
        i = tile_ref[item]
        c = chunk_ref[item]
        m_old = m_ref[i]
        sels, cjs = [], []
        m_new = m_old
        for jb in range(cb):
            j = c * cb + jb
            cj = ((j - i) * blk).astype(F32) * slope
            sel = sel_ref[i, pl.ds(j, 1), :] > 0.5
            m_new = jnp.maximum(m_new, jnp.where(sel, cmax_ref[slot, jb] + cj, NEG_BIG))
            sels.append(sel)
            cjs.append(cj)
        alpha = jnp.exp2(m_old - m_new)
        pv = None
        for jb in range(cb):
            shift = jnp.where(sels[jb], m_new - cjs[jb], POS_BIG)
            p = jnp.exp2(s_refs[slot][jb * blk:(jb + 1) * blk, :] - shift)
            d = jnp.dot(vt_ref[0, 0, c * cb + jb], p.astype(BF16), preferred_element_type=F32)
            pv = d if pv is None else pv + d
        acc_ref[i] = alpha * acc_ref[i] + pv
        m_ref[i] = m_new

    for slot in range(MOBA_DEPTH - 1):
        scores(slot, slot)

    def group(kk, carry):
        for u in range(MOBA_DEPTH * MOBA_ROUNDS):
            t = MOBA_DEPTH * MOBA_ROUNDS * kk + u
            scores(t + MOBA_DEPTH - 1, (u + MOBA_DEPTH - 1) % MOBA_DEPTH)
            softmax_pv(t, u % MOBA_DEPTH)
        return carry

    lax.fori_loop(0, n_items // (MOBA_DEPTH * MOBA_ROUNDS), group, 0)

    def finish(i, carry):
        acc = acc_ref[i]
        o = acc[:dh] * (1.0 / acc[dh:dh + 1])
        o_ref[0, pl.ds(pl.multiple_of(i * blk, blk), blk), :] = o.T.astype(BF16)
        return carry

    lax.fori_loop(0, nblk, finish, 0, unroll=4)


def _moba_attention(q, k, vt, km, slopes, *, heads, blk, top, cb):
    bsz, seq, d = q.shape
    dh = d // heads
    nblk = seq // blk
    assert nblk % cb == 0
    qc = min(seq, 4 * blk)
    n_items, tiles, chunks = _moba_items(nblk, cb)
    kern = functools.partial(_moba_kernel, blk=blk, nblk=nblk, top=min(top, nblk), cb=cb,
                             n_items=n_items, qc=qc, dh=dh)
    smem = pl.BlockSpec(memory_space=pltpu.SMEM)
    return pl.pallas_call(
        kern,
        grid=(bsz, heads),
        in_specs=[
            smem, smem, smem,
            pl.BlockSpec((1, seq, dh), lambda b, h: (b, 0, h)),
            pl.BlockSpec((1, seq, dh), lambda b, h: (b, 0, h)),
            pl.BlockSpec((1, 1, nblk, dh + BF16_ROWS, blk), lambda b, h: (b, h, 0, 0, 0)),
            pl.BlockSpec((1, nblk, dh), lambda b, h: (b, 0, h)),
        ],
        out_specs=pl.BlockSpec((1, seq, dh), lambda b, h: (b, 0, h)),
        out_shape=jax.ShapeDtypeStruct((bsz, seq, d), BF16),
        scratch_shapes=[
            pltpu.VMEM((nblk, nblk, blk), F32),
            pltpu.VMEM((2, blk, blk), F32),
            pltpu.VMEM((nblk, 1, blk), F32),
            pltpu.VMEM((nblk, dh + BF16_ROWS, blk), F32),
            pltpu.VMEM((MOBA_DEPTH, cb, 1, blk), F32),
        ] + [pltpu.VMEM((cb * blk, blk), F32)] * MOBA_DEPTH,
        compiler_params=_params("parallel", "parallel"),
        name="moba_attention",
    )(slopes, jnp.asarray(tiles, jnp.int32), jnp.asarray(chunks, jnp.int32), q, k, vt, km)


def _proj_mlp_kernel(a_ref, h_ref, wo_ref, gm_ref, g1_ref, wu_ref, wd_ref, g2_ref, o_ref, *, tf):
    u = jnp.dot(a_ref[...], wo_ref[...], preferred_element_type=F32)
    h1 = h_ref[...] + _rms_scale(u, gm_ref[...])
    xn = _rms_scale(h1, g1_ref[...]).astype(BF16)
    ff = wu_ref.shape[1]
    acc = None
    for c in range(ff // tf):
        t = jnp.dot(xn, wu_ref[:, c * tf:(c + 1) * tf], preferred_element_type=F32)
        t = jnp.maximum(t, 0.0)
        d = jnp.dot((t * t).astype(BF16), wd_ref[c * tf:(c + 1) * tf, :], preferred_element_type=F32)
        acc = d if acc is None else acc + d
    o_ref[...] = h1 + _rms_scale(acc, g2_ref[...])


def _proj_mlp(a, h, w_o, g_mix, g_pre, w_up, w_down, g_post, *, layer, tm, tf):
    m, d = h.shape
    kdim = a.shape[1]
    ff = w_up.shape[2]
    resident = pl.Buffered(1)
    row = lambda i: (i, 0)
    const = lambda i: (0, 0)
    return pl.pallas_call(
        functools.partial(_proj_mlp_kernel, tf=tf),
        grid=(m // tm,),
        in_specs=[
            pl.BlockSpec((tm, kdim), row),
            pl.BlockSpec((tm, d), row),
            pl.BlockSpec((kdim, d), const, pipeline_mode=resident),
            pl.BlockSpec((1, d), const),
            pl.BlockSpec((1, d), const),
            pl.BlockSpec((None, d, ff), lambda i: (layer, 0, 0), pipeline_mode=resident),
            pl.BlockSpec((None, ff, d), lambda i: (layer, 0, 0), pipeline_mode=resident),
            pl.BlockSpec((1, d), const),
        ],
        out_specs=pl.BlockSpec((tm, d), row),
        out_shape=jax.ShapeDtypeStruct((m, d), F32),
        compiler_params=_params("parallel"),
        name="proj_mlp",
    )(a, h, w_o, g_mix.reshape(1, d), g_pre.reshape(1, d), w_up, w_down, g_post.reshape(1, d))


def _mlstm_in_kernel(x_ref, g_ref, w_ref, wg_ref, bg_ref,
                     qt_ref, k_ref, vt_ref, op_ref, gates_ref,
                     wqt_ref, wk_ref, wvt_ref, wo_ref, *, heads, kscale):
    nq = wk_ref.shape[1]
    d = wo_ref.shape[1]

    @pl.when(pl.program_id(0) == 0)
    def _():
        wqt_ref[...] = w_ref[:, :nq].T.astype(BF16)
        wk_ref[...] = w_ref[:, nq:2 * nq].astype(BF16)
        wvt_ref[...] = w_ref[:, 2 * nq:2 * nq + d].T.astype(BF16)
        wo_ref[...] = w_ref[:, 2 * nq + d:2 * nq + 2 * d].astype(BF16)

    xn = _rms_scale(x_ref[...], g_ref[...]).astype(BF16)
    qt_ref[0] = lax.dot_general(wqt_ref[...], xn, _NT, preferred_element_type=F32).astype(BF16)
    k_ref[...] = (jnp.dot(xn, wk_ref[...], preferred_element_type=F32) * kscale).astype(BF16)
    vt_ref[0] = lax.dot_general(wvt_ref[...], xn, _NT, preferred_element_type=F32).astype(BF16)
    op_ref[...] = jnp.dot(xn, wo_ref[...], preferred_element_type=F32).astype(BF16)
    z = jnp.dot(xn, wg_ref[...], preferred_element_type=F32) + bg_ref[...]
    z = GATE_SOFTCAP * jnp.tanh(z * (1.0 / GATE_SOFTCAP))
    log_f = jnp.minimum(z, 0.0) - jnp.log1p(jnp.exp(-jnp.abs(z)))
    lane = lax.broadcasted_iota(jnp.int32, z.shape, 1)
    gates_ref[...] = jnp.where(lane < heads, z, log_f)


def _mlstm_in_proj(x, gain, w_in, wg, bg, *, bsz, heads, dqk, tm):
    m, d = x.shape
    seq = m // bsz
    nq = heads * dqk
    dv_all = d
    tiles_per_b = seq // tm
    kern = functools.partial(_mlstm_in_kernel, heads=heads, kscale=dqk ** -0.5)
    const = lambda i: (0, 0)
    row = lambda i: (i, 0)
    tcol = lambda i: (i // tiles_per_b, 0, i % tiles_per_b)
    return pl.pallas_call(
        kern,
        grid=(m // tm,),
        in_specs=[
            pl.BlockSpec((tm, d), row),
            pl.BlockSpec((1, d), const),
            pl.BlockSpec(w_in.shape, const, pipeline_mode=pl.Buffered(1)),
            pl.BlockSpec((d, LANES), const),
            pl.BlockSpec((1, LANES), const),
        ],
        out_specs=[
            pl.BlockSpec((1, nq, tm), tcol),
            pl.BlockSpec((tm, nq), row),
            pl.BlockSpec((1, dv_all, tm), tcol),
            pl.BlockSpec((tm, dv_all), row),
            pl.BlockSpec((tm, LANES), row),
        ],
        out_shape=[
            jax.ShapeDtypeStruct((bsz, nq, seq), BF16),
            jax.ShapeDtypeStruct((m, nq), BF16),
            jax.ShapeDtypeStruct((bsz, dv_all, seq), BF16),
            jax.ShapeDtypeStruct((m, dv_all), BF16),
            jax.ShapeDtypeStruct((m, LANES), F32),
        ],
        scratch_shapes=[
            pltpu.VMEM((nq, d), BF16),
            pltpu.VMEM((d, nq), BF16),
            pltpu.VMEM((dv_all, d), BF16),
            pltpu.VMEM((d, dv_all), BF16),
        ],
        compiler_params=_params("arbitrary"),
        name="mlstm_in_proj",
    )(x, gain.reshape(1, d), w_in, wg, bg)


def _mlstm_kernel(qt_ref, k_ref, vt_ref, op_ref, g_ref, gnext_ref, nh_ref, y_ref,
                  c_ref, m_ref, rows_ref, ucol_ref, *, heads, dqk, dv, chunk):
    L = chunk
    ss = lax.broadcasted_iota(jnp.int32, (L, L), 0)
    tt = lax.broadcasted_iota(jnp.int32, (L, L), 1)
    causal = ss <= tt

    def gate_terms(gates):
        gates_t = gates.T
        i_rows = gates_t[0:heads]
        f_rows = gates_t[heads:2 * heads]
        tri = causal.astype(BF16)
        f_hi = f_rows.astype(BF16)
        f_r1 = f_rows - f_hi.astype(F32)
        f_mid = f_r1.astype(BF16)
        f_lo = (f_r1 - f_mid.astype(F32)).astype(BF16)
        b_rows = (jnp.dot(f_hi, tri, preferred_element_type=F32)
                  + jnp.dot(f_mid, tri, preferred_element_type=F32)
                  + jnp.dot(f_lo, tri, preferred_element_type=F32))
        rows_ref[0:heads] = i_rows
        rows_ref[heads:2 * heads] = b_rows
        ucol_ref[...] = jnp.concatenate(
            [i_rows - b_rows, jnp.zeros((LANES - heads, L), F32)], axis=0).T

    @pl.when(pl.program_id(1) == 0)
    def _():
        c_ref[...] = jnp.zeros_like(c_ref)
        m_ref[...] = jnp.zeros_like(m_ref)
        gate_terms(g_ref[0])

    ones_rows = (lax.broadcasted_iota(jnp.int32, (BF16_ROWS, L), 0) == 0).astype(BF16)
    sub = lax.broadcasted_iota(jnp.int32, (2 * dqk, 1), 0)
    lane = lax.broadcasted_iota(jnp.int32, (1, 2 * dqk), 1)

    for p in range(heads // 2):
        qt_pair = qt_ref[0, p * 2 * dqk:(p + 1) * 2 * dqk, :]
        kg = k_ref[0, :, p * 2 * dqk:(p + 1) * 2 * dqk]
        ct = c_ref[p]
        ct_b = ct.astype(BF16)
        upds, decays = [], []
        for half in range(2):
            h = 2 * p + half
            in_head = (sub >= half * dqk) & (sub < (half + 1) * dqk)
            qt_h = jnp.where(in_head, qt_pair, jnp.zeros_like(qt_pair))
            b_row = rows_ref[heads + h:heads + h + 1, :]
            i_row = rows_ref[h:h + 1, :]
            m_prev = m_ref[h][:, 0:1]
            log_d = jnp.where(causal, ucol_ref[:, h:h + 1] + b_row, -jnp.inf)
            log_inter = b_row + m_prev
            m_t = jnp.maximum(log_inter, jnp.max(log_d, axis=0, keepdims=True))
            dmat = jnp.exp(log_d - m_t)
            w_inter = jnp.exp(log_inter - m_t)
            s_t = jnp.dot(kg, qt_h, preferred_element_type=F32) * dmat
            vext = jnp.concatenate([vt_ref[0, h * dv:(h + 1) * dv, :], ones_rows], axis=0)
            num = (w_inter * jnp.dot(ct_b, qt_h, preferred_element_type=F32)
                   + jnp.dot(vext, s_t.astype(BF16), preferred_element_type=F32))
            den = num[dv:dv + 1]
            hout = num[:dv] * (1.0 / jnp.maximum(jnp.abs(den), jnp.exp(-m_t)))
            hn = hout * lax.rsqrt(jnp.mean(hout * hout, axis=0, keepdims=True) + RMS_EPS)
            og = jax.nn.sigmoid(op_ref[0, :, h * dv:(h + 1) * dv].astype(F32))
            y_ref[0, :, h * dv:(h + 1) * dv] = (
                og * (hn.T * nh_ref[:, h * dv:(h + 1) * dv])).astype(BF16)
            b_end = b_row[:, L - 1:L]
            log_w = b_end - b_row + i_row
            m_new = jnp.maximum(b_end + m_prev, jnp.max(log_w, axis=1, keepdims=True))
            decays.append(jnp.exp(b_end + m_prev - m_new))
            vw = (vext.astype(F32) * jnp.exp(log_w - m_new)).astype(BF16)
            upds.append(jnp.dot(vw, kg, preferred_element_type=F32))
            m_ref[h] = jnp.broadcast_to(m_new, (1, LANES))
        first = lane < dqk
        c_ref[p] = jnp.where(first, decays[0], decays[1]) * ct + jnp.where(first, upds[0], upds[1])

    gate_terms(gnext_ref[0])


def _mlstm_scan(qt, k, vt, opre, gates, norm_h, *, heads, chunk):
    bsz, dv_all, seq = vt.shape
    dv = dv_all // heads
    dqk = qt.shape[1] // heads
    kern = functools.partial(_mlstm_kernel, heads=heads, dqk=dqk, dv=dv, chunk=chunk)
    nchunk = seq // chunk
    rows = lambda b, c: (b, c, 0)
    cols = lambda b, c: (b, 0, c)
    return pl.pallas_call(
        kern,
        grid=(bsz, seq // chunk),
        in_specs=[
            pl.BlockSpec((1, heads * dqk, chunk), cols),
            pl.BlockSpec((1, chunk, heads * dqk), rows),
            pl.BlockSpec((1, dv_all, chunk), cols),
            pl.BlockSpec((1, chunk, dv_all), rows),
            pl.BlockSpec((1, chunk, LANES), rows),
            pl.BlockSpec((1, chunk, LANES), lambda b, c: (b, jnp.minimum(c + 1, nchunk - 1), 0)),
            pl.BlockSpec((1, dv_all), lambda b, c: (0, 0)),
        ],
        out_specs=pl.BlockSpec((1, chunk, dv_all), rows),
        out_shape=jax.ShapeDtypeStruct((bsz, seq, dv_all), BF16),
        scratch_shapes=[
            pltpu.VMEM((heads // 2, dv + BF16_ROWS, 2 * dqk), F32),
            pltpu.VMEM((heads, 1, LANES), F32),
            pltpu.VMEM((2 * heads, chunk), F32),
            pltpu.VMEM((chunk, LANES), F32),
        ],
        compiler_params=_params("parallel", "arbitrary"),
        name="mlstm_scan",
    )(qt, k, vt, opre, gates, gates, norm_h.reshape(1, dv_all))


def kernel(x, norm_mix_pre, norm_mix_post, norm_ffn_pre, norm_ffn_post, w_up, w_down,
           attn_w_qkv, attn_w_o, mlstm_w_in, mlstm_b_gates, mlstm_norm_h, mlstm_w_out):
    bsz, seq, d = x.shape
    m = bsz * seq
    heads = ATT_HEADS
    mh = MLSTM_HEADS
    dv = d // mh
    dqk = dv // 2

    q, k, km, vt = _qkv_proj(x, norm_mix_pre[0], attn_w_qkv[0], heads=heads, blk=MOBA_BLOCK,
                             tm=2 * MOBA_BLOCK)
    slopes = jnp.exp2(-8.0 * jnp.arange(1, heads + 1, dtype=F32) / heads)
    att = _moba_attention(q, k, vt, km, slopes, heads=heads, blk=MOBA_BLOCK, top=MOBA_TOPK, cb=2)
    w_up_b = w_up.astype(BF16)
    w_down_b = w_down.astype(BF16)
    h = _proj_mlp(att.reshape(m, d), x.reshape(m, d), attn_w_o[0].astype(BF16), norm_mix_post[0],
                  norm_ffn_pre[0], w_up_b, w_down_b, norm_ffn_post[0], layer=0, tm=512, tf=1024)

    w_in = mlstm_w_in[0]
    nq = mh * dqk
    wg = jnp.pad(w_in[:, 2 * nq + 2 * d:], ((0, 0), (0, LANES - 2 * mh))).astype(BF16)
    bg = jnp.pad(mlstm_b_gates[0], (0, LANES - 2 * mh)).reshape(1, LANES)
    qt, kmm, vmt, opre, gates = _mlstm_in_proj(h, norm_mix_pre[1], w_in, wg, bg,
                                               bsz=bsz, heads=mh, dqk=dqk, tm=512)
    y = _mlstm_scan(qt, kmm.reshape(bsz, seq, nq), vmt, opre.reshape(bsz, seq, d),
                    gates.reshape(bsz, seq, LANES), mlstm_norm_h[0], heads=mh, chunk=256)
    h = _proj_mlp(y.reshape(m, d), h, mlstm_w_out[0].astype(BF16), norm_mix_post[1],
                  norm_ffn_pre[1], w_up_b, w_down_b, norm_ffn_post[1], layer=1, tm=1024, tf=1024)
    return h.reshape(bsz, seq, d)
```

```python
import functools

import jax
import jax.numpy as jnp
from jax import lax
from jax.experimental import pallas as pl
from jax.experimental.pallas import tpu as pltpu

F32 = jnp.float32
BF16 = jnp.bfloat16

RMS_EPS = 1e-6
ATT_HEADS = 8
MOBA_BLOCK = 256
MOBA_TOPK = 3
MLSTM_HEADS = 8
GATE_SOFTCAP = 15.0

LANES = 128
VMEM_LIMIT = 56 * 1024 * 1024
NEG_BIG = -1e30
POS_BIG = 1e30
LOG2E = 1.4426950408889634
BF16_ROWS = 16
MOBA_DEPTH = 8
MOBA_ROUNDS = 2

_NT = (((1,), (1,)), ((), ()))


def _params(*sem):
    return pltpu.CompilerParams(dimension_semantics=sem, vmem_limit_bytes=VMEM_LIMIT)


def _rms_scale(x, gain):
    ms = jnp.mean(x * x, axis=-1, keepdims=True)
    return x * lax.rsqrt(ms + RMS_EPS) * gain


def _qkv_kernel(x_ref, g_ref, w_ref, q_ref, k_ref, km_ref, vt_ref, wq_ref, wk_ref, wvt_ref,
                *, scale, blk, heads):
    d = x_ref.shape[1]

    @pl.when(pl.program_id(0) == 0)
    def _():
        wq_ref[...] = w_ref[:, :d].astype(BF16)
        wk_ref[...] = w_ref[:, d:2 * d].astype(BF16)
        wvt_ref[...] = w_ref[:, 2 * d:].T.astype(BF16)

    xn = _rms_scale(x_ref[...], g_ref[...]).astype(BF16)
    tm = xn.shape[0]
    dh = d // heads
    q = jnp.dot(xn, wq_ref[...], preferred_element_type=F32)
    q_ref[...] = (q * scale).astype(BF16)
    k = jnp.dot(xn, wk_ref[...], preferred_element_type=F32)
    k_ref[...] = k.astype(BF16)
    for c in range(tm // blk):
        km_ref[c] = jnp.mean(k[c * blk:(c + 1) * blk], axis=0, keepdims=True)
    vt = lax.dot_general(wvt_ref[...], xn, _NT, preferred_element_type=F32)
    ones_row = (lax.broadcasted_iota(jnp.int32, (BF16_ROWS, blk), 0) == 0).astype(BF16)
    for h in range(heads):
        for c in range(tm // blk):
            vt_ref[0, h, c, :dh, :] = vt[h * dh:(h + 1) * dh, c * blk:(c + 1) * blk].astype(BF16)
            vt_ref[0, h, c, dh:, :] = ones_row


def _qkv_proj(x, gain, w_qkv, *, heads, blk, tm):
    bsz, seq, d = x.shape
    dh = d // heads
    nblk = seq // blk
    m = bsz * seq
    cpt = tm // blk
    tiles_per_b = seq // tm
    kern = functools.partial(_qkv_kernel, scale=dh ** -0.5 * LOG2E, blk=blk, heads=heads)
    const = lambda i: (0, 0)
    q, k, km, vt = pl.pallas_call(
        kern,
        grid=(m // tm,),
        in_specs=[
            pl.BlockSpec((tm, d), lambda i: (i, 0)),
            pl.BlockSpec((1, d), const),
            pl.BlockSpec((d, 3 * d), const, pipeline_mode=pl.Buffered(1)),
        ],
        out_specs=[
            pl.BlockSpec((tm, d), lambda i: (i, 0)),
            pl.BlockSpec((tm, d), lambda i: (i, 0)),
            pl.BlockSpec((cpt, 1, d), lambda i: (i, 0, 0)),
            pl.BlockSpec((1, heads, cpt, dh + BF16_ROWS, blk),
                         lambda i: (i // tiles_per_b, 0, i % tiles_per_b, 0, 0)),
        ],
        out_shape=[
            jax.ShapeDtypeStruct((m, d), BF16),
            jax.ShapeDtypeStruct((m, d), BF16),
            jax.ShapeDtypeStruct((m // blk, 1, d), F32),
            jax.ShapeDtypeStruct((bsz, heads, nblk, dh + BF16_ROWS, blk), BF16),
        ],
        scratch_shapes=[pltpu.VMEM((d, d), BF16)] * 3,
        compiler_params=_params("arbitrary"),
        name="qkv_proj",
    )(x.reshape(m, d), gain.reshape(1, d), w_qkv)
    return (q.reshape(bsz, seq, d), k.reshape(bsz, seq, d),
            km.reshape(bsz, nblk, d), vt)


def _moba_items(nblk, cb):
    tiles, chunks = [], []
    for i in range(nblk):
        for c in range(-(-(i + 1) // cb)):
            tiles.append(i)
            chunks.append(c)
    while len(tiles) % (MOBA_DEPTH * MOBA_ROUNDS):
        assert nblk >= 2 * cb
        tiles.append(0)
        chunks.append(1)
    n_items = len(tiles)
    tiles += [0] * (MOBA_DEPTH - 1)
    chunks += [0] * (MOBA_DEPTH - 1)
    return n_items, tiles, chunks


def _moba_kernel(slope_ref, tile_ref, chunk_ref, q_ref, k_ref, vt_ref, km_ref, o_ref,
                 sel_ref, bias_ref, m_ref, acc_ref, cmax_ref, *s_refs,
                 blk, nblk, top, cb, n_items, qc, dh):
    slope = slope_ref[pl.program_id(1)] * LOG2E

    kr = lax.broadcasted_iota(jnp.int32, (blk, blk), 0)
    qq = lax.broadcasted_iota(jnp.int32, (blk, blk), 1)
    base = kr.astype(F32) * slope
    bias_ref[0] = base
    bias_ref[1] = jnp.where(kr <= qq, base, NEG_BIG)

    m_ref[...] = jnp.full(m_ref.shape, NEG_BIG, F32)
    acc_ref[...] = jnp.zeros(acc_ref.shape, F32)

    km = km_ref[0]
    km_hi = km.astype(BF16)
    km_lo = (km - km_hi.astype(F32)).astype(BF16)
    row = lax.broadcasted_iota(jnp.int32, (nblk, qc), 0)
    rowf = row.astype(F32)
    tiles_per_qc = qc // blk

    def select(u, carry):
        q0 = pl.multiple_of(u * qc, qc)
        qs = q_ref[0, pl.ds(q0, qc), :]
        gate = (lax.dot_general(km_hi, qs, _NT, preferred_element_type=F32)
                + lax.dot_general(km_lo, qs, _NT, preferred_element_type=F32))
        qblk = u * tiles_per_qc + lax.broadcasted_iota(jnp.int32, (nblk, qc), 1) // blk
        past = row < qblk
        g = jnp.where(past, gate, -jnp.inf)
        picked = jnp.zeros((nblk, qc), F32)
        for _ in range(top):
            mx = jnp.max(g, axis=0, keepdims=True)
            first = jnp.min(jnp.where(g == mx, rowf, float(nblk)), axis=0, keepdims=True)
            hit = rowf == first
            picked = jnp.where(hit, 1.0, picked)
            g = jnp.where(hit, -jnp.inf, g)
        selv = jnp.where(((picked > 0.5) & past) | (row == qblk), 1.0, 0.0)
        for t in range(tiles_per_qc):
            sel_ref[u * tiles_per_qc + t] = selv[:, t * blk:(t + 1) * blk]
        return carry

    lax.fori_loop(0, (nblk * blk) // qc, select, 0, unroll=2)

    def scores(item, slot):
        i = tile_ref[item]
        c = chunk_ref[item]
        qi = q_ref[0, pl.ds(pl.multiple_of(i * blk, blk), blk), :]
        for jb in range(cb):
            j = c * cb + jb
            kb = k_ref[0, pl.ds(pl.multiple_of(j * blk, blk), blk), :]
            sb = (lax.dot_general(kb, qi, _NT, preferred_element_type=F32)
                  + bias_ref[jnp.where(j == i, 1, 0)])
            s_refs[slot][jb * blk:(jb + 1) * blk, :] = sb
            cmax_ref[slot, jb] = jnp.max(sb, axis=0, keepdims=True)

    def softmax_pv(item, slot):
        i = tile_ref[item]
        c = chunk_ref[item]
        m_old = m_ref[i]
        sels, cjs = [], []
        m_new = m_old
        for jb in range(cb):
            j = c * cb + jb
            cj = ((j - i) * blk).astype(F32) * slope
            sel = sel_ref[i, pl.ds(j, 1), :] > 0.5
            m_new = jnp.maximum(m_new, jnp.where(sel, cmax_ref[slot, jb] + cj, NEG_BIG))
            sels.append(sel)
            cjs.append(cj)
        alpha = jnp.exp2(m_old - m_new)
        pv = None
        for jb in range(cb):
            shift = jnp.where(sels[jb], m_new - cjs[jb], POS_BIG)
            p = jnp.exp2(s_refs[slot][jb * blk:(jb + 1) * blk, :] - shift)
            d = jnp.dot(vt_ref[0, 0, c * cb + jb], p.astype(BF16), preferred_element_type=F32)
            pv = d if pv is None else pv + d
        acc_ref[i] = alpha * acc_ref[i] + pv
        m_ref[i] = m_new

    for slot in range(MOBA_DEPTH - 1):
        scores(slot, slot)

    def group(kk, carry):
        for u in range(MOBA_DEPTH * MOBA_ROUNDS):
            t = MOBA_DEPTH * MOBA_ROUNDS * kk + u
            scores(t + MOBA_DEPTH - 1, (u + MOBA_DEPTH - 1) % MOBA_DEPTH)
            softmax_pv(t, u % MOBA_DEPTH)
        return carry

    lax.fori_loop(0, n_items // (MOBA_DEPTH * MOBA_ROUNDS), group, 0)

    def finish(i, carry):
        acc = acc_ref[i]
        o = acc[:dh] * (1.0 / acc[dh:dh + 1])
        o_ref[0, pl.ds(pl.multiple_of(i * blk, blk), blk), :] = o.T.astype(BF16)
        return carry

    lax.fori_loop(0, nblk, finish, 0, unroll=4)


def _moba_attention(q, k, vt, km, slopes, *, heads, blk, top, cb):
    bsz, seq, d = q.shape
    dh = d // heads
    nblk = seq // blk
    assert nblk % cb == 0
    qc = min(seq, 4 * blk)
    n_items, tiles, chunks = _moba_items(nblk, cb)
    kern = functools.partial(_moba_kernel, blk=blk, nblk=nblk, top=min(top, nblk), cb=cb,
                             n_items=n_items, qc=qc, dh=dh)
    smem = pl.BlockSpec(memory_space=pltpu.SMEM)
    return pl.pallas_call(
        kern,
        grid=(bsz, heads),
        in_specs=[
            smem, smem, smem,
            pl.BlockSpec((1, seq, dh), lambda b, h: (b, 0, h)),
            pl.BlockSpec((1, seq, dh), lambda b, h: (b, 0, h)),
            pl.BlockSpec((1, 1, nblk, dh + BF16_ROWS, blk), lambda b, h: (b, h, 0, 0, 0)),
            pl.BlockSpec((1, nblk, dh), lambda b, h: (b, 0, h)),
        ],
        out_specs=pl.BlockSpec((1, seq, dh), lambda b, h: (b, 0, h)),
        out_shape=jax.ShapeDtypeStruct((bsz, seq, d), BF16),
        scratch_shapes=[
            pltpu.VMEM((nblk, nblk, blk), F32),
            pltpu.VMEM((2, blk, blk), F32),
            pltpu.VMEM((nblk, 1, blk), F32),
            pltpu.VMEM((nblk, dh + BF16_ROWS, blk), F32),
            pltpu.VMEM((MOBA_DEPTH, cb, 1, blk), F32),
        ] + [pltpu.VMEM((cb * blk, blk), F32)] * MOBA_DEPTH,
        compiler_params=_params("parallel", "parallel"),
        name="moba_attention",
    )(slopes, jnp.asarray(tiles, jnp.int32), jnp.asarray(chunks, jnp.int32), q, k, vt, km)


def _proj_mlp_kernel(a_ref, h_ref, wo_ref, gm_ref, g1_ref, wu_ref, wd_ref, g2_ref, o_ref,
                     wo_s, wu_s, wd_s, *, tf, ncast):
    s = pl.program_id(0)
    wc = wu_ref.shape[1]

    @pl.when(s < ncast)
    def _():
        c0 = pl.multiple_of(s * wc, wc)
        wu_s[:, pl.ds(c0, wc)] = wu_ref[...].astype(BF16)
        wd_s[pl.ds(c0, wc), :] = wd_ref[...].astype(BF16)

    @pl.when(s == 0)
    def _():
        wo_s[...] = wo_ref[...].astype(BF16)

    @pl.when(s >= ncast)
    def _():
        u = jnp.dot(a_ref[...], wo_s[...], preferred_element_type=F32)
        h1 = h_ref[...] + _rms_scale(u, gm_ref[...])
        xn = _rms_scale(h1, g1_ref[...]).astype(BF16)
        ff = wu_s.shape[1]
        acc = None
        for c in range(ff // tf):
            t = jnp.dot(xn, wu_s[:, c * tf:(c + 1) * tf], preferred_element_type=F32)
            t = jnp.maximum(t, 0.0)
            d = jnp.dot((t * t).astype(BF16), wd_s[c * tf:(c + 1) * tf, :],
                        preferred_element_type=F32)
            acc = d if acc is None else acc + d
        o_ref[...] = h1 + _rms_scale(acc, g2_ref[...])


def _proj_mlp(a, h, w_o, g_mix, g_pre, w_up, w_down, g_post, *, layer, tm, tf, wc):
    m, d = h.shape
    kdim = a.shape[1]
    ff = w_up.shape[2]
    ncast = ff // wc
    row = lambda s: (jnp.maximum(s - ncast, 0), 0)
    const = lambda s: (0, 0)
    return pl.pallas_call(
        functools.partial(_proj_mlp_kernel, tf=tf, ncast=ncast),
        grid=(ncast + m // tm,),
        in_specs=[
            pl.BlockSpec((tm, kdim), row),
            pl.BlockSpec((tm, d), row),
            pl.BlockSpec((kdim, d), const, pipeline_mode=pl.Buffered(1)),
            pl.BlockSpec((1, d), const),
            pl.BlockSpec((1, d), const),
            pl.BlockSpec((None, d, wc), lambda s: (layer, 0, jnp.minimum(s, ncast - 1))),
            pl.BlockSpec((None, wc, d), lambda s: (layer, jnp.minimum(s, ncast - 1), 0)),
            pl.BlockSpec((1, d), const),
        ],
        out_specs=pl.BlockSpec((tm, d), row),
        out_shape=jax.ShapeDtypeStruct((m, d), F32),
        scratch_shapes=[
            pltpu.VMEM((kdim, d), BF16),
            pltpu.VMEM((d, ff), BF16),
            pltpu.VMEM((ff, d), BF16),
        ],
        compiler_params=_params("arbitrary"),
        name="proj_mlp",
    )(a, h, w_o, g_mix.reshape(1, d), g_pre.reshape(1, d), w_up, w_down, g_post.reshape(1, d))


def _mlstm_in_kernel(x_ref, g_ref, w_ref, wg_ref, bg_ref,
                     qt_ref, k_ref, vt_ref, op_ref, gates_ref,
                     wqt_ref, wk_ref, wvt_ref, wo_ref, *, heads, kscale):
    nq = wk_ref.shape[1]
    d = wo_ref.shape[1]

    @pl.when(pl.program_id(0) == 0)
    def _():
        wqt_ref[...] = w_ref[:, :nq].T.astype(BF16)
        wk_ref[...] = w_ref[:, nq:2 * nq].astype(BF16)
        wvt_ref[...] = w_ref[:, 2 * nq:2 * nq + d].T.astype(BF16)
        wo_ref[...] = w_ref[:, 2 * nq + d:2 * nq + 2 * d].astype(BF16)

    xn = _rms_scale(x_ref[...], g_ref[...]).astype(BF16)
    qt_ref[0] = lax.dot_general(wqt_ref[...], xn, _NT, preferred_element_type=F32).astype(BF16)
    k_ref[...] = (jnp.dot(xn, wk_ref[...], preferred_element_type=F32) * kscale).astype(BF16)
    vt_ref[0] = lax.dot_general(wvt_ref[...], xn, _NT, preferred_element_type=F32).astype(BF16)
    op_ref[...] = jnp.dot(xn, wo_ref[...], preferred_element_type=F32).astype(BF16)
    z = jnp.dot(xn, wg_ref[...], preferred_element_type=F32) + bg_ref[...]
    z = GATE_SOFTCAP * jnp.tanh(z * (1.0 / GATE_SOFTCAP))
    log_f = jnp.minimum(z, 0.0) - jnp.log1p(jnp.exp(-jnp.abs(z)))
    lane = lax.broadcasted_iota(jnp.int32, z.shape, 1)
    gates_ref[...] = jnp.where(lane < heads, z, log_f)


def _mlstm_in_proj(x, gain, w_in, wg, bg, *, bsz, heads, dqk, tm):
    m, d = x.shape
    seq = m // bsz
    nq = heads * dqk
    dv_all = d
    tiles_per_b = seq // tm
    kern = functools.partial(_mlstm_in_kernel, heads=heads, kscale=dqk ** -0.5)
    const = lambda i: (0, 0)
    row = lambda i: (i, 0)
    tcol = lambda i: (i // tiles_per_b, 0, i % tiles_per_b)
    return pl.pallas_call(
        kern,
        grid=(m // tm,),
        in_specs=[
            pl.BlockSpec((tm, d), row),
            pl.BlockSpec((1, d), const),
            pl.BlockSpec(w_in.shape, const, pipeline_mode=pl.Buffered(1)),
            pl.BlockSpec((d, LANES), const),
            pl.BlockSpec((1, LANES), const),
        ],
        out_specs=[
            pl.BlockSpec((1, nq, tm), tcol),
            pl.BlockSpec((tm, nq), row),
            pl.BlockSpec((1, dv_all, tm), tcol),
            pl.BlockSpec((tm, dv_all), row),
            pl.BlockSpec((tm, LANES), row),
        ],
        out_shape=[
            jax.ShapeDtypeStruct((bsz, nq, seq), BF16),
            jax.ShapeDtypeStruct((m, nq), BF16),
            jax.ShapeDtypeStruct((bsz, dv_all, seq), BF16),
            jax.ShapeDtypeStruct((m, dv_all), BF16),
            jax.ShapeDtypeStruct((m, LANES), F32),
        ],
        scratch_shapes=[
            pltpu.VMEM((nq, d), BF16),
            pltpu.VMEM((d, nq), BF16),
            pltpu.VMEM((dv_all, d), BF16),
            pltpu.VMEM((d, dv_all), BF16),
        ],
        compiler_params=_params("arbitrary"),
        name="mlstm_in_proj",
    )(x, gain.reshape(1, d), w_in, wg, bg)


def _mlstm_kernel(qt_ref, k_ref, vt_ref, op_ref, g_ref, gnext_ref, nh_ref, y_ref,
                  c_ref, m_ref, rows_ref, ucol_ref, *, heads, dqk, dv, chunk):
    L = chunk
    ss = lax.broadcasted_iota(jnp.int32, (L, L), 0)
    tt = lax.broadcasted_iota(jnp.int32, (L, L), 1)
    causal = ss <= tt

    def gate_terms(gates):
        gates_t = gates.T
        i_rows = gates_t[0:heads]
        f_rows = gates_t[heads:2 * heads]
        tri = causal.astype(BF16)
        f_hi = f_rows.astype(BF16)
        f_r1 = f_rows - f_hi.astype(F32)
        f_mid = f_r1.astype(BF16)
        f_lo = (f_r1 - f_mid.astype(F32)).astype(BF16)
        b_rows = (jnp.dot(f_hi, tri, preferred_element_type=F32)
                  + jnp.dot(f_mid, tri, preferred_element_type=F32)
                  + jnp.dot(f_lo, tri, preferred_element_type=F32))
        rows_ref[0:heads] = i_rows
        rows_ref[heads:2 * heads] = b_rows
        ucol_ref[...] = jnp.concatenate(
            [i_rows - b_rows, jnp.zeros((LANES - heads, L), F32)], axis=0).T

    @pl.when(pl.program_id(1) == 0)
    def _():
        c_ref[...] = jnp.zeros_like(c_ref)
        m_ref[...] = jnp.zeros_like(m_ref)
        gate_terms(g_ref[0])

    ones_rows = (lax.broadcasted_iota(jnp.int32, (BF16_ROWS, L), 0) == 0).astype(BF16)
    sub = lax.broadcasted_iota(jnp.int32, (2 * dqk, 1), 0)
    lane = lax.broadcasted_iota(jnp.int32, (1, 2 * dqk), 1)

    for p in range(heads // 2):
        qt_pair = qt_ref[0, p * 2 * dqk:(p + 1) * 2 * dqk, :]
        kg = k_ref[0, :, p * 2 * dqk:(p + 1) * 2 * dqk]
        ct = c_ref[p]
        ct_b = ct.astype(BF16)
        upds, decays = [], []
        for half in range(2):
            h = 2 * p + half
            in_head = (sub >= half * dqk) & (sub < (half + 1) * dqk)
            qt_h = jnp.where(in_head, qt_pair, jnp.zeros_like(qt_pair))
            b_row = rows_ref[heads + h:heads + h + 1, :]
            i_row = rows_ref[h:h + 1, :]
            m_prev = m_ref[h][:, 0:1]
            log_d = jnp.where(causal, ucol_ref[:, h:h + 1] + b_row, -jnp.inf)
            log_inter = b_row + m_prev
            m_t = jnp.maximum(log_inter, jnp.max(log_d, axis=0, keepdims=True))
            dmat = jnp.exp(log_d - m_t)
            w_inter = jnp.exp(log_inter - m_t)
            s_t = jnp.dot(kg, qt_h, preferred_element_type=F32) * dmat
            vext = jnp.concatenate([vt_ref[0, h * dv:(h + 1) * dv, :], ones_rows], axis=0)
            num = (w_inter * jnp.dot(ct_b, qt_h, preferred_element_type=F32)
                   + jnp.dot(vext, s_t.astype(BF16), preferred_element_type=F32))
            den = num[dv:dv + 1]
            hout = num[:dv] * (1.0 / jnp.maximum(jnp.abs(den), jnp.exp(-m_t)))
            hn = hout * lax.rsqrt(jnp.mean(hout * hout, axis=0, keepdims=True) + RMS_EPS)
            og = jax.nn.sigmoid(op_ref[0, :, h * dv:(h + 1) * dv].astype(F32))
            y_ref[0, :, h * dv:(h + 1) * dv] = (
                og * (hn.T * nh_ref[:, h * dv:(h + 1) * dv])).astype(BF16)
            b_end = b_row[:, L - 1:L]
            log_w = b_end - b_row + i_row
            m_new = jnp.maximum(b_end + m_prev, jnp.max(log_w, axis=1, keepdims=True))
            decays.append(jnp.exp(b_end + m_prev - m_new))
            vw = (vext.astype(F32) * jnp.exp(log_w - m_new)).astype(BF16)
            upds.append(jnp.dot(vw, kg, preferred_element_type=F32))
            m_ref[h] = jnp.broadcast_to(m_new, (1, LANES))
        first = lane < dqk
        c_ref[p] = jnp.where(first, decays[0], decays[1]) * ct + jnp.where(first, upds[0], upds[1])

    gate_terms(gnext_ref[0])


def _mlstm_scan(qt, k, vt, opre, gates, norm_h, *, heads, chunk):
    bsz, dv_all, seq = vt.shape
    dv = dv_all // heads
    dqk = qt.shape[1] // heads
    kern = functools.partial(_mlstm_kernel, heads=heads, dqk=dqk, dv=dv, chunk=chunk)
    nchunk = seq // chunk
    rows = lambda b, c: (b, c, 0)
    cols = lambda b, c: (b, 0, c)
    return pl.pallas_call(
        kern,
        grid=(bsz, seq // chunk),
        in_specs=[
            pl.BlockSpec((1, heads * dqk, chunk), cols),
            pl.BlockSpec((1, chunk, heads * dqk), rows),
            pl.BlockSpec((1, dv_all, chunk), cols),
            pl.BlockSpec((1, chunk, dv_all), rows),
            pl.BlockSpec((1, chunk, LANES), rows),
            pl.BlockSpec((1, chunk, LANES), lambda b, c: (b, jnp.minimum(c + 1, nchunk - 1), 0)),
            pl.BlockSpec((1, dv_all), lambda b, c: (0, 0)),
        ],
        out_specs=pl.BlockSpec((1, chunk, dv_all), rows),
        out_shape=jax.ShapeDtypeStruct((bsz, seq, dv_all), BF16),
        scratch_shapes=[
            pltpu.VMEM((heads // 2, dv + BF16_ROWS, 2 * dqk), F32),
            pltpu.VMEM((heads, 1, LANES), F32),
            pltpu.VMEM((2 * heads, chunk), F32),
            pltpu.VMEM((chunk, LANES), F32),
        ],
        compiler_params=_params("parallel", "arbitrary"),
        name="mlstm_scan",
    )(qt, k, vt, opre, gates, gates, norm_h.reshape(1, dv_all))


def kernel(x, norm_mix_pre, norm_mix_post, norm_ffn_pre, norm_ffn_post, w_up, w_down,
           attn_w_qkv, attn_w_o, mlstm_w_in, mlstm_b_gates, mlstm_norm_h, mlstm_w_out):
    bsz, seq, d = x.shape
    m = bsz * seq
    heads = ATT_HEADS
    mh = MLSTM_HEADS
    dv = d // mh
    dqk = dv // 2

    q, k, km, vt = _qkv_proj(x, norm_mix_pre[0], attn_w_qkv[0], heads=heads, blk=MOBA_BLOCK,
                             tm=4 * MOBA_BLOCK)
    slopes = jnp.exp2(-8.0 * jnp.arange(1, heads + 1, dtype=F32) / heads)
    att = _moba_attention(q, k, vt, km, slopes, heads=heads, blk=MOBA_BLOCK, top=MOBA_TOPK, cb=2)
    h = _proj_mlp(att.reshape(m, d), x.reshape(m, d), attn_w_o[0], norm_mix_post[0],
                  norm_ffn_pre[0], w_up, w_down, norm_ffn_post[0], layer=0, tm=512, tf=1024, wc=512)

    w_in = mlstm_w_in[0]
    nq = mh * dqk
    wg = jnp.pad(w_in[:, 2 * nq + 2 * d:], ((0, 0), (0, LANES - 2 * mh))).astype(BF16)
    bg = jnp.pad(mlstm_b_gates[0], (0, LANES - 2 * mh)).reshape(1, LANES)
    qt, kmm, vmt, opre, gates = _mlstm_in_proj(h, norm_mix_pre[1], w_in, wg, bg,
                                               bsz=bsz, heads=mh, dqk=dqk, tm=1024)
    y = _mlstm_scan(qt, kmm.reshape(bsz, seq, nq), vmt, opre.reshape(bsz, seq, d),
                    gates.reshape(bsz, seq, LANES), mlstm_norm_h[0], heads=mh, chunk=256)
    h = _proj_mlp(y.reshape(m, d), h, mlstm_w_out[0], norm_mix_post[1],
                  norm_ffn_pre[1], w_up, w_down, norm_ffn_post[1], layer=1, tm=512, tf=1024, wc=512)
    return h.reshape(bsz, seq, d)
```

```python
import functools

import jax
import jax.numpy as jnp
from jax import lax
from jax.experimental import pallas as pl
from jax.experimental.pallas import tpu as pltpu

F32 = jnp.float32
BF16 = jnp.bfloat16

RMS_EPS = 1e-6
ATT_HEADS = 8
MOBA_BLOCK = 256
MOBA_TOPK = 3
MLSTM_HEADS = 8
GATE_SOFTCAP = 15.0

LANES = 128
VMEM_LIMIT = 56 * 1024 * 1024
NEG_BIG = -1e30
POS_BIG = 1e30
LOG2E = 1.4426950408889634
BF16_ROWS = 16
MOBA_DEPTH = 8
MOBA_ROUNDS = 2

_NT = (((1,), (1,)), ((), ()))


def _params(*sem):
    return pltpu.CompilerParams(dimension_semantics=sem, vmem_limit_bytes=VMEM_LIMIT)


def _rms_scale(x, gain):
    ms = jnp.mean(x * x, axis=-1, keepdims=True)
    return x * lax.rsqrt(ms + RMS_EPS) * gain


def _qkv_kernel(x_ref, g_ref, w_ref, q_ref, k_ref, km_ref, vt_ref, wq_ref, wk_ref, wvt_ref,
                *, scale, blk, heads):
    d = x_ref.shape[1]

    @pl.when(pl.program_id(0) == 0)
    def _():
        wq_ref[...] = w_ref[:, :d].astype(BF16)
        wk_ref[...] = w_ref[:, d:2 * d].astype(BF16)
        wvt_ref[...] = w_ref[:, 2 * d:].T.astype(BF16)

    xn = _rms_scale(x_ref[...], g_ref[...]).astype(BF16)
    tm = xn.shape[0]
    dh = d // heads
    q = jnp.dot(xn, wq_ref[...], preferred_element_type=F32)
    q_ref[...] = (q * scale).astype(BF16)
    k = jnp.dot(xn, wk_ref[...], preferred_element_type=F32)
    k_ref[...] = k.astype(BF16)
    for c in range(tm // blk):
        km_ref[c] = jnp.mean(k[c * blk:(c + 1) * blk], axis=0, keepdims=True)
    vt = lax.dot_general(wvt_ref[...], xn, _NT, preferred_element_type=F32)
    ones_row = (lax.broadcasted_iota(jnp.int32, (BF16_ROWS, blk), 0) == 0).astype(BF16)
    for h in range(heads):
        for c in range(tm // blk):
            vt_ref[0, h, c, :dh, :] = vt[h * dh:(h + 1) * dh, c * blk:(c + 1) * blk].astype(BF16)
            vt_ref[0, h, c, dh:, :] = ones_row


def _qkv_proj(x, gain, w_qkv, *, heads, blk, tm):
    bsz, seq, d = x.shape
    dh = d // heads
    nblk = seq // blk
    m = bsz * seq
    cpt = tm // blk
    tiles_per_b = seq // tm
    kern = functools.partial(_qkv_kernel, scale=dh ** -0.5 * LOG2E, blk=blk, heads=heads)
    const = lambda i: (0, 0)
    q, k, km, vt = pl.pallas_call(
        kern,
        grid=(m // tm,),
        in_specs=[
            pl.BlockSpec((tm, d), lambda i: (i, 0)),
            pl.BlockSpec((1, d), const),
            pl.BlockSpec((d, 3 * d), const, pipeline_mode=pl.Buffered(1)),
        ],
        out_specs=[
            pl.BlockSpec((tm, d), lambda i: (i, 0)),
            pl.BlockSpec((tm, d), lambda i: (i, 0)),
            pl.BlockSpec((cpt, 1, d), lambda i: (i, 0, 0)),
            pl.BlockSpec((1, heads, cpt, dh + BF16_ROWS, blk),
                         lambda i: (i // tiles_per_b, 0, i % tiles_per_b, 0, 0)),
        ],
        out_shape=[
            jax.ShapeDtypeStruct((m, d), BF16),
            jax.ShapeDtypeStruct((m, d), BF16),
            jax.ShapeDtypeStruct((m // blk, 1, d), F32),
            jax.ShapeDtypeStruct((bsz, heads, nblk, dh + BF16_ROWS, blk), BF16),
        ],
        scratch_shapes=[pltpu.VMEM((d, d), BF16)] * 3,
        compiler_params=_params("arbitrary"),
        name="qkv_proj",
    )(x.reshape(m, d), gain.reshape(1, d), w_qkv)
    return (q.reshape(bsz, seq, d), k.reshape(bsz, seq, d),
            km.reshape(bsz, nblk, d), vt)


def _moba_items(nblk, cb):
    tiles, chunks = [], []
    for i in range(nblk):
        for c in range(-(-(i + 1) // cb)):
            tiles.append(i)
            chunks.append(c)
    while len(tiles) % (MOBA_DEPTH * MOBA_ROUNDS):
        assert nblk >= 2 * cb
        tiles.append(0)
        chunks.append(1)
    n_items = len(tiles)
    tiles += [0] * (MOBA_DEPTH - 1)
    chunks += [0] * (MOBA_DEPTH - 1)
    return n_items, tiles, chunks


def _moba_kernel(slope_ref, tile_ref, chunk_ref, q_ref, k_ref, vt_ref, km_ref, o_ref,
                 sel_ref, bias_ref, m_ref, acc_ref, cmax_ref, *s_refs,
                 blk, nblk, top, cb, n_items, qc, dh):
    slope = slope_ref[pl.program_id(1)] * LOG2E

    kr = lax.broadcasted_iota(jnp.int32, (blk, blk), 0)
    qq = lax.broadcasted_iota(jnp.int32, (blk, blk), 1)
    base = kr.astype(F32) * slope
    bias_ref[0] = base
    bias_ref[1] = jnp.where(kr <= qq, base, NEG_BIG)

    m_ref[...] = jnp.full(m_ref.shape, NEG_BIG, F32)
    acc_ref[...] = jnp.zeros(acc_ref.shape, F32)

    km = km_ref[0]
    km_hi = km.astype(BF16)
    km_lo = (km - km_hi.astype(F32)).astype(BF16)
    row = lax.broadcasted_iota(jnp.int32, (nblk, qc), 0)
    rowf = row.astype(F32)
    tiles_per_qc = qc // blk

    def select(u, carry):
        q0 = pl.multiple_of(u * qc, qc)
        qs = q_ref[0, pl.ds(q0, qc), :]
        gate = (lax.dot_general(km_hi, qs, _NT, preferred_element_type=F32)
                + lax.dot_general(km_lo, qs, _NT, preferred_element_type=F32))
        qblk = u * tiles_per_qc + lax.broadcasted_iota(jnp.int32, (nblk, qc), 1) // blk
        past = row < qblk
        g = jnp.where(past, gate, -jnp.inf)
        picked = jnp.zeros((nblk, qc), F32)
        for _ in range(top):
            mx = jnp.max(g, axis=0, keepdims=True)
            first = jnp.min(jnp.where(g == mx, rowf, float(nblk)), axis=0, keepdims=True)
            hit = rowf == first
            picked = jnp.where(hit, 1.0, picked)
            g = jnp.where(hit, -jnp.inf, g)
        selv = jnp.where(((picked > 0.5) & past) | (row == qblk), 1.0, 0.0)
        for t in range(tiles_per_qc):
            sel_ref[u * tiles_per_qc + t] = selv[:, t * blk:(t + 1) * blk]
        return carry

    lax.fori_loop(0, (nblk * blk) // qc, select, 0, unroll=4)

    def scores(item, slot):
        i = tile_ref[item]
        c = chunk_ref[item]
        qi = q_ref[0, pl.ds(pl.multiple_of(i * blk, blk), blk), :]
        for jb in range(cb):
            j = c * cb + jb
            kb = k_ref[0, pl.ds(pl.multiple_of(j * blk, blk), blk), :]
            sb = (lax.dot_general(kb, qi, _NT, preferred_element_type=F32)
                  + bias_ref[jnp.where(j == i, 1, 0)])
            s_refs[slot][jb * blk:(jb + 1) * blk, :] = sb
            cmax_ref[slot, jb] = jnp.max(sb, axis=0, keepdims=True)

    def softmax_pv(item, slot):
        i = tile_ref[item]
        c = chunk_ref[item]
        m_old = m_ref[i]
        sels, cjs = [], []
        m_new = m_old
        for jb in range(cb):
            j = c * cb + jb
            cj = ((j - i) * blk).astype(F32) * slope
            sel = sel_ref[i, pl.ds(j, 1), :] > 0.5
            m_new = jnp.maximum(m_new, jnp.where(sel, cmax_ref[slot, jb] + cj, NEG_BIG))
            sels.append(sel)
            cjs.append(cj)
        alpha = jnp.exp2(m_old - m_new)
        pv = None
        for jb in range(cb):
            shift = jnp.where(sels[jb], m_new - cjs[jb], POS_BIG)
            p = jnp.exp2(s_refs[slot][jb * blk:(jb + 1) * blk, :] - shift)
            d = jnp.dot(vt_ref[0, 0, c * cb + jb], p.astype(BF16), preferred_element_type=F32)
            pv = d if pv is None else pv + d
        acc_ref[i] = alpha * acc_ref[i] + pv
        m_ref[i] = m_new

    for slot in range(MOBA_DEPTH - 1):
        scores(slot, slot)

    def group(kk, carry):
        for u in range(MOBA_DEPTH * MOBA_ROUNDS):
            t = MOBA_DEPTH * MOBA_ROUNDS * kk + u
            scores(t + MOBA_DEPTH - 1, (u + MOBA_DEPTH - 1) % MOBA_DEPTH)
            softmax_pv(t, u % MOBA_DEPTH)
        return carry

    lax.fori_loop(0, n_items // (MOBA_DEPTH * MOBA_ROUNDS), group, 0)

    def finish(i, carry):
        acc = acc_ref[i]
        o = acc[:dh] * (1.0 / acc[dh:dh + 1])
        o_ref[0, pl.ds(pl.multiple_of(i * blk, blk), blk), :] = o.T.astype(BF16)
        return carry

    lax.fori_loop(0, nblk, finish, 0, unroll=8)


def _moba_attention(q, k, vt, km, slopes, *, heads, blk, top, cb):
    bsz, seq, d = q.shape
    dh = d // heads
    nblk = seq // blk
    assert nblk % cb == 0
    qc = min(seq, 4 * blk)
    n_items, tiles, chunks = _moba_items(nblk, cb)
    kern = functools.partial(_moba_kernel, blk=blk, nblk=nblk, top=min(top, nblk), cb=cb,
                             n_items=n_items, qc=qc, dh=dh)
    smem = pl.BlockSpec(memory_space=pltpu.SMEM)
    return pl.pallas_call(
        kern,
        grid=(bsz, heads),
        in_specs=[
            smem, smem, smem,
            pl.BlockSpec((1, seq, dh), lambda b, h: (b, 0, h)),
            pl.BlockSpec((1, seq, dh), lambda b, h: (b, 0, h)),
            pl.BlockSpec((1, 1, nblk, dh + BF16_ROWS, blk), lambda b, h: (b, h, 0, 0, 0)),
            pl.BlockSpec((1, nblk, dh), lambda b, h: (b, 0, h)),
        ],
        out_specs=pl.BlockSpec((1, seq, dh), lambda b, h: (b, 0, h)),
        out_shape=jax.ShapeDtypeStruct((bsz, seq, d), BF16),
        scratch_shapes=[
            pltpu.VMEM((nblk, nblk, blk), F32),
            pltpu.VMEM((2, blk, blk), F32),
            pltpu.VMEM((nblk, 1, blk), F32),
            pltpu.VMEM((nblk, dh + BF16_ROWS, blk), F32),
            pltpu.VMEM((MOBA_DEPTH, cb, 1, blk), F32),
        ] + [pltpu.VMEM((cb * blk, blk), F32)] * MOBA_DEPTH,
        compiler_params=_params("parallel", "parallel"),
        name="moba_attention",
    )(slopes, jnp.asarray(tiles, jnp.int32), jnp.asarray(chunks, jnp.int32), q, k, vt, km)


def _proj_mlp_kernel(a_ref, h_ref, wo_ref, gm_ref, g1_ref, wu_ref, wd_ref, g2_ref, o_ref,
                     wo_s, wu_s, wd_s, *, tf, ncast):
    s = pl.program_id(0)
    wc = wu_ref.shape[1]

    @pl.when(s < ncast)
    def _():
        c0 = pl.multiple_of(s * wc, wc)
        wu_s[:, pl.ds(c0, wc)] = wu_ref[...].astype(BF16)
        wd_s[pl.ds(c0, wc), :] = wd_ref[...].astype(BF16)

    @pl.when(s == 0)
    def _():
        wo_s[...] = wo_ref[...].astype(BF16)

    @pl.when(s >= ncast)
    def _():
        u = jnp.dot(a_ref[...], wo_s[...], preferred_element_type=F32)
        h1 = h_ref[...] + _rms_scale(u, gm_ref[...])
        xn = _rms_scale(h1, g1_ref[...]).astype(BF16)
        ff = wu_s.shape[1]
        acc = None
        for c in range(ff // tf):
            t = jnp.dot(xn, wu_s[:, c * tf:(c + 1) * tf], preferred_element_type=F32)
            t = jnp.maximum(t, 0.0)
            d = jnp.dot((t * t).astype(BF16), wd_s[c * tf:(c + 1) * tf, :],
                        preferred_element_type=F32)
            acc = d if acc is None else acc + d
        o_ref[...] = h1 + _rms_scale(acc, g2_ref[...])


def _proj_mlp(a, h, w_o, g_mix, g_pre, w_up, w_down, g_post, *, layer, tm, tf, wc):
    m, d = h.shape
    kdim = a.shape[1]
    ff = w_up.shape[2]
    ncast = ff // wc
    row = lambda s: (jnp.maximum(s - ncast, 0), 0)
    const = lambda s: (0, 0)
    return pl.pallas_call(
        functools.partial(_proj_mlp_kernel, tf=tf, ncast=ncast),
        grid=(ncast + m // tm,),
        in_specs=[
            pl.BlockSpec((tm, kdim), row),
            pl.BlockSpec((tm, d), row),
            pl.BlockSpec((kdim, d), const, pipeline_mode=pl.Buffered(1)),
            pl.BlockSpec((1, d), const),
            pl.BlockSpec((1, d), const),
            pl.BlockSpec((None, d, wc), lambda s: (layer, 0, jnp.minimum(s, ncast - 1))),
            pl.BlockSpec((None, wc, d), lambda s: (layer, jnp.minimum(s, ncast - 1), 0)),
            pl.BlockSpec((1, d), const),
        ],
        out_specs=pl.BlockSpec((tm, d), row),
        out_shape=jax.ShapeDtypeStruct((m, d), F32),
        scratch_shapes=[
            pltpu.VMEM((kdim, d), BF16),
            pltpu.VMEM((d, ff), BF16),
            pltpu.VMEM((ff, d), BF16),
        ],
        compiler_params=_params("arbitrary"),
        name="proj_mlp",
    )(a, h, w_o, g_mix.reshape(1, d), g_pre.reshape(1, d), w_up, w_down, g_post.reshape(1, d))


def _mlstm_in_kernel(x_ref, g_ref, w_ref, wg_ref, bg_ref,
                     qt_ref, k_ref, vt_ref, op_ref, gates_ref,
                     wqt_ref, wvt_ref, *, heads, kscale):
    nq = wqt_ref.shape[0]
    d = wvt_ref.shape[0]

    @pl.when(pl.program_id(0) == 0)
    def _():
        wqt_ref[...] = w_ref[:, :nq].astype(F32).T.astype(BF16)
        wvt_ref[...] = w_ref[:, 2 * nq:2 * nq + d].astype(F32).T.astype(BF16)

    xn = _rms_scale(x_ref[...], g_ref[...]).astype(BF16)
    qt_ref[0] = lax.dot_general(wqt_ref[...], xn, _NT, preferred_element_type=F32).astype(BF16)
    k_ref[...] = (jnp.dot(xn, w_ref[:, nq:2 * nq], preferred_element_type=F32) * kscale).astype(BF16)
    vt_ref[0] = lax.dot_general(wvt_ref[...], xn, _NT, preferred_element_type=F32).astype(BF16)
    op_ref[...] = jnp.dot(xn, w_ref[:, 2 * nq + d:2 * nq + 2 * d],
                          preferred_element_type=F32).astype(BF16)
    z = jnp.dot(xn, wg_ref[...], preferred_element_type=F32) + bg_ref[...]
    z = GATE_SOFTCAP * jnp.tanh(z * (1.0 / GATE_SOFTCAP))
    log_f = jnp.minimum(z, 0.0) - jnp.log1p(jnp.exp(-jnp.abs(z)))
    lane = lax.broadcasted_iota(jnp.int32, z.shape, 1)
    gates_ref[...] = jnp.where(lane < heads, z, log_f)


def _mlstm_in_proj(x, gain, w_in, wg, bg, *, bsz, heads, dqk, tm):
    m, d = x.shape
    seq = m // bsz
    nq = heads * dqk
    dv_all = d
    tiles_per_b = seq // tm
    kern = functools.partial(_mlstm_in_kernel, heads=heads, kscale=dqk ** -0.5)
    const = lambda i: (0, 0)
    row = lambda i: (i, 0)
    tcol = lambda i: (i // tiles_per_b, 0, i % tiles_per_b)
    return pl.pallas_call(
        kern,
        grid=(m // tm,),
        in_specs=[
            pl.BlockSpec((tm, d), row),
            pl.BlockSpec((1, d), const),
            pl.BlockSpec(w_in.shape, const, pipeline_mode=pl.Buffered(1)),
            pl.BlockSpec((d, LANES), const),
            pl.BlockSpec((1, LANES), const),
        ],
        out_specs=[
            pl.BlockSpec((1, nq, tm), tcol),
            pl.BlockSpec((tm, nq), row),
            pl.BlockSpec((1, dv_all, tm), tcol),
            pl.BlockSpec((tm, dv_all), row),
            pl.BlockSpec((tm, LANES), row),
        ],
        out_shape=[
            jax.ShapeDtypeStruct((bsz, nq, seq), BF16),
            jax.ShapeDtypeStruct((m, nq), BF16),
            jax.ShapeDtypeStruct((bsz, dv_all, seq), BF16),
            jax.ShapeDtypeStruct((m, dv_all), BF16),
            jax.ShapeDtypeStruct((m, LANES), F32),
        ],
        scratch_shapes=[
            pltpu.VMEM((nq, d), BF16),
            pltpu.VMEM((dv_all, d), BF16),
        ],
        compiler_params=_params("arbitrary"),
        name="mlstm_in_proj",
    )(x, gain.reshape(1, d), w_in, wg, bg)


def _mlstm_kernel(qt_ref, k_ref, vt_ref, op_ref, g_ref, gnext_ref, nh_ref, y_ref,
                  c_ref, m_ref, rows_ref, ucol_ref, *, heads, dqk, dv, chunk):
    L = chunk
    ss = lax.broadcasted_iota(jnp.int32, (L, L), 0)
    tt = lax.broadcasted_iota(jnp.int32, (L, L), 1)
    causal = ss <= tt

    def gate_terms(gates):
        gates_t = gates.T
        i_rows = gates_t[0:heads]
        f_rows = gates_t[heads:2 * heads]
        tri = causal.astype(BF16)
        f_hi = f_rows.astype(BF16)
        f_r1 = f_rows - f_hi.astype(F32)
        f_mid = f_r1.astype(BF16)
        f_lo = (f_r1 - f_mid.astype(F32)).astype(BF16)
        b_rows = (jnp.dot(f_hi, tri, preferred_element_type=F32)
                  + jnp.dot(f_mid, tri, preferred_element_type=F32)
                  + jnp.dot(f_lo, tri, preferred_element_type=F32))
        rows_ref[0:heads] = i_rows
        rows_ref[heads:2 * heads] = b_rows
        ucol_ref[...] = jnp.concatenate(
            [i_rows - b_rows, jnp.zeros((LANES - heads, L), F32)], axis=0).T

    @pl.when(pl.program_id(1) == 0)
    def _():
        c_ref[...] = jnp.zeros_like(c_ref)
        m_ref[...] = jnp.zeros_like(m_ref)
        gate_terms(g_ref[0])

    ones_rows = (lax.broadcasted_iota(jnp.int32, (BF16_ROWS, L), 0) == 0).astype(BF16)
    sub = lax.broadcasted_iota(jnp.int32, (2 * dqk, 1), 0)
    lane = lax.broadcasted_iota(jnp.int32, (1, 2 * dqk), 1)

    first = lane < dqk
    kgs, cts, s_qk, inter = [], [], [], []
    for h in range(heads):
        p, half = divmod(h, 2)
        qt_pair = qt_ref[0, p * 2 * dqk:(p + 1) * 2 * dqk, :]
        kg = k_ref[0, :, p * 2 * dqk:(p + 1) * 2 * dqk]
        in_head = (sub >= half * dqk) & (sub < (half + 1) * dqk)
        qt_h = jnp.where(in_head, qt_pair, jnp.zeros_like(qt_pair))
        ct = c_ref[p]
        kgs.append(kg)
        cts.append(ct)
        s_qk.append(jnp.dot(kg, qt_h, preferred_element_type=F32))
        inter.append(jnp.dot(ct.astype(BF16), qt_h, preferred_element_type=F32))

    s_ts, m_ts, w_inters, vexts = [], [], [], []
    for h in range(heads):
        b_row = rows_ref[heads + h:heads + h + 1, :]
        m_prev = m_ref[h][:, 0:1]
        log_d = jnp.where(causal, ucol_ref[:, h:h + 1] + b_row, -jnp.inf)
        log_inter = b_row + m_prev
        m_t = jnp.maximum(log_inter, jnp.max(log_d, axis=0, keepdims=True))
        s_ts.append((s_qk[h] * jnp.exp(log_d - m_t)).astype(BF16))
        w_inters.append(jnp.exp(log_inter - m_t))
        m_ts.append(m_t)
        vexts.append(jnp.concatenate([vt_ref[0, h * dv:(h + 1) * dv, :], ones_rows], axis=0))

    for h in range(heads):
        num = (w_inters[h] * inter[h]
               + jnp.dot(vexts[h], s_ts[h], preferred_element_type=F32))
        den = num[dv:dv + 1]
        hout = num[:dv] * (1.0 / jnp.maximum(jnp.abs(den), jnp.exp(-m_ts[h])))
        hn = hout * lax.rsqrt(jnp.mean(hout * hout, axis=0, keepdims=True) + RMS_EPS)
        og = jax.nn.sigmoid(op_ref[0, :, h * dv:(h + 1) * dv].astype(F32))
        y_ref[0, :, h * dv:(h + 1) * dv] = (
            og * (hn.T * nh_ref[:, h * dv:(h + 1) * dv])).astype(BF16)

    upds, decays = [], []
    for h in range(heads):
        b_row = rows_ref[heads + h:heads + h + 1, :]
        i_row = rows_ref[h:h + 1, :]
        m_prev = m_ref[h][:, 0:1]
        b_end = b_row[:, L - 1:L]
        log_w = b_end - b_row + i_row
        m_new = jnp.maximum(b_end + m_prev, jnp.max(log_w, axis=1, keepdims=True))
        decays.append(jnp.exp(b_end + m_prev - m_new))
        vw = (vexts[h].astype(F32) * jnp.exp(log_w - m_new)).astype(BF16)
        upds.append(jnp.dot(vw, kgs[h], preferred_element_type=F32))
        m_ref[h] = jnp.broadcast_to(m_new, (1, LANES))
    for p in range(heads // 2):
        c_ref[p] = (jnp.where(first, decays[2 * p], decays[2 * p + 1]) * cts[2 * p]
                    + jnp.where(first, upds[2 * p], upds[2 * p + 1]))

    gate_terms(gnext_ref[0])


def _mlstm_scan(qt, k, vt, opre, gates, norm_h, *, heads, chunk):
    bsz, dv_all, seq = vt.shape
    dv = dv_all // heads
    dqk = qt.shape[1] // heads
    kern = functools.partial(_mlstm_kernel, heads=heads, dqk=dqk, dv=dv, chunk=chunk)
    nchunk = seq // chunk
    rows = lambda b, c: (b, c, 0)
    cols = lambda b, c: (b, 0, c)
    return pl.pallas_call(
        kern,
        grid=(bsz, seq // chunk),
        in_specs=[
            pl.BlockSpec((1, heads * dqk, chunk), cols),
            pl.BlockSpec((1, chunk, heads * dqk), rows),
            pl.BlockSpec((1, dv_all, chunk), cols),
            pl.BlockSpec((1, chunk, dv_all), rows),
            pl.BlockSpec((1, chunk, LANES), rows),
            pl.BlockSpec((1, chunk, LANES), lambda b, c: (b, jnp.minimum(c + 1, nchunk - 1), 0)),
            pl.BlockSpec((1, dv_all), lambda b, c: (0, 0)),
        ],
        out_specs=pl.BlockSpec((1, chunk, dv_all), rows),
        out_shape=jax.ShapeDtypeStruct((bsz, seq, dv_all), BF16),
        scratch_shapes=[
            pltpu.VMEM((heads // 2, dv + BF16_ROWS, 2 * dqk), F32),
            pltpu.VMEM((heads, 1, LANES), F32),
            pltpu.VMEM((2 * heads, chunk), F32),
            pltpu.VMEM((chunk, LANES), F32),
        ],
        compiler_params=_params("parallel", "arbitrary"),
        name="mlstm_scan",
    )(qt, k, vt, opre, gates, gates, norm_h.reshape(1, dv_all))


def kernel(x, norm_mix_pre, norm_mix_post, norm_ffn_pre, norm_ffn_post, w_up, w_down,
           attn_w_qkv, attn_w_o, mlstm_w_in, mlstm_b_gates, mlstm_norm_h, mlstm_w_out):
    bsz, seq, d = x.shape
    m = bsz * seq
    heads = ATT_HEADS
    mh = MLSTM_HEADS
    dv = d // mh
    dqk = dv // 2

    q, k, km, vt = _qkv_proj(x, norm_mix_pre[0], attn_w_qkv[0], heads=heads, blk=MOBA_BLOCK,
                             tm=4 * MOBA_BLOCK)
    slopes = jnp.exp2(-8.0 * jnp.arange(1, heads + 1, dtype=F32) / heads)
    att = _moba_attention(q, k, vt, km, slopes, heads=heads, blk=MOBA_BLOCK, top=MOBA_TOPK, cb=2)
    h = _proj_mlp(att.reshape(m, d), x.reshape(m, d), attn_w_o[0], norm_mix_post[0],
                  norm_ffn_pre[0], w_up, w_down, norm_ffn_post[0], layer=0, tm=512, tf=1024, wc=512)

    nq = mh * dqk
    w_in = mlstm_w_in[0, :, :2 * nq + 2 * d].astype(BF16)
    wg = jnp.pad(mlstm_w_in[0, :, 2 * nq + 2 * d:], ((0, 0), (0, LANES - 2 * mh))).astype(BF16)
    bg = jnp.pad(mlstm_b_gates[0], (0, LANES - 2 * mh)).reshape(1, LANES)
    qt, kmm, vmt, opre, gates = _mlstm_in_proj(h, norm_mix_pre[1], w_in, wg, bg,
                                               bsz=bsz, heads=mh, dqk=dqk, tm=1024)
    y = _mlstm_scan(qt, kmm.reshape(bsz, seq, nq), vmt, opre.reshape(bsz, seq, d),
                    gates.reshape(bsz, seq, LANES), mlstm_norm_h[0], heads=mh, chunk=256)
    h = _proj_mlp(y.reshape(m, d), h, mlstm_w_out[0], norm_mix_post[1],
                  norm_ffn_pre[1], w_up, w_down, norm_ffn_post[1], layer=1, tm=512, tf=1024, wc=512)
    return h.reshape(bsz, seq, d)
```

```python
import functools

import jax
import jax.numpy as jnp
from jax import lax
from jax.experimental import pallas as pl
from jax.experimental.pallas import tpu as pltpu

F32 = jnp.float32
BF16 = jnp.bfloat16

RMS_EPS = 1e-6
ATT_HEADS = 8
MOBA_BLOCK = 256
MOBA_TOPK = 3
MLSTM_HEADS = 8
GATE_SOFTCAP = 15.0

LANES = 128
VMEM_LIMIT = 56 * 1024 * 1024
NEG_BIG = -1e30
POS_BIG = 1e30
LOG2E = 1.4426950408889634
BF16_ROWS = 16
MOBA_DEPTH = 8
MOBA_ROUNDS = 2

_NT = (((1,), (1,)), ((), ()))


def _params(*sem):
    return pltpu.CompilerParams(dimension_semantics=sem, vmem_limit_bytes=VMEM_LIMIT)


def _rms_scale(x, gain):
    ms = jnp.mean(x * x, axis=-1, keepdims=True)
    return x * lax.rsqrt(ms + RMS_EPS) * gain


def _qkv_kernel(x_ref, g_ref, w_ref, q_ref, k_ref, km_ref, vt_ref, wq_ref, wk_ref, wvt_ref,
                *, scale, blk, heads):
    d = x_ref.shape[1]

    @pl.when(pl.program_id(0) == 0)
    def _():
        wq_ref[...] = w_ref[:, :d].astype(BF16)
        wk_ref[...] = w_ref[:, d:2 * d].astype(BF16)
        wvt_ref[...] = w_ref[:, 2 * d:].T.astype(BF16)

    xn = _rms_scale(x_ref[...], g_ref[...]).astype(BF16)
    tm = xn.shape[0]
    dh = d // heads
    q = jnp.dot(xn, wq_ref[...], preferred_element_type=F32)
    q_ref[...] = (q * scale).astype(BF16)
    k = jnp.dot(xn, wk_ref[...], preferred_element_type=F32)
    k_ref[...] = k.astype(BF16)
    for c in range(tm // blk):
        km_ref[c] = jnp.mean(k[c * blk:(c + 1) * blk], axis=0, keepdims=True)
    vt = lax.dot_general(wvt_ref[...], xn, _NT, preferred_element_type=F32)
    ones_row = (lax.broadcasted_iota(jnp.int32, (BF16_ROWS, blk), 0) == 0).astype(BF16)
    for h in range(heads):
        for c in range(tm // blk):
            vt_ref[0, h, c, :dh, :] = vt[h * dh:(h + 1) * dh, c * blk:(c + 1) * blk].astype(BF16)
            vt_ref[0, h, c, dh:, :] = ones_row


def _qkv_proj(x, gain, w_qkv, *, heads, blk, tm):
    bsz, seq, d = x.shape
    dh = d // heads
    nblk = seq // blk
    m = bsz * seq
    cpt = tm // blk
    tiles_per_b = seq // tm
    kern = functools.partial(_qkv_kernel, scale=dh ** -0.5 * LOG2E, blk=blk, heads=heads)
    const = lambda i: (0, 0)
    q, k, km, vt = pl.pallas_call(
        kern,
        grid=(m // tm,),
        in_specs=[
            pl.BlockSpec((tm, d), lambda i: (i, 0)),
            pl.BlockSpec((1, d), const),
            pl.BlockSpec((d, 3 * d), const, pipeline_mode=pl.Buffered(1)),
        ],
        out_specs=[
            pl.BlockSpec((tm, d), lambda i: (i, 0)),
            pl.BlockSpec((tm, d), lambda i: (i, 0)),
            pl.BlockSpec((cpt, 1, d), lambda i: (i, 0, 0)),
            pl.BlockSpec((1, heads, cpt, dh + BF16_ROWS, blk),
                         lambda i: (i // tiles_per_b, 0, i % tiles_per_b, 0, 0)),
        ],
        out_shape=[
            jax.ShapeDtypeStruct((m, d), BF16),
            jax.ShapeDtypeStruct((m, d), BF16),
            jax.ShapeDtypeStruct((m // blk, 1, d), F32),
            jax.ShapeDtypeStruct((bsz, heads, nblk, dh + BF16_ROWS, blk), BF16),
        ],
        scratch_shapes=[pltpu.VMEM((d, d), BF16)] * 3,
        compiler_params=_params("arbitrary"),
        name="qkv_proj",
    )(x.reshape(m, d), gain.reshape(1, d), w_qkv)
    return (q.reshape(bsz, seq, d), k.reshape(bsz, seq, d),
            km.reshape(bsz, nblk, d), vt)


def _moba_items(nblk, cb):
    tiles, chunks = [], []
    for i in range(nblk):
        for c in range(-(-(i + 1) // cb)):
            tiles.append(i)
            chunks.append(c)
    while len(tiles) % (MOBA_DEPTH * MOBA_ROUNDS):
        assert nblk >= 2 * cb
        tiles.append(0)
        chunks.append(1)
    n_items = len(tiles)
    tiles += [0] * (MOBA_DEPTH - 1)
    chunks += [0] * (MOBA_DEPTH - 1)
    return n_items, tiles, chunks


def _moba_kernel(slope_ref, tile_ref, chunk_ref, q_ref, k_ref, vt_ref, km_ref, o_ref,
                 sel_ref, bias_ref, m_ref, acc_ref, cmax_ref, *s_refs,
                 blk, nblk, top, cb, n_items, qc, dh):
    slope = slope_ref[pl.program_id(1)] * LOG2E

    kr = lax.broadcasted_iota(jnp.int32, (blk, blk), 0)
    qq = lax.broadcasted_iota(jnp.int32, (blk, blk), 1)
    base = kr.astype(F32) * slope
    bias_ref[0] = base
    bias_ref[1] = jnp.where(kr <= qq, base, NEG_BIG)

    m_ref[...] = jnp.full(m_ref.shape, NEG_BIG, F32)
    acc_ref[...] = jnp.zeros(acc_ref.shape, F32)

    km = km_ref[0]
    km_hi = km.astype(BF16)
    km_lo = (km - km_hi.astype(F32)).astype(BF16)
    row = lax.broadcasted_iota(jnp.int32, (nblk, qc), 0)
    rowf = row.astype(F32)
    tiles_per_qc = qc // blk

    def select(u, carry):
        q0 = pl.multiple_of(u * qc, qc)
        qs = q_ref[0, pl.ds(q0, qc), :]
        gate = (lax.dot_general(km_hi, qs, _NT, preferred_element_type=F32)
                + lax.dot_general(km_lo, qs, _NT, preferred_element_type=F32))
        qblk = u * tiles_per_qc + lax.broadcasted_iota(jnp.int32, (nblk, qc), 1) // blk
        past = row < qblk
        g = jnp.where(past, gate, -jnp.inf)
        picked = jnp.zeros((nblk, qc), F32)
        for _ in range(top):
            mx = jnp.max(g, axis=0, keepdims=True)
            first = jnp.min(jnp.where(g == mx, rowf, float(nblk)), axis=0, keepdims=True)
            hit = rowf == first
            picked = jnp.where(hit, 1.0, picked)
            g = jnp.where(hit, -jnp.inf, g)
        selv = jnp.where(((picked > 0.5) & past) | (row == qblk), 1.0, 0.0)
        for t in range(tiles_per_qc):
            sel_ref[u * tiles_per_qc + t] = selv[:, t * blk:(t + 1) * blk]
        return carry

    lax.fori_loop(0, (nblk * blk) // qc, select, 0, unroll=True)

    def scores(item, slot):
        i = tile_ref[item]
        c = chunk_ref[item]
        qi = q_ref[0, pl.ds(pl.multiple_of(i * blk, blk), blk), :]
        for jb in range(cb):
            j = c * cb + jb
            kb = k_ref[0, pl.ds(pl.multiple_of(j * blk, blk), blk), :]
            sb = (lax.dot_general(kb, qi, _NT, preferred_element_type=F32)
                  + bias_ref[jnp.where(j == i, 1, 0)])
            s_refs[slot][jb * blk:(jb + 1) * blk, :] = sb
            cmax_ref[slot, jb] = jnp.max(sb, axis=0, keepdims=True)

    def softmax_pv(item, slot):
        i = tile_ref[item]
        c = chunk_ref[item]
        m_old = m_ref[i]
        sels, cjs = [], []
        m_new = m_old
        for jb in range(cb):
            j = c * cb + jb
            cj = ((j - i) * blk).astype(F32) * slope
            sel = sel_ref[i, pl.ds(j, 1), :] > 0.5
            m_new = jnp.maximum(m_new, jnp.where(sel, cmax_ref[slot, jb] + cj, NEG_BIG))
            sels.append(sel)
            cjs.append(cj)
        alpha = jnp.exp2(m_old - m_new)
        pv = None
        for jb in range(cb):
            shift = jnp.where(sels[jb], m_new - cjs[jb], POS_BIG)
            p = jnp.exp2(s_refs[slot][jb * blk:(jb + 1) * blk, :] - shift)
            d = jnp.dot(vt_ref[0, 0, c * cb + jb], p.astype(BF16), preferred_element_type=F32)
            pv = d if pv is None else pv + d
        acc_ref[i] = alpha * acc_ref[i] + pv
        m_ref[i] = m_new

    for slot in range(MOBA_DEPTH - 1):
        scores(slot, slot)

    def group(kk, carry):
        for u in range(MOBA_DEPTH * MOBA_ROUNDS):
            t = MOBA_DEPTH * MOBA_ROUNDS * kk + u
            scores(t + MOBA_DEPTH - 1, (u + MOBA_DEPTH - 1) % MOBA_DEPTH)
            softmax_pv(t, u % MOBA_DEPTH)
        return carry

    lax.fori_loop(0, n_items // (MOBA_DEPTH * MOBA_ROUNDS), group, 0)

    def finish(i, carry):
        acc = acc_ref[i]
        o = acc[:dh] * (1.0 / acc[dh:dh + 1])
        o_ref[0, pl.ds(pl.multiple_of(i * blk, blk), blk), :] = o.T.astype(BF16)
        return carry

    lax.fori_loop(0, nblk, finish, 0, unroll=True)


def _moba_attention(q, k, vt, km, slopes, *, heads, blk, top, cb):
    bsz, seq, d = q.shape
    dh = d // heads
    nblk = seq // blk
    assert nblk % cb == 0
    qc = min(seq, 4 * blk)
    n_items, tiles, chunks = _moba_items(nblk, cb)
    kern = functools.partial(_moba_kernel, blk=blk, nblk=nblk, top=min(top, nblk), cb=cb,
                             n_items=n_items, qc=qc, dh=dh)
    smem = pl.BlockSpec(memory_space=pltpu.SMEM)
    return pl.pallas_call(
        kern,
        grid=(bsz, heads),
        in_specs=[
            smem, smem, smem,
            pl.BlockSpec((1, seq, dh), lambda b, h: (b, 0, h)),
            pl.BlockSpec((1, seq, dh), lambda b, h: (b, 0, h)),
            pl.BlockSpec((1, 1, nblk, dh + BF16_ROWS, blk), lambda b, h: (b, h, 0, 0, 0)),
            pl.BlockSpec((1, nblk, dh), lambda b, h: (b, 0, h)),
        ],
        out_specs=pl.BlockSpec((1, seq, dh), lambda b, h: (b, 0, h)),
        out_shape=jax.ShapeDtypeStruct((bsz, seq, d), BF16),
        scratch_shapes=[
            pltpu.VMEM((nblk, nblk, blk), F32),
            pltpu.VMEM((2, blk, blk), F32),
            pltpu.VMEM((nblk, 1, blk), F32),
            pltpu.VMEM((nblk, dh + BF16_ROWS, blk), F32),
            pltpu.VMEM((MOBA_DEPTH, cb, 1, blk), F32),
        ] + [pltpu.VMEM((cb * blk, blk), F32)] * MOBA_DEPTH,
        compiler_params=_params("parallel", "parallel"),
        name="moba_attention",
    )(slopes, jnp.asarray(tiles, jnp.int32), jnp.asarray(chunks, jnp.int32), q, k, vt, km)


def _proj_mlp_kernel(a_ref, h_ref, wo_ref, gm_ref, g1_ref, wu_ref, wd_ref, g2_ref, o_ref,
                     wu_s, wd_s, h1_s, xn_s, acc_s, *, tf, ncast):
    s = pl.program_id(0)
    wc = wu_ref.shape[1]

    def mixer_residual():
        u = jnp.dot(a_ref[...], wo_ref[...].astype(BF16), preferred_element_type=F32)
        return h_ref[...] + _rms_scale(u, gm_ref[...])

    def mlp_chunk(xn, wu, wd):
        t = jnp.maximum(jnp.dot(xn, wu, preferred_element_type=F32), 0.0)
        return jnp.dot((t * t).astype(BF16), wd, preferred_element_type=F32)

    @pl.when(s == 0)
    def _():
        h1 = mixer_residual()
        h1_s[...] = h1
        xn_s[...] = _rms_scale(h1, g1_ref[...]).astype(BF16)
        acc_s[...] = jnp.zeros_like(acc_s)

    @pl.when(s < ncast)
    def _():
        c0 = pl.multiple_of(s * wc, wc)
        wu = wu_ref[...].astype(BF16)
        wd = wd_ref[...].astype(BF16)
        wu_s[:, pl.ds(c0, wc)] = wu
        wd_s[pl.ds(c0, wc), :] = wd
        acc_s[...] += mlp_chunk(xn_s[...], wu, wd)

    @pl.when(s == ncast - 1)
    def _():
        o_ref[...] = h1_s[...] + _rms_scale(acc_s[...], g2_ref[...])

    @pl.when(s >= ncast)
    def _():
        h1 = mixer_residual()
        xn = _rms_scale(h1, g1_ref[...]).astype(BF16)
        ff = wu_s.shape[1]
        acc = None
        for c in range(ff // tf):
            d = mlp_chunk(xn, wu_s[:, c * tf:(c + 1) * tf], wd_s[c * tf:(c + 1) * tf, :])
            acc = d if acc is None else acc + d
        o_ref[...] = h1 + _rms_scale(acc, g2_ref[...])


def _proj_mlp(a, h, w_o, g_mix, g_pre, w_up, w_down, g_post, *, layer, tm, tf, wc):
    m, d = h.shape
    kdim = a.shape[1]
    ff = w_up.shape[2]
    ncast = ff // wc
    row = lambda s: (jnp.maximum(s - (ncast - 1), 0), 0)
    const = lambda s: (0, 0)
    return pl.pallas_call(
        functools.partial(_proj_mlp_kernel, tf=tf, ncast=ncast),
        grid=(ncast - 1 + m // tm,),
        in_specs=[
            pl.BlockSpec((tm, kdim), row),
            pl.BlockSpec((tm, d), row),
            pl.BlockSpec((kdim, d), const, pipeline_mode=pl.Buffered(1)),
            pl.BlockSpec((1, d), const),
            pl.BlockSpec((1, d), const),
            pl.BlockSpec((None, d, wc), lambda s: (layer, 0, jnp.minimum(s, ncast - 1))),
            pl.BlockSpec((None, wc, d), lambda s: (layer, jnp.minimum(s, ncast - 1), 0)),
            pl.BlockSpec((1, d), const),
        ],
        out_specs=pl.BlockSpec((tm, d), row),
        out_shape=jax.ShapeDtypeStruct((m, d), F32),
        scratch_shapes=[
            pltpu.VMEM((d, ff), BF16),
            pltpu.VMEM((ff, d), BF16),
            pltpu.VMEM((tm, d), F32),
            pltpu.VMEM((tm, d), BF16),
            pltpu.VMEM((tm, d), F32),
        ],
        compiler_params=_params("arbitrary"),
        name="proj_mlp",
    )(a, h, w_o, g_mix.reshape(1, d), g_pre.reshape(1, d), w_up, w_down, g_post.reshape(1, d))


def _mlstm_in_kernel(x_ref, g_ref, w_ref, wg_ref, bg_ref,
                     qt_ref, k_ref, vt_ref, op_ref, gates_ref,
                     wqt_ref, wvt_ref, *, heads, kscale):
    nq = wqt_ref.shape[0]
    d = wvt_ref.shape[0]

    @pl.when(pl.program_id(0) == 0)
    def _():
        wqt_ref[...] = w_ref[:, :nq].astype(F32).T.astype(BF16)
        wvt_ref[...] = w_ref[:, 2 * nq:2 * nq + d].astype(F32).T.astype(BF16)

    xn = _rms_scale(x_ref[...], g_ref[...]).astype(BF16)
    qt_ref[0] = lax.dot_general(wqt_ref[...], xn, _NT, preferred_element_type=F32).astype(BF16)
    k_ref[...] = (jnp.dot(xn, w_ref[:, nq:2 * nq], preferred_element_type=F32) * kscale).astype(BF16)
    vt_ref[0] = lax.dot_general(wvt_ref[...], xn, _NT, preferred_element_type=F32).astype(BF16)
    op_ref[...] = jnp.dot(xn, w_ref[:, 2 * nq + d:2 * nq + 2 * d],
                          preferred_element_type=F32).astype(BF16)
    z = jnp.dot(xn, wg_ref[...], preferred_element_type=F32) + bg_ref[...]
    z = GATE_SOFTCAP * jnp.tanh(z * (1.0 / GATE_SOFTCAP))
    log_f = jnp.minimum(z, 0.0) - jnp.log1p(jnp.exp(-jnp.abs(z)))
    lane = lax.broadcasted_iota(jnp.int32, z.shape, 1)
    gates_ref[...] = jnp.where(lane < heads, z, log_f)


def _mlstm_in_proj(x, gain, w_in, wg, bg, *, bsz, heads, dqk, tm):
    m, d = x.shape
    seq = m // bsz
    nq = heads * dqk
    dv_all = d
    tiles_per_b = seq // tm
    kern = functools.partial(_mlstm_in_kernel, heads=heads, kscale=dqk ** -0.5)
    const = lambda i: (0, 0)
    row = lambda i: (i, 0)
    tcol = lambda i: (i // tiles_per_b, 0, i % tiles_per_b)
    return pl.pallas_call(
        kern,
        grid=(m // tm,),
        in_specs=[
            pl.BlockSpec((tm, d), row),
            pl.BlockSpec((1, d), const),
            pl.BlockSpec(w_in.shape, const, pipeline_mode=pl.Buffered(1)),
            pl.BlockSpec((d, LANES), const),
            pl.BlockSpec((1, LANES), const),
        ],
        out_specs=[
            pl.BlockSpec((1, nq, tm), tcol),
            pl.BlockSpec((tm, nq), row),
            pl.BlockSpec((1, dv_all, tm), tcol),
            pl.BlockSpec((tm, dv_all), row),
            pl.BlockSpec((tm, LANES), row),
        ],
        out_shape=[
            jax.ShapeDtypeStruct((bsz, nq, seq), BF16),
            jax.ShapeDtypeStruct((m, nq), BF16),
            jax.ShapeDtypeStruct((bsz, dv_all, seq), BF16),
            jax.ShapeDtypeStruct((m, dv_all), BF16),
            jax.ShapeDtypeStruct((m, LANES), F32),
        ],
        scratch_shapes=[
            pltpu.VMEM((nq, d), BF16),
            pltpu.VMEM((dv_all, d), BF16),
        ],
        compiler_params=_params("arbitrary"),
        name="mlstm_in_proj",
    )(x, gain.reshape(1, d), w_in, wg, bg)


def _mlstm_kernel(qt_ref, k_ref, vt_ref, op_ref, g_ref, gnext_ref, nh_ref, y_ref,
                  c_ref, m_ref, rows_ref, ucol_ref, *, heads, dqk, dv, chunk):
    L = chunk
    ss = lax.broadcasted_iota(jnp.int32, (L, L), 0)
    tt = lax.broadcasted_iota(jnp.int32, (L, L), 1)
    causal = ss <= tt

    def gate_terms(gates):
        gates_t = gates.T
        i_rows = gates_t[0:heads]
        f_rows = gates_t[heads:2 * heads]
        tri = causal.astype(BF16)
        f_hi = f_rows.astype(BF16)
        f_r1 = f_rows - f_hi.astype(F32)
        f_mid = f_r1.astype(BF16)
        f_lo = (f_r1 - f_mid.astype(F32)).astype(BF16)
        b_rows = (jnp.dot(f_hi, tri, preferred_element_type=F32)
                  + jnp.dot(f_mid, tri, preferred_element_type=F32)
                  + jnp.dot(f_lo, tri, preferred_element_type=F32))
        rows_ref[0:heads] = i_rows
        rows_ref[heads:2 * heads] = b_rows
        ucol_ref[...] = jnp.concatenate(
            [i_rows - b_rows, jnp.zeros((LANES - heads, L), F32)], axis=0).T

    @pl.when(pl.program_id(1) == 0)
    def _():
        c_ref[...] = jnp.zeros_like(c_ref)
        m_ref[...] = jnp.zeros_like(m_ref)
        gate_terms(g_ref[0])

    ones_rows = (lax.broadcasted_iota(jnp.int32, (BF16_ROWS, L), 0) == 0).astype(BF16)
    sub = lax.broadcasted_iota(jnp.int32, (2 * dqk, 1), 0)
    lane = lax.broadcasted_iota(jnp.int32, (1, 2 * dqk), 1)

    first = lane < dqk
    kgs, cts, s_qk, inter = [], [], [], []
    for h in range(heads):
        p, half = divmod(h, 2)
        qt_pair = qt_ref[0, p * 2 * dqk:(p + 1) * 2 * dqk, :]
        kg = k_ref[0, :, p * 2 * dqk:(p + 1) * 2 * dqk]
        in_head = (sub >= half * dqk) & (sub < (half + 1) * dqk)
        qt_h = jnp.where(in_head, qt_pair, jnp.zeros_like(qt_pair))
        ct = c_ref[p]
        kgs.append(kg)
        cts.append(ct)
        s_qk.append(jnp.dot(kg, qt_h, preferred_element_type=F32))
        inter.append(jnp.dot(ct.astype(BF16), qt_h, preferred_element_type=F32))

    s_ts, m_ts, w_inters, vexts = [], [], [], []
    for h in range(heads):
        b_row = rows_ref[heads + h:heads + h + 1, :]
        m_prev = m_ref[h][:, 0:1]
        log_d = jnp.where(causal, ucol_ref[:, h:h + 1] + b_row, -jnp.inf)
        log_inter = b_row + m_prev
        m_t = jnp.maximum(log_inter, jnp.max(log_d, axis=0, keepdims=True))
        s_ts.append((s_qk[h] * jnp.exp(log_d - m_t)).astype(BF16))
        w_inters.append(jnp.exp(log_inter - m_t))
        m_ts.append(m_t)
        vexts.append(jnp.concatenate([vt_ref[0, h * dv:(h + 1) * dv, :], ones_rows], axis=0))

    for h in range(heads):
        num = (w_inters[h] * inter[h]
               + jnp.dot(vexts[h], s_ts[h], preferred_element_type=F32))
        den = num[dv:dv + 1]
        hout = num[:dv] * (1.0 / jnp.maximum(jnp.abs(den), jnp.exp(-m_ts[h])))
        hn = hout * lax.rsqrt(jnp.mean(hout * hout, axis=0, keepdims=True) + RMS_EPS)
        og = jax.nn.sigmoid(op_ref[0, :, h * dv:(h + 1) * dv].astype(F32))
        y_ref[0, :, h * dv:(h + 1) * dv] = (
            og * (hn.T * nh_ref[:, h * dv:(h + 1) * dv])).astype(BF16)

    upds, decays = [], []
    for h in range(heads):
        b_row = rows_ref[heads + h:heads + h + 1, :]
        i_row = rows_ref[h:h + 1, :]
        m_prev = m_ref[h][:, 0:1]
        b_end = b_row[:, L - 1:L]
        log_w = b_end - b_row + i_row
        m_new = jnp.maximum(b_end + m_prev, jnp.max(log_w, axis=1, keepdims=True))
        decays.append(jnp.exp(b_end + m_prev - m_new))
        vw = (vexts[h].astype(F32) * jnp.exp(log_w - m_new)).astype(BF16)
        upds.append(jnp.dot(vw, kgs[h], preferred_element_type=F32))
        m_ref[h] = jnp.broadcast_to(m_new, (1, LANES))
    for p in range(heads // 2):
        c_ref[p] = (jnp.where(first, decays[2 * p], decays[2 * p + 1]) * cts[2 * p]
                    + jnp.where(first, upds[2 * p], upds[2 * p + 1]))

    gate_terms(gnext_ref[0])


def _mlstm_scan(qt, k, vt, opre, gates, norm_h, *, heads, chunk):
    bsz, dv_all, seq = vt.shape
    dv = dv_all // heads
    dqk = qt.shape[1] // heads
    kern = functools.partial(_mlstm_kernel, heads=heads, dqk=dqk, dv=dv, chunk=chunk)
    nchunk = seq // chunk
    rows = lambda b, c: (b, c, 0)
    cols = lambda b, c: (b, 0, c)
    return pl.pallas_call(
        kern,
        grid=(bsz, seq // chunk),
        in_specs=[
            pl.BlockSpec((1, heads * dqk, chunk), cols),
            pl.BlockSpec((1, chunk, heads * dqk), rows),
            pl.BlockSpec((1, dv_all, chunk), cols),
            pl.BlockSpec((1, chunk, dv_all), rows),
            pl.BlockSpec((1, chunk, LANES), rows),
            pl.BlockSpec((1, chunk, LANES), lambda b, c: (b, jnp.minimum(c + 1, nchunk - 1), 0)),
            pl.BlockSpec((1, dv_all), lambda b, c: (0, 0)),
        ],
        out_specs=pl.BlockSpec((1, chunk, dv_all), rows),
        out_shape=jax.ShapeDtypeStruct((bsz, seq, dv_all), BF16),
        scratch_shapes=[
            pltpu.VMEM((heads // 2, dv + BF16_ROWS, 2 * dqk), F32),
            pltpu.VMEM((heads, 1, LANES), F32),
            pltpu.VMEM((2 * heads, chunk), F32),
            pltpu.VMEM((chunk, LANES), F32),
        ],
        compiler_params=_params("parallel", "arbitrary"),
        name="mlstm_scan",
    )(qt, k, vt, opre, gates, gates, norm_h.reshape(1, dv_all))


def kernel(x, norm_mix_pre, norm_mix_post, norm_ffn_pre, norm_ffn_post, w_up, w_down,
           attn_w_qkv, attn_w_o, mlstm_w_in, mlstm_b_gates, mlstm_norm_h, mlstm_w_out):
    bsz, seq, d = x.shape
    m = bsz * seq
    heads = ATT_HEADS
    mh = MLSTM_HEADS
    dv = d // mh
    dqk = dv // 2

    q, k, km, vt = _qkv_proj(x, norm_mix_pre[0], attn_w_qkv[0], heads=heads, blk=MOBA_BLOCK,
                             tm=4 * MOBA_BLOCK)
    slopes = jnp.exp2(-8.0 * jnp.arange(1, heads + 1, dtype=F32) / heads)
    att = _moba_attention(q, k, vt, km, slopes, heads=heads, blk=MOBA_BLOCK, top=MOBA_TOPK, cb=2)
    h = _proj_mlp(att.reshape(m, d), x.reshape(m, d), attn_w_o[0], norm_mix_post[0],
                  norm_ffn_pre[0], w_up, w_down, norm_ffn_post[0], layer=0, tm=512, tf=1024, wc=512)

    nq = mh * dqk
    w_in = mlstm_w_in[0, :, :2 * nq + 2 * d].astype(BF16)
    wg = jnp.pad(mlstm_w_in[0, :, 2 * nq + 2 * d:], ((0, 0), (0, LANES - 2 * mh))).astype(BF16)
    bg = jnp.pad(mlstm_b_gates[0], (0, LANES - 2 * mh)).reshape(1, LANES)
    qt, kmm, vmt, opre, gates = _mlstm_in_proj(h, norm_mix_pre[1], w_in, wg, bg,
                                               bsz=bsz, heads=mh, dqk=dqk, tm=1024)
    y = _mlstm_scan(qt, kmm.reshape(bsz, seq, nq), vmt, opre.reshape(bsz, seq, d),
                    gates.reshape(bsz, seq, LANES), mlstm_norm_h[0], heads=mh, chunk=256)
    h = _proj_mlp(y.reshape(m, d), h, mlstm_w_out[0], norm_mix_post[1],
                  norm_ffn_pre[1], w_up, w_down, norm_ffn_post[1], layer=1, tm=512, tf=1024, wc=512)
    return h.reshape(bsz, seq, d)
```

```python
import functools

import jax
import jax.numpy as jnp
from jax import lax
from jax.experimental import pallas as pl
from jax.experimental.pallas import tpu as pltpu

F32 = jnp.float32
BF16 = jnp.bfloat16

RMS_EPS = 1e-6
ATT_HEADS = 8
MOBA_BLOCK = 256
MOBA_TOPK = 3
MLSTM_HEADS = 8
GATE_SOFTCAP = 15.0

LANES = 128
VMEM_LIMIT = 56 * 1024 * 1024
NEG_BIG = -1e30
POS_BIG = 1e30
LOG2E = 1.4426950408889634
BF16_ROWS = 16
MOBA_DEPTH = 17
MOBA_ROUNDS = 4

_NT = (((1,), (1,)), ((), ()))


def _params(*sem):
    return pltpu.CompilerParams(dimension_semantics=sem, vmem_limit_bytes=VMEM_LIMIT)


def _rms_scale(x, gain):
    ms = jnp.mean(x * x, axis=-1, keepdims=True)
    return x * lax.rsqrt(ms + RMS_EPS) * gain


def _qkv_kernel(x_ref, g_ref, w_ref, q_ref, k_ref, km_ref, vt_ref, wq_ref, wk_ref, wvt_ref,
                *, scale, blk, heads):
    d = x_ref.shape[1]

    @pl.when(pl.program_id(0) == 0)
    def _():
        wq_ref[...] = w_ref[:, :d].astype(BF16)
        wk_ref[...] = w_ref[:, d:2 * d].astype(BF16)
        wvt_ref[...] = w_ref[:, 2 * d:].T.astype(BF16)

    xn = _rms_scale(x_ref[...], g_ref[...]).astype(BF16)
    tm = xn.shape[0]
    dh = d // heads
    q = jnp.dot(xn, wq_ref[...], preferred_element_type=F32)
    q_ref[...] = (q * scale).astype(BF16)
    k = jnp.dot(xn, wk_ref[...], preferred_element_type=F32)
    k_ref[...] = k.astype(BF16)
    for c in range(tm // blk):
        km_ref[c] = jnp.mean(k[c * blk:(c + 1) * blk], axis=0, keepdims=True)
    vt = lax.dot_general(wvt_ref[...], xn, _NT, preferred_element_type=F32)
    ones_row = (lax.broadcasted_iota(jnp.int32, (BF16_ROWS, blk), 0) == 0).astype(BF16)
    for h in range(heads):
        for c in range(tm // blk):
            vt_ref[0, h, c, :dh, :] = vt[h * dh:(h + 1) * dh, c * blk:(c + 1) * blk].astype(BF16)
            vt_ref[0, h, c, dh:, :] = ones_row


def _qkv_proj(x, gain, w_qkv, *, heads, blk, tm):
    bsz, seq, d = x.shape
    dh = d // heads
    nblk = seq // blk
    m = bsz * seq
    cpt = tm // blk
    tiles_per_b = seq // tm
    kern = functools.partial(_qkv_kernel, scale=dh ** -0.5 * LOG2E, blk=blk, heads=heads)
    const = lambda i: (0, 0)
    q, k, km, vt = pl.pallas_call(
        kern,
        grid=(m // tm,),
        in_specs=[
            pl.BlockSpec((tm, d), lambda i: (i, 0)),
            pl.BlockSpec((1, d), const),
            pl.BlockSpec((d, 3 * d), const, pipeline_mode=pl.Buffered(1)),
        ],
        out_specs=[
            pl.BlockSpec((tm, d), lambda i: (i, 0)),
            pl.BlockSpec((tm, d), lambda i: (i, 0)),
            pl.BlockSpec((cpt, 1, d), lambda i: (i, 0, 0)),
            pl.BlockSpec((1, heads, cpt, dh + BF16_ROWS, blk),
                         lambda i: (i // tiles_per_b, 0, i % tiles_per_b, 0, 0)),
        ],
        out_shape=[
            jax.ShapeDtypeStruct((m, d), BF16),
            jax.ShapeDtypeStruct((m, d), BF16),
            jax.ShapeDtypeStruct((m // blk, 1, d), F32),
            jax.ShapeDtypeStruct((bsz, heads, nblk, dh + BF16_ROWS, blk), BF16),
        ],
        scratch_shapes=[pltpu.VMEM((d, d), BF16)] * 3,
        compiler_params=_params("arbitrary"),
        name="qkv_proj",
    )(x.reshape(m, d), gain.reshape(1, d), w_qkv)
    return (q.reshape(bsz, seq, d), k.reshape(bsz, seq, d),
            km.reshape(bsz, nblk, d), vt)


def _moba_items(nblk, cb):
    tiles, chunks = [], []
    for i in range(nblk):
        for c in range(-(-(i + 1) // cb)):
            tiles.append(i)
            chunks.append(c)
    while len(tiles) % (MOBA_DEPTH * MOBA_ROUNDS):
        assert nblk >= 2 * cb
        tiles.append(0)
        chunks.append(1)
    n_items = len(tiles)
    tiles += [0] * (MOBA_DEPTH - 1)
    chunks += [0] * (MOBA_DEPTH - 1)
    return n_items, tiles, chunks


def _moba_kernel(slope_ref, tile_ref, chunk_ref, q_ref, k_ref, vt_ref, km_ref, o_ref,
                 sel_ref, bias_ref, m_ref, acc_ref, cmax_ref, *s_refs,
                 blk, nblk, top, cb, n_items, qc, dh):
    slope = slope_ref[pl.program_id(1)] * LOG2E

    kr = lax.broadcasted_iota(jnp.int32, (blk, blk), 0)
    qq = lax.broadcasted_iota(jnp.int32, (blk, blk), 1)
    base = kr.astype(F32) * slope
    bias_ref[0] = base
    bias_ref[1] = jnp.where(kr <= qq, base, NEG_BIG)

    m_ref[...] = jnp.full(m_ref.shape, NEG_BIG, F32)
    acc_ref[...] = jnp.zeros(acc_ref.shape, F32)

    km = km_ref[0]
    km_hi = km.astype(BF16)
    km_lo = (km - km_hi.astype(F32)).astype(BF16)
    row = lax.broadcasted_iota(jnp.int32, (nblk, qc), 0)
    rowf = row.astype(F32)
    tiles_per_qc = qc // blk

    def select(u, carry):
        q0 = pl.multiple_of(u * qc, qc)
        qs = q_ref[0, pl.ds(q0, qc), :]
        gate = (lax.dot_general(km_hi, qs, _NT, preferred_element_type=F32)
                + lax.dot_general(km_lo, qs, _NT, preferred_element_type=F32))
        qblk = u * tiles_per_qc + lax.broadcasted_iota(jnp.int32, (nblk, qc), 1) // blk
        past = row < qblk
        g = jnp.where(past, gate, -jnp.inf)
        picked = jnp.zeros((nblk, qc), F32)
        for _ in range(top):
            mx = jnp.max(g, axis=0, keepdims=True)
            first = jnp.min(jnp.where(g == mx, rowf, float(nblk)), axis=0, keepdims=True)
            hit = rowf == first
            picked = jnp.where(hit, 1.0, picked)
            g = jnp.where(hit, -jnp.inf, g)
        selv = jnp.where(((picked > 0.5) & past) | (row == qblk), 1.0, 0.0)
        for t in range(tiles_per_qc):
            sel_ref[u * tiles_per_qc + t] = selv[:, t * blk:(t + 1) * blk]
        return carry

    lax.fori_loop(0, (nblk * blk) // qc, select, 0, unroll=True)

    def scores(item, slot):
        i = tile_ref[item]
        c = chunk_ref[item]
        qi = q_ref[0, pl.ds(pl.multiple_of(i * blk, blk), blk), :]
        for jb in range(cb):
            j = c * cb + jb
            kb = k_ref[0, pl.ds(pl.multiple_of(j * blk, blk), blk), :]
            sb = (lax.dot_general(kb, qi, _NT, preferred_element_type=F32)
                  + bias_ref[jnp.where(j == i, 1, 0)])
            s_refs[slot][jb * blk:(jb + 1) * blk, :] = sb
            cmax_ref[slot, jb] = jnp.max(sb, axis=0, keepdims=True)

    def softmax_pv(item, slot):
        i = tile_ref[item]
        c = chunk_ref[item]
        m_old = m_ref[i]
        sels, cjs = [], []
        m_new = m_old
        for jb in range(cb):
            j = c * cb + jb
            cj = ((j - i) * blk).astype(F32) * slope
            sel = sel_ref[i, pl.ds(j, 1), :] > 0.5
            m_new = jnp.maximum(m_new, jnp.where(sel, cmax_ref[slot, jb] + cj, NEG_BIG))
            sels.append(sel)
            cjs.append(cj)
        alpha = jnp.exp2(m_old - m_new)
        pv = None
        for jb in range(cb):
            shift = jnp.where(sels[jb], m_new - cjs[jb], POS_BIG)
            p = jnp.exp2(s_refs[slot][jb * blk:(jb + 1) * blk, :] - shift)
            d = jnp.dot(vt_ref[0, 0, c * cb + jb], p.astype(BF16), preferred_element_type=F32)
            pv = d if pv is None else pv + d
        acc_ref[i] = alpha * acc_ref[i] + pv
        m_ref[i] = m_new

    for slot in range(MOBA_DEPTH - 1):
        scores(slot, slot)

    def group(kk, carry):
        for u in range(MOBA_DEPTH * MOBA_ROUNDS):
            t = MOBA_DEPTH * MOBA_ROUNDS * kk + u
            scores(t + MOBA_DEPTH - 1, (u + MOBA_DEPTH - 1) % MOBA_DEPTH)
            softmax_pv(t, u % MOBA_DEPTH)
        return carry

    lax.fori_loop(0, n_items // (MOBA_DEPTH * MOBA_ROUNDS), group, 0)

    def finish(i, carry):
        acc = acc_ref[i]
        o = acc[:dh] * (1.0 / acc[dh:dh + 1])
        o_ref[0, pl.ds(pl.multiple_of(i * blk, blk), blk), :] = o.T.astype(BF16)
        return carry

    lax.fori_loop(0, nblk, finish, 0, unroll=True)


def _moba_attention(q, k, vt, km, slopes, *, heads, blk, top, cb):
    bsz, seq, d = q.shape
    dh = d // heads
    nblk = seq // blk
    assert nblk % cb == 0
    qc = min(seq, 4 * blk)
    n_items, tiles, chunks = _moba_items(nblk, cb)
    kern = functools.partial(_moba_kernel, blk=blk, nblk=nblk, top=min(top, nblk), cb=cb,
                             n_items=n_items, qc=qc, dh=dh)
    smem = pl.BlockSpec(memory_space=pltpu.SMEM)
    return pl.pallas_call(
        kern,
        grid=(bsz, heads),
        in_specs=[
            smem, smem, smem,
            pl.BlockSpec((1, seq, dh), lambda b, h: (b, 0, h)),
            pl.BlockSpec((1, seq, dh), lambda b, h: (b, 0, h)),
            pl.BlockSpec((1, 1, nblk, dh + BF16_ROWS, blk), lambda b, h: (b, h, 0, 0, 0)),
            pl.BlockSpec((1, nblk, dh), lambda b, h: (b, 0, h)),
        ],
        out_specs=pl.BlockSpec((1, seq, dh), lambda b, h: (b, 0, h)),
        out_shape=jax.ShapeDtypeStruct((bsz, seq, d), BF16),
        scratch_shapes=[
            pltpu.VMEM((nblk, nblk, blk), F32),
            pltpu.VMEM((2, blk, blk), F32),
            pltpu.VMEM((nblk, 1, blk), F32),
            pltpu.VMEM((nblk, dh + BF16_ROWS, blk), F32),
            pltpu.VMEM((MOBA_DEPTH, cb, 1, blk), F32),
        ] + [pltpu.VMEM((cb * blk, blk), F32)] * MOBA_DEPTH,
        compiler_params=_params("parallel", "parallel"),
        name="moba_attention",
    )(slopes, jnp.asarray(tiles, jnp.int32), jnp.asarray(chunks, jnp.int32), q, k, vt, km)


def _proj_mlp_kernel(a_ref, h_ref, wo_ref, gm_ref, g1_ref, wu_ref, wd_ref, g2_ref, o_ref,
                     wu_s, wd_s, h1_s, xn_s, acc_s, *, tf, ncast):
    s = pl.program_id(0)
    wc = wu_ref.shape[1]

    def mixer_residual():
        u = jnp.dot(a_ref[...], wo_ref[...].astype(BF16), preferred_element_type=F32)
        return h_ref[...] + _rms_scale(u, gm_ref[...])

    def mlp_chunk(xn, wu, wd):
        t = jnp.maximum(jnp.dot(xn, wu, preferred_element_type=F32), 0.0)
        return jnp.dot((t * t).astype(BF16), wd, preferred_element_type=F32)

    @pl.when(s == 0)
    def _():
        h1 = mixer_residual()
        h1_s[...] = h1
        xn_s[...] = _rms_scale(h1, g1_ref[...]).astype(BF16)
        acc_s[...] = jnp.zeros_like(acc_s)

    @pl.when(s < ncast)
    def _():
        c0 = pl.multiple_of(s * wc, wc)
        wu = wu_ref[...].astype(BF16)
        wd = wd_ref[...].astype(BF16)
        wu_s[:, pl.ds(c0, wc)] = wu
        wd_s[pl.ds(c0, wc), :] = wd
        acc_s[...] += mlp_chunk(xn_s[...], wu, wd)

    @pl.when(s == ncast - 1)
    def _():
        o_ref[...] = h1_s[...] + _rms_scale(acc_s[...], g2_ref[...])

    @pl.when(s >= ncast)
    def _():
        h1 = mixer_residual()
        xn = _rms_scale(h1, g1_ref[...]).astype(BF16)
        ff = wu_s.shape[1]
        acc = None
        for c in range(ff // tf):
            d = mlp_chunk(xn, wu_s[:, c * tf:(c + 1) * tf], wd_s[c * tf:(c + 1) * tf, :])
            acc = d if acc is None else acc + d
        o_ref[...] = h1 + _rms_scale(acc, g2_ref[...])


def _proj_mlp(a, h, w_o, g_mix, g_pre, w_up, w_down, g_post, *, layer, tm, tf, wc):
    m, d = h.shape
    kdim = a.shape[1]
    ff = w_up.shape[2]
    ncast = ff // wc
    row = lambda s: (jnp.maximum(s - (ncast - 1), 0), 0)
    const = lambda s: (0, 0)
    return pl.pallas_call(
        functools.partial(_proj_mlp_kernel, tf=tf, ncast=ncast),
        grid=(ncast - 1 + m // tm,),
        in_specs=[
            pl.BlockSpec((tm, kdim), row),
            pl.BlockSpec((tm, d), row),
            pl.BlockSpec((kdim, d), const, pipeline_mode=pl.Buffered(1)),
            pl.BlockSpec((1, d), const),
            pl.BlockSpec((1, d), const),
            pl.BlockSpec((None, d, wc), lambda s: (layer, 0, jnp.minimum(s, ncast - 1))),
            pl.BlockSpec((None, wc, d), lambda s: (layer, jnp.minimum(s, ncast - 1), 0)),
            pl.BlockSpec((1, d), const),
        ],
        out_specs=pl.BlockSpec((tm, d), row),
        out_shape=jax.ShapeDtypeStruct((m, d), F32),
        scratch_shapes=[
            pltpu.VMEM((d, ff), BF16),
            pltpu.VMEM((ff, d), BF16),
            pltpu.VMEM((tm, d), F32),
            pltpu.VMEM((tm, d), BF16),
            pltpu.VMEM((tm, d), F32),
        ],
        compiler_params=_params("arbitrary"),
        name="proj_mlp",
    )(a, h, w_o, g_mix.reshape(1, d), g_pre.reshape(1, d), w_up, w_down, g_post.reshape(1, d))


def _mlstm_in_kernel(x_ref, g_ref, w_ref, wg_ref, bg_ref,
                     qt_ref, k_ref, vt_ref, op_ref, gates_ref,
                     wqt_ref, wvt_ref, *, heads, kscale):
    nq = wqt_ref.shape[0]
    d = wvt_ref.shape[0]

    @pl.when(pl.program_id(0) == 0)
    def _():
        wqt_ref[...] = w_ref[:, :nq].astype(F32).T.astype(BF16)
        wvt_ref[...] = w_ref[:, 2 * nq:2 * nq + d].astype(F32).T.astype(BF16)

    xn = _rms_scale(x_ref[...], g_ref[...]).astype(BF16)
    qt_ref[0] = lax.dot_general(wqt_ref[...], xn, _NT, preferred_element_type=F32).astype(BF16)
    k_ref[...] = (jnp.dot(xn, w_ref[:, nq:2 * nq], preferred_element_type=F32) * kscale).astype(BF16)
    vt_ref[0] = lax.dot_general(wvt_ref[...], xn, _NT, preferred_element_type=F32).astype(BF16)
    op_ref[...] = jnp.dot(xn, w_ref[:, 2 * nq + d:2 * nq + 2 * d],
                          preferred_element_type=F32).astype(BF16)
    z = jnp.dot(xn, wg_ref[...], preferred_element_type=F32) + bg_ref[...]
    z = GATE_SOFTCAP * jnp.tanh(z * (1.0 / GATE_SOFTCAP))
    log_f = jnp.minimum(z, 0.0) - jnp.log1p(jnp.exp(-jnp.abs(z)))
    lane = lax.broadcasted_iota(jnp.int32, z.shape, 1)
    gates_ref[...] = jnp.where(lane < heads, z, log_f)


def _mlstm_in_proj(x, gain, w_in, wg, bg, *, bsz, heads, dqk, tm):
    m, d = x.shape
    seq = m // bsz
    nq = heads * dqk
    dv_all = d
    tiles_per_b = seq // tm
    kern = functools.partial(_mlstm_in_kernel, heads=heads, kscale=dqk ** -0.5)
    const = lambda i: (0, 0)
    row = lambda i: (i, 0)
    tcol = lambda i: (i // tiles_per_b, 0, i % tiles_per_b)
    return pl.pallas_call(
        kern,
        grid=(m // tm,),
        in_specs=[
            pl.BlockSpec((tm, d), row),
            pl.BlockSpec((1, d), const),
            pl.BlockSpec(w_in.shape, const, pipeline_mode=pl.Buffered(1)),
            pl.BlockSpec((d, LANES), const),
            pl.BlockSpec((1, LANES), const),
        ],
        out_specs=[
            pl.BlockSpec((1, nq, tm), tcol),
            pl.BlockSpec((tm, nq), row),
            pl.BlockSpec((1, dv_all, tm), tcol),
            pl.BlockSpec((tm, dv_all), row),
            pl.BlockSpec((tm, LANES), row),
        ],
        out_shape=[
            jax.ShapeDtypeStruct((bsz, nq, seq), BF16),
            jax.ShapeDtypeStruct((m, nq), BF16),
            jax.ShapeDtypeStruct((bsz, dv_all, seq), BF16),
            jax.ShapeDtypeStruct((m, dv_all), BF16),
            jax.ShapeDtypeStruct((m, LANES), F32),
        ],
        scratch_shapes=[
            pltpu.VMEM((nq, d), BF16),
            pltpu.VMEM((dv_all, d), BF16),
        ],
        compiler_params=_params("arbitrary"),
        name="mlstm_in_proj",
    )(x, gain.reshape(1, d), w_in, wg, bg)


def _mlstm_kernel(qt_ref, k_ref, vt_ref, op_ref, g_ref, gnext_ref, nh_ref, y_ref,
                  c_ref, m_ref, rows_ref, ucol_ref, *, heads, dqk, dv, chunk):
    L = chunk
    ss = lax.broadcasted_iota(jnp.int32, (L, L), 0)
    tt = lax.broadcasted_iota(jnp.int32, (L, L), 1)
    causal = ss <= tt

    def gate_terms(gates):
        gates_t = gates.T
        i_rows = gates_t[0:heads]
        f_rows = gates_t[heads:2 * heads]
        tri = causal.astype(BF16)
        f_hi = f_rows.astype(BF16)
        f_r1 = f_rows - f_hi.astype(F32)
        f_mid = f_r1.astype(BF16)
        f_lo = (f_r1 - f_mid.astype(F32)).astype(BF16)
        b_rows = (jnp.dot(f_hi, tri, preferred_element_type=F32)
                  + jnp.dot(f_mid, tri, preferred_element_type=F32)
                  + jnp.dot(f_lo, tri, preferred_element_type=F32))
        rows_ref[0:heads] = i_rows
        rows_ref[heads:2 * heads] = b_rows
        ucol_ref[...] = jnp.concatenate(
            [i_rows - b_rows, jnp.zeros((LANES - heads, L), F32)], axis=0).T

    @pl.when(pl.program_id(1) == 0)
    def _():
        c_ref[...] = jnp.zeros_like(c_ref)
        m_ref[...] = jnp.zeros_like(m_ref)
        gate_terms(g_ref[0])

    ones_rows = (lax.broadcasted_iota(jnp.int32, (BF16_ROWS, L), 0) == 0).astype(BF16)
    sub = lax.broadcasted_iota(jnp.int32, (2 * dqk, 1), 0)
    lane = lax.broadcasted_iota(jnp.int32, (1, 2 * dqk), 1)

    first = lane < dqk
    kgs, cts, s_qk, inter = [], [], [], []
    for h in range(heads):
        p, half = divmod(h, 2)
        qt_pair = qt_ref[0, p * 2 * dqk:(p + 1) * 2 * dqk, :]
        kg = k_ref[0, :, p * 2 * dqk:(p + 1) * 2 * dqk]
        in_head = (sub >= half * dqk) & (sub < (half + 1) * dqk)
        qt_h = jnp.where(in_head, qt_pair, jnp.zeros_like(qt_pair))
        ct = c_ref[p]
        kgs.append(kg)
        cts.append(ct)
        s_qk.append(jnp.dot(kg, qt_h, preferred_element_type=F32))
        inter.append(jnp.dot(ct.astype(BF16), qt_h, preferred_element_type=F32))

    s_ts, m_ts, w_inters, vexts = [], [], [], []
    for h in range(heads):
        b_row = rows_ref[heads + h:heads + h + 1, :]
        m_prev = m_ref[h][:, 0:1]
        log_d = jnp.where(causal, ucol_ref[:, h:h + 1] + b_row, -jnp.inf)
        log_inter = b_row + m_prev
        m_t = jnp.maximum(log_inter, jnp.max(log_d, axis=0, keepdims=True))
        s_ts.append((s_qk[h] * jnp.exp(log_d - m_t)).astype(BF16))
        w_inters.append(jnp.exp(log_inter - m_t))
        m_ts.append(m_t)
        vexts.append(jnp.concatenate([vt_ref[0, h * dv:(h + 1) * dv, :], ones_rows], axis=0))

    for h in range(heads):
        num = (w_inters[h] * inter[h]
               + jnp.dot(vexts[h], s_ts[h], preferred_element_type=F32))
        den = num[dv:dv + 1]
        hout = num[:dv] * (1.0 / jnp.maximum(jnp.abs(den), jnp.exp(-m_ts[h])))
        hn = hout * lax.rsqrt(jnp.mean(hout * hout, axis=0, keepdims=True) + RMS_EPS)
        og = jax.nn.sigmoid(op_ref[0, :, h * dv:(h + 1) * dv].astype(F32))
        y_ref[0, :, h * dv:(h + 1) * dv] = (
            og * (hn.T * nh_ref[:, h * dv:(h + 1) * dv])).astype(BF16)

    upds, decays = [], []
    for h in range(heads):
        b_row = rows_ref[heads + h:heads + h + 1, :]
        i_row = rows_ref[h:h + 1, :]
        m_prev = m_ref[h][:, 0:1]
        b_end = b_row[:, L - 1:L]
        log_w = b_end - b_row + i_row
        m_new = jnp.maximum(b_end + m_prev, jnp.max(log_w, axis=1, keepdims=True))
        decays.append(jnp.exp(b_end + m_prev - m_new))
        vw = (vexts[h].astype(F32) * jnp.exp(log_w - m_new)).astype(BF16)
        upds.append(jnp.dot(vw, kgs[h], preferred_element_type=F32))
        m_ref[h] = jnp.broadcast_to(m_new, (1, LANES))
    for p in range(heads // 2):
        c_ref[p] = (jnp.where(first, decays[2 * p], decays[2 * p + 1]) * cts[2 * p]
                    + jnp.where(first, upds[2 * p], upds[2 * p + 1]))

    gate_terms(gnext_ref[0])


def _mlstm_scan(qt, k, vt, opre, gates, norm_h, *, heads, chunk):
    bsz, dv_all, seq = vt.shape
    dv = dv_all // heads
    dqk = qt.shape[1] // heads
    kern = functools.partial(_mlstm_kernel, heads=heads, dqk=dqk, dv=dv, chunk=chunk)
    nchunk = seq // chunk
    rows = lambda b, c: (b, c, 0)
    cols = lambda b, c: (b, 0, c)
    return pl.pallas_call(
        kern,
        grid=(bsz, seq // chunk),
        in_specs=[
            pl.BlockSpec((1, heads * dqk, chunk), cols),
            pl.BlockSpec((1, chunk, heads * dqk), rows),
            pl.BlockSpec((1, dv_all, chunk), cols),
            pl.BlockSpec((1, chunk, dv_all), rows),
            pl.BlockSpec((1, chunk, LANES), rows),
            pl.BlockSpec((1, chunk, LANES), lambda b, c: (b, jnp.minimum(c + 1, nchunk - 1), 0)),
            pl.BlockSpec((1, dv_all), lambda b, c: (0, 0)),
        ],
        out_specs=pl.BlockSpec((1, chunk, dv_all), rows),
        out_shape=jax.ShapeDtypeStruct((bsz, seq, dv_all), BF16),
        scratch_shapes=[
            pltpu.VMEM((heads // 2, dv + BF16_ROWS, 2 * dqk), F32),
            pltpu.VMEM((heads, 1, LANES), F32),
            pltpu.VMEM((2 * heads, chunk), F32),
            pltpu.VMEM((chunk, LANES), F32),
        ],
        compiler_params=_params("parallel", "arbitrary"),
        name="mlstm_scan",
    )(qt, k, vt, opre, gates, gates, norm_h.reshape(1, dv_all))


def kernel(x, norm_mix_pre, norm_mix_post, norm_ffn_pre, norm_ffn_post, w_up, w_down,
           attn_w_qkv, attn_w_o, mlstm_w_in, mlstm_b_gates, mlstm_norm_h, mlstm_w_out):
    bsz, seq, d = x.shape
    m = bsz * seq
    heads = ATT_HEADS
    mh = MLSTM_HEADS
    dv = d // mh
    dqk = dv // 2

    q, k, km, vt = _qkv_proj(x, norm_mix_pre[0], attn_w_qkv[0], heads=heads, blk=MOBA_BLOCK,
                             tm=4 * MOBA_BLOCK)
    slopes = jnp.exp2(-8.0 * jnp.arange(1, heads + 1, dtype=F32) / heads)
    att = _moba_attention(q, k, vt, km, slopes, heads=heads, blk=MOBA_BLOCK, top=MOBA_TOPK, cb=2)
    h = _proj_mlp(att.reshape(m, d), x.reshape(m, d), attn_w_o[0], norm_mix_post[0],
                  norm_ffn_pre[0], w_up, w_down, norm_ffn_post[0], layer=0, tm=512, tf=1024, wc=512)

    nq = mh * dqk
    w_in = mlstm_w_in[0, :, :2 * nq + 2 * d].astype(BF16)
    wg = jnp.pad(mlstm_w_in[0, :, 2 * nq + 2 * d:], ((0, 0), (0, LANES - 2 * mh))).astype(BF16)
    bg = jnp.pad(mlstm_b_gates[0], (0, LANES - 2 * mh)).reshape(1, LANES)
    qt, kmm, vmt, opre, gates = _mlstm_in_proj(h, norm_mix_pre[1], w_in, wg, bg,
                                               bsz=bsz, heads=mh, dqk=dqk, tm=1024)
    y = _mlstm_scan(qt, kmm.reshape(bsz, seq, nq), vmt, opre.reshape(bsz, seq, d),
                    gates.reshape(bsz, seq, LANES), mlstm_norm_h[0], heads=mh, chunk=256)
    h = _proj_mlp(y.reshape(m, d), h, mlstm_w_out[0], norm_mix_post[1],
                  norm_ffn_pre[1], w_up, w_down, norm_ffn_post[1], layer=1, tm=512, tf=1024, wc=512)
    return h.reshape(bsz, seq, d)
```

```python
import functools

import jax
import jax.numpy as jnp
from jax import lax
from jax.experimental import pallas as pl
from jax.experimental.pallas import tpu as pltpu

F32 = jnp.float32
BF16 = jnp.bfloat16

RMS_EPS = 1e-6
ATT_HEADS = 8
MOBA_BLOCK = 256
MOBA_TOPK = 3
MLSTM_HEADS = 8
GATE_SOFTCAP = 15.0

LANES = 128
VMEM_LIMIT = 56 * 1024 * 1024
NEG_BIG = -1e30
POS_BIG = 1e30
LOG2E = 1.4426950408889634
BF16_ROWS = 16
MOBA_DEPTH = 17
MOBA_ROUNDS = 2

_NT = (((1,), (1,)), ((), ()))


def _params(*sem):
    return pltpu.CompilerParams(dimension_semantics=sem, vmem_limit_bytes=VMEM_LIMIT)


def _rms_scale(x, gain):
    ms = jnp.mean(x * x, axis=-1, keepdims=True)
    return x * lax.rsqrt(ms + RMS_EPS) * gain


def _qkv_kernel(x_ref, g_ref, w_ref, q_ref, k_ref, km_ref, vt_ref, wq_ref, wk_ref, wvt_ref,
                *, scale, blk, heads):
    d = x_ref.shape[1]

    @pl.when(pl.program_id(0) == 0)
    def _():
        wq_ref[...] = w_ref[:, :d].astype(BF16)
        wk_ref[...] = w_ref[:, d:2 * d].astype(BF16)
        wvt_ref[...] = w_ref[:, 2 * d:].T.astype(BF16)

    xn = _rms_scale(x_ref[...], g_ref[...]).astype(BF16)
    tm = xn.shape[0]
    dh = d // heads
    q = jnp.dot(xn, wq_ref[...], preferred_element_type=F32)
    q_ref[...] = (q * scale).astype(BF16)
    k = jnp.dot(xn, wk_ref[...], preferred_element_type=F32)
    k_ref[...] = k.astype(BF16)
    for c in range(tm // blk):
        km_ref[c] = jnp.mean(k[c * blk:(c + 1) * blk], axis=0, keepdims=True)
    vt = lax.dot_general(wvt_ref[...], xn, _NT, preferred_element_type=F32)
    ones_row = (lax.broadcasted_iota(jnp.int32, (BF16_ROWS, blk), 0) == 0).astype(BF16)
    for h in range(heads):
        for c in range(tm // blk):
            vt_ref[0, h, c, :dh, :] = vt[h * dh:(h + 1) * dh, c * blk:(c + 1) * blk].astype(BF16)
            vt_ref[0, h, c, dh:, :] = ones_row


def _qkv_proj(x, gain, w_qkv, *, heads, blk, tm):
    bsz, seq, d = x.shape
    dh = d // heads
    nblk = seq // blk
    m = bsz * seq
    cpt = tm // blk
    tiles_per_b = seq // tm
    kern = functools.partial(_qkv_kernel, scale=dh ** -0.5 * LOG2E, blk=blk, heads=heads)
    const = lambda i: (0, 0)
    q, k, km, vt = pl.pallas_call(
        kern,
        grid=(m // tm,),
        in_specs=[
            pl.BlockSpec((tm, d), lambda i: (i, 0)),
            pl.BlockSpec((1, d), const),
            pl.BlockSpec((d, 3 * d), const, pipeline_mode=pl.Buffered(1)),
        ],
        out_specs=[
            pl.BlockSpec((tm, d), lambda i: (i, 0)),
            pl.BlockSpec((tm, d), lambda i: (i, 0)),
            pl.BlockSpec((cpt, 1, d), lambda i: (i, 0, 0)),
            pl.BlockSpec((1, heads, cpt, dh + BF16_ROWS, blk),
                         lambda i: (i // tiles_per_b, 0, i % tiles_per_b, 0, 0)),
        ],
        out_shape=[
            jax.ShapeDtypeStruct((m, d), BF16),
            jax.ShapeDtypeStruct((m, d), BF16),
            jax.ShapeDtypeStruct((m // blk, 1, d), F32),
            jax.ShapeDtypeStruct((bsz, heads, nblk, dh + BF16_ROWS, blk), BF16),
        ],
        scratch_shapes=[pltpu.VMEM((d, d), BF16)] * 3,
        compiler_params=_params("arbitrary"),
        name="qkv_proj",
    )(x.reshape(m, d), gain.reshape(1, d), w_qkv)
    return (q.reshape(bsz, seq, d), k.reshape(bsz, seq, d),
            km.reshape(bsz, nblk, d), vt)


def _moba_items(nblk, cb):
    tiles, chunks = [], []
    for i in range(nblk):
        for c in range(-(-(i + 1) // cb)):
            tiles.append(i)
            chunks.append(c)
    while len(tiles) % (MOBA_DEPTH * MOBA_ROUNDS):
        assert nblk >= 2 * cb
        tiles.append(0)
        chunks.append(1)
    n_items = len(tiles)
    tiles += [0] * (MOBA_DEPTH - 1)
    chunks += [0] * (MOBA_DEPTH - 1)
    return n_items, tiles, chunks


def _moba_kernel(slope_ref, tile_ref, chunk_ref, q_ref, k_ref, vt_ref, km_ref, o_ref,
                 sel_ref, bias_ref, m_ref, acc_ref, cmax_ref, *s_refs,
                 blk, nblk, top, cb, n_items, qc, dh):
    slope = slope_ref[pl.program_id(1)] * LOG2E

    kr = lax.broadcasted_iota(jnp.int32, (blk, blk), 0)
    qq = lax.broadcasted_iota(jnp.int32, (blk, blk), 1)
    base = kr.astype(F32) * slope
    bias_ref[0] = base
    bias_ref[1] = jnp.where(kr <= qq, base, NEG_BIG)

    m_ref[...] = jnp.full(m_ref.shape, NEG_BIG, F32)
    acc_ref[...] = jnp.zeros(acc_ref.shape, F32)

    km = km_ref[0]
    km_hi = km.astype(BF16)
    km_lo = (km - km_hi.astype(F32)).astype(BF16)
    row = lax.broadcasted_iota(jnp.int32, (nblk, qc), 0)
    rowf = row.astype(F32)
    tiles_per_qc = qc // blk

    def select(u, carry):
        q0 = pl.multiple_of(u * qc, qc)
        qs = q_ref[0, pl.ds(q0, qc), :]
        gate = (lax.dot_general(km_hi, qs, _NT, preferred_element_type=F32)
                + lax.dot_general(km_lo, qs, _NT, preferred_element_type=F32))
        qblk = u * tiles_per_qc + lax.broadcasted_iota(jnp.int32, (nblk, qc), 1) // blk
        past = row < qblk
        g = jnp.where(past, gate, -jnp.inf)
        picked = jnp.zeros((nblk, qc), F32)
        for _ in range(top):
            mx = jnp.max(g, axis=0, keepdims=True)
            first = jnp.min(jnp.where(g == mx, rowf, float(nblk)), axis=0, keepdims=True)
            hit = rowf == first
            picked = jnp.where(hit, 1.0, picked)
            g = jnp.where(hit, -jnp.inf, g)
        selv = jnp.where(((picked > 0.5) & past) | (row == qblk), 1.0, 0.0)
        for t in range(tiles_per_qc):
            sel_ref[u * tiles_per_qc + t] = selv[:, t * blk:(t + 1) * blk]
        return carry

    lax.fori_loop(0, (nblk * blk) // qc, select, 0, unroll=True)

    def scores(item, slot):
        i = tile_ref[item]
        c = chunk_ref[item]
        qi = q_ref[0, pl.ds(pl.multiple_of(i * blk, blk), blk), :]
        for jb in range(cb):
            j = c * cb + jb
            kb = k_ref[0, pl.ds(pl.multiple_of(j * blk, blk), blk), :]
            sb = (lax.dot_general(kb, qi, _NT, preferred_element_type=F32)
                  + bias_ref[jnp.where(j == i, 1, 0)])
            s_refs[slot][jb * blk:(jb + 1) * blk, :] = sb
            cmax_ref[slot, jb] = jnp.max(sb, axis=0, keepdims=True)

    def softmax_pv(item, slot):
        i = tile_ref[item]
        c = chunk_ref[item]
        m_old = m_ref[i]
        sels, cjs = [], []
        m_new = m_old
        for jb in range(cb):
            j = c * cb + jb
            cj = ((j - i) * blk).astype(F32) * slope
            sel = sel_ref[i, pl.ds(j, 1), :] > 0.5
            m_new = jnp.maximum(m_new, jnp.where(sel, cmax_ref[slot, jb] + cj, NEG_BIG))
            sels.append(sel)
            cjs.append(cj)
        alpha = jnp.exp2(m_old - m_new)
        pv = None
        for jb in range(cb):
            shift = jnp.where(sels[jb], m_new - cjs[jb], POS_BIG)
            p = jnp.exp2(s_refs[slot][jb * blk:(jb + 1) * blk, :] - shift)
            d = jnp.dot(vt_ref[0, 0, c * cb + jb], p.astype(BF16), preferred_element_type=F32)
            pv = d if pv is None else pv + d
        acc_ref[i] = alpha * acc_ref[i] + pv
        m_ref[i] = m_new

    for slot in range(MOBA_DEPTH - 1):
        scores(slot, slot)

    def group(kk, carry):
        for u in range(MOBA_DEPTH * MOBA_ROUNDS):
            t = MOBA_DEPTH * MOBA_ROUNDS * kk + u
            scores(t + MOBA_DEPTH - 1, (u + MOBA_DEPTH - 1) % MOBA_DEPTH)
            softmax_pv(t, u % MOBA_DEPTH)
        return carry

    lax.fori_loop(0, n_items // (MOBA_DEPTH * MOBA_ROUNDS), group, 0)

    def finish(i, carry):
        acc = acc_ref[i]
        o = acc[:dh] * (1.0 / acc[dh:dh + 1])
        o_ref[0, pl.ds(pl.multiple_of(i * blk, blk), blk), :] = o.T.astype(BF16)
        return carry

    lax.fori_loop(0, nblk, finish, 0, unroll=True)


def _moba_attention(q, k, vt, km, slopes, *, heads, blk, top, cb):
    bsz, seq, d = q.shape
    dh = d // heads
    nblk = seq // blk
    assert nblk % cb == 0
    qc = min(seq, 4 * blk)
    n_items, tiles, chunks = _moba_items(nblk, cb)
    kern = functools.partial(_moba_kernel, blk=blk, nblk=nblk, top=min(top, nblk), cb=cb,
                             n_items=n_items, qc=qc, dh=dh)
    smem = pl.BlockSpec(memory_space=pltpu.SMEM)
    return pl.pallas_call(
        kern,
        grid=(bsz, heads),
        in_specs=[
            smem, smem, smem,
            pl.BlockSpec((1, seq, dh), lambda b, h: (b, 0, h)),
            pl.BlockSpec((1, seq, dh), lambda b, h: (b, 0, h)),
            pl.BlockSpec((1, 1, nblk, dh + BF16_ROWS, blk), lambda b, h: (b, h, 0, 0, 0)),
            pl.BlockSpec((1, nblk, dh), lambda b, h: (b, 0, h)),
        ],
        out_specs=pl.BlockSpec((1, seq, dh), lambda b, h: (b, 0, h)),
        out_shape=jax.ShapeDtypeStruct((bsz, seq, d), BF16),
        scratch_shapes=[
            pltpu.VMEM((nblk, nblk, blk), F32),
            pltpu.VMEM((2, blk, blk), F32),
            pltpu.VMEM((nblk, 1, blk), F32),
            pltpu.VMEM((nblk, dh + BF16_ROWS, blk), F32),
            pltpu.VMEM((MOBA_DEPTH, cb, 1, blk), F32),
        ] + [pltpu.VMEM((cb * blk, blk), F32)] * MOBA_DEPTH,
        compiler_params=_params("parallel", "parallel"),
        name="moba_attention",
    )(slopes, jnp.asarray(tiles, jnp.int32), jnp.asarray(chunks, jnp.int32), q, k, vt, km)


def _proj_mlp_kernel(a_ref, h_ref, wo_ref, gm_ref, g1_ref, wu_ref, wd_ref, g2_ref, o_ref,
                     wu_s, wd_s, h1_s, xn_s, acc_s, *, tf, ncast):
    s = pl.program_id(0)
    wc = wu_ref.shape[1]

    def mixer_residual():
        u = jnp.dot(a_ref[...], wo_ref[...].astype(BF16), preferred_element_type=F32)
        return h_ref[...] + _rms_scale(u, gm_ref[...])

    def mlp_chunk(xn, wu, wd):
        t = jnp.maximum(jnp.dot(xn, wu, preferred_element_type=F32), 0.0)
        return jnp.dot((t * t).astype(BF16), wd, preferred_element_type=F32)

    @pl.when(s == 0)
    def _():
        h1 = mixer_residual()
        h1_s[...] = h1
        xn_s[...] = _rms_scale(h1, g1_ref[...]).astype(BF16)
        acc_s[...] = jnp.zeros_like(acc_s)

    @pl.when(s < ncast)
    def _():
        c0 = pl.multiple_of(s * wc, wc)
        wu = wu_ref[...].astype(BF16)
        wd = wd_ref[...].astype(BF16)
        wu_s[:, pl.ds(c0, wc)] = wu
        wd_s[pl.ds(c0, wc), :] = wd
        acc_s[...] += mlp_chunk(xn_s[...], wu, wd)

    @pl.when(s == ncast - 1)
    def _():
        o_ref[...] = h1_s[...] + _rms_scale(acc_s[...], g2_ref[...])

    @pl.when(s >= ncast)
    def _():
        h1 = mixer_residual()
        xn = _rms_scale(h1, g1_ref[...]).astype(BF16)
        ff = wu_s.shape[1]
        acc = None
        for c in range(ff // tf):
            d = mlp_chunk(xn, wu_s[:, c * tf:(c + 1) * tf], wd_s[c * tf:(c + 1) * tf, :])
            acc = d if acc is None else acc + d
        o_ref[...] = h1 + _rms_scale(acc, g2_ref[...])


def _proj_mlp(a, h, w_o, g_mix, g_pre, w_up, w_down, g_post, *, layer, tm, tf, wc):
    m, d = h.shape
    kdim = a.shape[1]
    ff = w_up.shape[2]
    ncast = ff // wc
    row = lambda s: (jnp.maximum(s - (ncast - 1), 0), 0)
    const = lambda s: (0, 0)
    return pl.pallas_call(
        functools.partial(_proj_mlp_kernel, tf=tf, ncast=ncast),
        grid=(ncast - 1 + m // tm,),
        in_specs=[
            pl.BlockSpec((tm, kdim), row),
            pl.BlockSpec((tm, d), row),
            pl.BlockSpec((kdim, d), const, pipeline_mode=pl.Buffered(1)),
            pl.BlockSpec((1, d), const),
            pl.BlockSpec((1, d), const),
            pl.BlockSpec((None, d, wc), lambda s: (layer, 0, jnp.minimum(s, ncast - 1))),
            pl.BlockSpec((None, wc, d), lambda s: (layer, jnp.minimum(s, ncast - 1), 0)),
            pl.BlockSpec((1, d), const),
        ],
        out_specs=pl.BlockSpec((tm, d), row),
        out_shape=jax.ShapeDtypeStruct((m, d), F32),
        scratch_shapes=[
            pltpu.VMEM((d, ff), BF16),
            pltpu.VMEM((ff, d), BF16),
            pltpu.VMEM((tm, d), F32),
            pltpu.VMEM((tm, d), BF16),
            pltpu.VMEM((tm, d), F32),
        ],
        compiler_params=_params("arbitrary"),
        name="proj_mlp",
    )(a, h, w_o, g_mix.reshape(1, d), g_pre.reshape(1, d), w_up, w_down, g_post.reshape(1, d))


def _mlstm_in_kernel(x_ref, g_ref, w_ref, wg_ref, bg_ref,
                     qt_ref, k_ref, vt_ref, op_ref, gates_ref,
                     wqt_ref, wvt_ref, *, heads, kscale):
    nq = wqt_ref.shape[0]
    d = wvt_ref.shape[0]

    @pl.when(pl.program_id(0) == 0)
    def _():
        wqt_ref[...] = w_ref[:, :nq].astype(F32).T.astype(BF16)
        wvt_ref[...] = w_ref[:, 2 * nq:2 * nq + d].astype(F32).T.astype(BF16)

    xn = _rms_scale(x_ref[...], g_ref[...]).astype(BF16)
    qt_ref[0] = lax.dot_general(wqt_ref[...], xn, _NT, preferred_element_type=F32).astype(BF16)
    k_ref[...] = (jnp.dot(xn, w_ref[:, nq:2 * nq], preferred_element_type=F32) * kscale).astype(BF16)
    vt_ref[0] = lax.dot_general(wvt_ref[...], xn, _NT, preferred_element_type=F32).astype(BF16)
    op_ref[...] = jnp.dot(xn, w_ref[:, 2 * nq + d:2 * nq + 2 * d],
                          preferred_element_type=F32).astype(BF16)
    z = jnp.dot(xn, wg_ref[...], preferred_element_type=F32) + bg_ref[...]
    z = GATE_SOFTCAP * jnp.tanh(z * (1.0 / GATE_SOFTCAP))
    log_f = jnp.minimum(z, 0.0) - jnp.log1p(jnp.exp(-jnp.abs(z)))
    lane = lax.broadcasted_iota(jnp.int32, z.shape, 1)
    gates_ref[...] = jnp.where(lane < heads, z, log_f)


def _mlstm_in_proj(x, gain, w_in, wg, bg, *, bsz, heads, dqk, tm):
    m, d = x.shape
    seq = m // bsz
    nq = heads * dqk
    dv_all = d
    tiles_per_b = seq // tm
    kern = functools.partial(_mlstm_in_kernel, heads=heads, kscale=dqk ** -0.5)
    const = lambda i: (0, 0)
    row = lambda i: (i, 0)
    tcol = lambda i: (i // tiles_per_b, 0, i % tiles_per_b)
    return pl.pallas_call(
        kern,
        grid=(m // tm,),
        in_specs=[
            pl.BlockSpec((tm, d), row),
            pl.BlockSpec((1, d), const),
            pl.BlockSpec(w_in.shape, const, pipeline_mode=pl.Buffered(1)),
            pl.BlockSpec((d, LANES), const),
            pl.BlockSpec((1, LANES), const),
        ],
        out_specs=[
            pl.BlockSpec((1, nq, tm), tcol),
            pl.BlockSpec((tm, nq), row),
            pl.BlockSpec((1, dv_all, tm), tcol),
            pl.BlockSpec((tm, dv_all), row),
            pl.BlockSpec((tm, LANES), row),
        ],
        out_shape=[
            jax.ShapeDtypeStruct((bsz, nq, seq), BF16),
            jax.ShapeDtypeStruct((m, nq), BF16),
            jax.ShapeDtypeStruct((bsz, dv_all, seq), BF16),
            jax.ShapeDtypeStruct((m, dv_all), BF16),
            jax.ShapeDtypeStruct((m, LANES), F32),
        ],
        scratch_shapes=[
            pltpu.VMEM((nq, d), BF16),
            pltpu.VMEM((dv_all, d), BF16),
        ],
        compiler_params=_params("arbitrary"),
        name="mlstm_in_proj",
    )(x, gain.reshape(1, d), w_in, wg, bg)


def _mlstm_kernel(qt_ref, k_ref, vt_ref, op_ref, g_ref, gnext_ref, nh_ref, y_ref,
                  c_ref, m_ref, rows_ref, ucol_ref, *, heads, dqk, dv, chunk):
    L = chunk
    ss = lax.broadcasted_iota(jnp.int32, (L, L), 0)
    tt = lax.broadcasted_iota(jnp.int32, (L, L), 1)
    causal = ss <= tt

    def gate_terms(gates):
        gates_t = gates.T
        i_rows = gates_t[0:heads]
        f_rows = gates_t[heads:2 * heads]
        tri = causal.astype(BF16)
        f_hi = f_rows.astype(BF16)
        f_r1 = f_rows - f_hi.astype(F32)
        f_mid = f_r1.astype(BF16)
        f_lo = (f_r1 - f_mid.astype(F32)).astype(BF16)
        b_rows = (jnp.dot(f_hi, tri, preferred_element_type=F32)
                  + jnp.dot(f_mid, tri, preferred_element_type=F32)
                  + jnp.dot(f_lo, tri, preferred_element_type=F32))
        rows_ref[0:heads] = i_rows
        rows_ref[heads:2 * heads] = b_rows
        ucol_ref[...] = jnp.concatenate(
            [i_rows - b_rows, jnp.zeros((LANES - heads, L), F32)], axis=0).T

    @pl.when(pl.program_id(1) == 0)
    def _():
        c_ref[...] = jnp.zeros_like(c_ref)
        m_ref[...] = jnp.zeros_like(m_ref)
        gate_terms(g_ref[0])

    ones_rows = (lax.broadcasted_iota(jnp.int32, (BF16_ROWS, L), 0) == 0).astype(BF16)
    sub = lax.broadcasted_iota(jnp.int32, (2 * dqk, 1), 0)
    lane = lax.broadcasted_iota(jnp.int32, (1, 2 * dqk), 1)

    first = lane < dqk
    kgs, cts, s_qk, inter = [], [], [], []
    for h in range(heads):
        p, half = divmod(h, 2)
        qt_pair = qt_ref[0, p * 2 * dqk:(p + 1) * 2 * dqk, :]
        kg = k_ref[0, :, p * 2 * dqk:(p + 1) * 2 * dqk]
        in_head = (sub >= half * dqk) & (sub < (half + 1) * dqk)
        qt_h = jnp.where(in_head, qt_pair, jnp.zeros_like(qt_pair))
        ct = c_ref[p]
        kgs.append(kg)
        cts.append(ct)
        s_qk.append(jnp.dot(kg, qt_h, preferred_element_type=F32))
        inter.append(jnp.dot(ct.astype(BF16), qt_h, preferred_element_type=F32))

    s_ts, m_ts, w_inters, vexts = [], [], [], []
    for h in range(heads):
        b_row = rows_ref[heads + h:heads + h + 1, :]
        m_prev = m_ref[h][:, 0:1]
        log_d = jnp.where(causal, ucol_ref[:, h:h + 1] + b_row, -jnp.inf)
        log_inter = b_row + m_prev
        m_t = jnp.maximum(log_inter, jnp.max(log_d, axis=0, keepdims=True))
        s_ts.append((s_qk[h] * jnp.exp(log_d - m_t)).astype(BF16))
        w_inters.append(jnp.exp(log_inter - m_t))
        m_ts.append(m_t)
        vexts.append(jnp.concatenate([vt_ref[0, h * dv:(h + 1) * dv, :], ones_rows], axis=0))

    for h in range(heads):
        num = (w_inters[h] * inter[h]
               + jnp.dot(vexts[h], s_ts[h], preferred_element_type=F32))
        den = num[dv:dv + 1]
        hout = num[:dv] * (1.0 / jnp.maximum(jnp.abs(den), jnp.exp(-m_ts[h])))
        hn = hout * lax.rsqrt(jnp.mean(hout * hout, axis=0, keepdims=True) + RMS_EPS)
        og = jax.nn.sigmoid(op_ref[0, :, h * dv:(h + 1) * dv].astype(F32))
        y_ref[0, :, h * dv:(h + 1) * dv] = (
            og * (hn.T * nh_ref[:, h * dv:(h + 1) * dv])).astype(BF16)

    upds, decays = [], []
    for h in range(heads):
        b_row = rows_ref[heads + h:heads + h + 1, :]
        i_row = rows_ref[h:h + 1, :]
        m_prev = m_ref[h][:, 0:1]
        b_end = b_row[:, L - 1:L]
        log_w = b_end - b_row + i_row
        m_new = jnp.maximum(b_end + m_prev, jnp.max(log_w, axis=1, keepdims=True))
        decays.append(jnp.exp(b_end + m_prev - m_new))
        vw = (vexts[h].astype(F32) * jnp.exp(log_w - m_new)).astype(BF16)
        upds.append(jnp.dot(vw, kgs[h], preferred_element_type=F32))
        m_ref[h] = jnp.broadcast_to(m_new, (1, LANES))
    for p in range(heads // 2):
        c_ref[p] = (jnp.where(first, decays[2 * p], decays[2 * p + 1]) * cts[2 * p]
                    + jnp.where(first, upds[2 * p], upds[2 * p + 1]))

    gate_terms(gnext_ref[0])


def _mlstm_scan(qt, k, vt, opre, gates, norm_h, *, heads, chunk):
    bsz, dv_all, seq = vt.shape
    dv = dv_all // heads
    dqk = qt.shape[1] // heads
    kern = functools.partial(_mlstm_kernel, heads=heads, dqk=dqk, dv=dv, chunk=chunk)
    nchunk = seq // chunk
    rows = lambda b, c: (b, c, 0)
    cols = lambda b, c: (b, 0, c)
    return pl.pallas_call(
        kern,
        grid=(bsz, seq // chunk),
        in_specs=[
            pl.BlockSpec((1, heads * dqk, chunk), cols),
            pl.BlockSpec((1, chunk, heads * dqk), rows),
            pl.BlockSpec((1, dv_all, chunk), cols),
            pl.BlockSpec((1, chunk, dv_all), rows),
            pl.BlockSpec((1, chunk, LANES), rows),
            pl.BlockSpec((1, chunk, LANES), lambda b, c: (b, jnp.minimum(c + 1, nchunk - 1), 0)),
            pl.BlockSpec((1, dv_all), lambda b, c: (0, 0)),
        ],
        out_specs=pl.BlockSpec((1, chunk, dv_all), rows),
        out_shape=jax.ShapeDtypeStruct((bsz, seq, dv_all), BF16),
        scratch_shapes=[
            pltpu.VMEM((heads // 2, dv + BF16_ROWS, 2 * dqk), F32),
            pltpu.VMEM((heads, 1, LANES), F32),
            pltpu.VMEM((2 * heads, chunk), F32),
            pltpu.VMEM((chunk, LANES), F32),
        ],
        compiler_params=_params("parallel", "arbitrary"),
        name="mlstm_scan",
    )(qt, k, vt, opre, gates, gates, norm_h.reshape(1, dv_all))


def kernel(x, norm_mix_pre, norm_mix_post, norm_ffn_pre, norm_ffn_post, w_up, w_down,
           attn_w_qkv, attn_w_o, mlstm_w_in, mlstm_b_gates, mlstm_norm_h, mlstm_w_out):
    bsz, seq, d = x.shape
    m = bsz * seq
    heads = ATT_HEADS
    mh = MLSTM_HEADS
    dv = d // mh
    dqk = dv // 2

    q, k, km, vt = _qkv_proj(x, norm_mix_pre[0], attn_w_qkv[0], heads=heads, blk=MOBA_BLOCK,
                             tm=4 * MOBA_BLOCK)
    slopes = jnp.exp2(-8.0 * jnp.arange(1, heads + 1, dtype=F32) / heads)
    att = _moba_attention(q, k, vt, km, slopes, heads=heads, blk=MOBA_BLOCK, top=MOBA_TOPK, cb=2)
    h = _proj_mlp(att.reshape(m, d), x.reshape(m, d), attn_w_o[0], norm_mix_post[0],
                  norm_ffn_pre[0], w_up, w_down, norm_ffn_post[0], layer=0, tm=512, tf=1024, wc=512)

    nq = mh * dqk
    w_in = mlstm_w_in[0, :, :2 * nq + 2 * d].astype(BF16)
    wg = jnp.pad(mlstm_w_in[0, :, 2 * nq + 2 * d:], ((0, 0), (0, LANES - 2 * mh))).astype(BF16)
    bg = jnp.pad(mlstm_b_gates[0], (0, LANES - 2 * mh)).reshape(1, LANES)
    qt, kmm, vmt, opre, gates = _mlstm_in_proj(h, norm_mix_pre[1], w_in, wg, bg,
                                               bsz=bsz, heads=mh, dqk=dqk, tm=1024)
    y = _mlstm_scan(qt, kmm.reshape(bsz, seq, nq), vmt, opre.reshape(bsz, seq, d),
                    gates.reshape(bsz, seq, LANES), mlstm_norm_h[0], heads=mh, chunk=256)
    h = _proj_mlp(y.reshape(m, d), h, mlstm_w_out[0], norm_mix_post[1],
                  norm_ffn_pre[1], w_up, w_down, norm_ffn_post[1], layer=1, tm=512, tf=1024, wc=512)
    return h.reshape(bsz, seq, d)
```

```python
import functools

import jax
import jax.numpy as jnp
from jax import lax
from jax.experimental import pallas as pl
from jax.experimental.pallas import tpu as pltpu

F32 = jnp.float32
BF16 = jnp.bfloat16

RMS_EPS = 1e-6
ATT_HEADS = 8
MOBA_BLOCK = 256
MOBA_TOPK = 3
MLSTM_HEADS = 8
GATE_SOFTCAP = 15.0

LANES = 128
VMEM_LIMIT = 56 * 1024 * 1024
NEG_BIG = -1e30
POS_BIG = 1e30
LOG2E = 1.4426950408889634
BF16_ROWS = 16
MOBA_DEPTH = 17
MOBA_ROUNDS = 4

_NT = (((1,), (1,)), ((), ()))


def _params(*sem):
    return pltpu.CompilerParams(dimension_semantics=sem, vmem_limit_bytes=VMEM_LIMIT)


def _rms_scale(x, gain):
    ms = jnp.mean(x * x, axis=-1, keepdims=True)
    return x * lax.rsqrt(ms + RMS_EPS) * gain


def _qkv_kernel(x_ref, g_ref, w_ref, q_ref, k_ref, km_ref, vt_ref, wq_ref, wk_ref, wvt_ref,
                *, scale, blk, heads):
    d = x_ref.shape[1]

    @pl.when(pl.program_id(0) == 0)
    def _():
        wq_ref[...] = w_ref[:, :d].astype(BF16)
        wk_ref[...] = w_ref[:, d:2 * d].astype(BF16)
        wvt_ref[...] = w_ref[:, 2 * d:].T.astype(BF16)

    xn = _rms_scale(x_ref[...], g_ref[...]).astype(BF16)
    tm = xn.shape[0]
    dh = d // heads
    q = jnp.dot(xn, wq_ref[...], preferred_element_type=F32)
    q_ref[...] = (q * scale).astype(BF16)
    k = jnp.dot(xn, wk_ref[...], preferred_element_type=F32)
    k_ref[...] = k.astype(BF16)
    for c in range(tm // blk):
        km_ref[c] = jnp.mean(k[c * blk:(c + 1) * blk], axis=0, keepdims=True)
    vt = lax.dot_general(wvt_ref[...], xn, _NT, preferred_element_type=F32)
    ones_row = (lax.broadcasted_iota(jnp.int32, (BF16_ROWS, blk), 0) == 0).astype(BF16)
    for h in range(heads):
        for c in range(tm // blk):
            vt_ref[0, h, c, :dh, :] = vt[h * dh:(h + 1) * dh, c * blk:(c + 1) * blk].astype(BF16)
            vt_ref[0, h, c, dh:, :] = ones_row


def _qkv_proj(x, gain, w_qkv, *, heads, blk, tm):
    bsz, seq, d = x.shape
    dh = d // heads
    nblk = seq // blk
    m = bsz * seq
    cpt = tm // blk
    tiles_per_b = seq // tm
    kern = functools.partial(_qkv_kernel, scale=dh ** -0.5 * LOG2E, blk=blk, heads=heads)
    const = lambda i: (0, 0)
    q, k, km, vt = pl.pallas_call(
        kern,
        grid=(m // tm,),
        in_specs=[
            pl.BlockSpec((tm, d), lambda i: (i, 0)),
            pl.BlockSpec((1, d), const),
            pl.BlockSpec((d, 3 * d), const, pipeline_mode=pl.Buffered(1)),
        ],
        out_specs=[
            pl.BlockSpec((tm, d), lambda i: (i, 0)),
            pl.BlockSpec((tm, d), lambda i: (i, 0)),
            pl.BlockSpec((cpt, 1, d), lambda i: (i, 0, 0)),
            pl.BlockSpec((1, heads, cpt, dh + BF16_ROWS, blk),
                         lambda i: (i // tiles_per_b, 0, i % tiles_per_b, 0, 0)),
        ],
        out_shape=[
            jax.ShapeDtypeStruct((m, d), BF16),
            jax.ShapeDtypeStruct((m, d), BF16),
            jax.ShapeDtypeStruct((m // blk, 1, d), F32),
            jax.ShapeDtypeStruct((bsz, heads, nblk, dh + BF16_ROWS, blk), BF16),
        ],
        scratch_shapes=[pltpu.VMEM((d, d), BF16)] * 3,
        compiler_params=_params("arbitrary"),
        name="qkv_proj",
    )(x.reshape(m, d), gain.reshape(1, d), w_qkv)
    return (q.reshape(bsz, seq, d), k.reshape(bsz, seq, d),
            km.reshape(bsz, nblk, d), vt)


def _moba_items(nblk, cb):
    tiles, chunks = [], []
    for i in range(nblk):
        for c in range(-(-(i + 1) // cb)):
            tiles.append(i)
            chunks.append(c)
    while len(tiles) % (MOBA_DEPTH * MOBA_ROUNDS):
        assert nblk >= 2 * cb
        tiles.append(0)
        chunks.append(1)
    n_items = len(tiles)
    tiles += [0] * (MOBA_DEPTH - 1)
    chunks += [0] * (MOBA_DEPTH - 1)
    return n_items, tiles, chunks


def _moba_kernel(slope_ref, tile_ref, chunk_ref, q_ref, k_ref, vt_ref, km_ref, o_ref,
                 sel_ref, bias_ref, m_ref, acc_ref, cmax_ref, *s_refs,
                 blk, nblk, top, cb, n_items, qc, dh):
    slope = slope_ref[pl.program_id(1)] * LOG2E

    kr = lax.broadcasted_iota(jnp.int32, (blk, blk), 0)
    qq = lax.broadcasted_iota(jnp.int32, (blk, blk), 1)
    base = kr.astype(F32) * slope
    bias_ref[0] = base
    bias_ref[1] = jnp.where(kr <= qq, base, NEG_BIG)

    m_ref[...] = jnp.full(m_ref.shape, NEG_BIG, F32)
    acc_ref[...] = jnp.zeros(acc_ref.shape, F32)

    km = km_ref[0]
    km_hi = km.astype(BF16)
    km_lo = (km - km_hi.astype(F32)).astype(BF16)
    row = lax.broadcasted_iota(jnp.int32, (nblk, qc), 0)
    rowf = row.astype(F32)
    tiles_per_qc = qc // blk

    def select(u, carry):
        q0 = pl.multiple_of(u * qc, qc)
        qs = q_ref[0, pl.ds(q0, qc), :]
        gate = (lax.dot_general(km_hi, qs, _NT, preferred_element_type=F32)
                + lax.dot_general(km_lo, qs, _NT, preferred_element_type=F32))
        qblk = u * tiles_per_qc + lax.broadcasted_iota(jnp.int32, (nblk, qc), 1) // blk
        past = row < qblk
        g = jnp.where(past, gate, -jnp.inf)
        picked = jnp.zeros((nblk, qc), F32)
        for _ in range(top):
            mx = jnp.max(g, axis=0, keepdims=True)
            first = jnp.min(jnp.where(g == mx, rowf, float(nblk)), axis=0, keepdims=True)
            hit = rowf == first
            picked = jnp.where(hit, 1.0, picked)
            g = jnp.where(hit, -jnp.inf, g)
        selv = jnp.where(((picked > 0.5) & past) | (row == qblk), 1.0, 0.0)
        for t in range(tiles_per_qc):
            sel_ref[u * tiles_per_qc + t] = selv[:, t * blk:(t + 1) * blk]
        return carry

    lax.fori_loop(0, (nblk * blk) // qc, select, 0, unroll=True)

    def scores(item, slot):
        i = tile_ref[item]
        c = chunk_ref[item]
        qi = q_ref[0, pl.ds(pl.multiple_of(i * blk, blk), blk), :]
        for jb in range(cb):
            j = c * cb + jb
            kb = k_ref[0, pl.ds(pl.multiple_of(j * blk, blk), blk), :]
            sb = (lax.dot_general(kb, qi, _NT, preferred_element_type=F32)
                  + bias_ref[jnp.where(j == i, 1, 0)])
            s_refs[slot][jb * blk:(jb + 1) * blk, :] = sb
            cmax_ref[slot, jb] = jnp.max(sb, axis=0, keepdims=True)

    def softmax_pv(item, slot):
        i = tile_ref[item]
        c = chunk_ref[item]
        m_old = m_ref[i]
        sels, cjs = [], []
        m_new = m_old
        for jb in range(cb):
            j = c * cb + jb
            cj = ((j - i) * blk).astype(F32) * slope
            sel = sel_ref[i, pl.ds(j, 1), :] > 0.5
            m_new = jnp.maximum(m_new, jnp.where(sel, cmax_ref[slot, jb] + cj, NEG_BIG))
            sels.append(sel)
            cjs.append(cj)
        alpha = jnp.exp2(m_old - m_new)
        pv = None
        for jb in range(cb):
            shift = jnp.where(sels[jb], m_new - cjs[jb], POS_BIG)
            p = jnp.exp2(s_refs[slot][jb * blk:(jb + 1) * blk, :] - shift)
            d = jnp.dot(vt_ref[0, 0, c * cb + jb], p.astype(BF16), preferred_element_type=F32)
            pv = d if pv is None else pv + d
        acc_ref[i] = alpha * acc_ref[i] + pv
        m_ref[i] = m_new

    for slot in range(MOBA_DEPTH - 1):
        scores(slot, slot)

    def group(kk, carry):
        for u in range(MOBA_DEPTH * MOBA_ROUNDS):
            t = MOBA_DEPTH * MOBA_ROUNDS * kk + u
            scores(t + MOBA_DEPTH - 1, (u + MOBA_DEPTH - 1) % MOBA_DEPTH)
            softmax_pv(t, u % MOBA_DEPTH)
        return carry

    lax.fori_loop(0, n_items // (MOBA_DEPTH * MOBA_ROUNDS), group, 0)

    def finish(i, carry):
        acc = acc_ref[i]
        o = acc[:dh] * (1.0 / acc[dh:dh + 1])
        o_ref[0, pl.ds(pl.multiple_of(i * blk, blk), blk), :] = o.T.astype(BF16)
        return carry

    lax.fori_loop(0, nblk, finish, 0, unroll=True)


def _moba_attention(q, k, vt, km, slopes, *, heads, blk, top, cb):
    bsz, seq, d = q.shape
    dh = d // heads
    nblk = seq // blk
    assert nblk % cb == 0
    qc = min(seq, 4 * blk)
    n_items, tiles, chunks = _moba_items(nblk, cb)
    kern = functools.partial(_moba_kernel, blk=blk, nblk=nblk, top=min(top, nblk), cb=cb,
                             n_items=n_items, qc=qc, dh=dh)
    smem = pl.BlockSpec(memory_space=pltpu.SMEM)
    return pl.pallas_call(
        kern,
        grid=(bsz, heads),
        in_specs=[
            smem, smem, smem,
            pl.BlockSpec((1, seq, dh), lambda b, h: (b, 0, h)),
            pl.BlockSpec((1, seq, dh), lambda b, h: (b, 0, h)),
            pl.BlockSpec((1, 1, nblk, dh + BF16_ROWS, blk), lambda b, h: (b, h, 0, 0, 0)),
            pl.BlockSpec((1, nblk, dh), lambda b, h: (b, 0, h)),
        ],
        out_specs=pl.BlockSpec((1, seq, dh), lambda b, h: (b, 0, h)),
        out_shape=jax.ShapeDtypeStruct((bsz, seq, d), BF16),
        scratch_shapes=[
            pltpu.VMEM((nblk, nblk, blk), F32),
            pltpu.VMEM((2, blk, blk), F32),
            pltpu.VMEM((nblk, 1, blk), F32),
            pltpu.VMEM((nblk, dh + BF16_ROWS, blk), F32),
            pltpu.VMEM((MOBA_DEPTH, cb, 1, blk), F32),
        ] + [pltpu.VMEM((cb * blk, blk), F32)] * MOBA_DEPTH,
        compiler_params=_params("parallel", "parallel"),
        name="moba_attention",
    )(slopes, jnp.asarray(tiles, jnp.int32), jnp.asarray(chunks, jnp.int32), q, k, vt, km)


def _proj_mlp_kernel(a_ref, h_ref, wo_ref, gm_ref, g1_ref, wu_ref, wd_ref, g2_ref, o_ref,
                     wu_s, wd_s, h1_s, xn_s, acc_s, *, tf, ncast):
    s = pl.program_id(0)
    wc = wu_ref.shape[1]

    def mixer_residual():
        u = jnp.dot(a_ref[...], wo_ref[...].astype(BF16), preferred_element_type=F32)
        return h_ref[...] + _rms_scale(u, gm_ref[...])

    def mlp_chunk(xn, wu, wd):
        t = jnp.maximum(jnp.dot(xn, wu, preferred_element_type=F32), 0.0)
        return jnp.dot((t * t).astype(BF16), wd, preferred_element_type=F32)

    @pl.when(s == 0)
    def _():
        h1 = mixer_residual()
        h1_s[...] = h1
        xn_s[...] = _rms_scale(h1, g1_ref[...]).astype(BF16)
        acc_s[...] = jnp.zeros_like(acc_s)

    @pl.when(s < ncast)
    def _():
        c0 = pl.multiple_of(s * wc, wc)
        wu = wu_ref[...].astype(BF16)
        wd = wd_ref[...].astype(BF16)
        wu_s[:, pl.ds(c0, wc)] = wu
        wd_s[pl.ds(c0, wc), :] = wd
        acc_s[...] += mlp_chunk(xn_s[...], wu, wd)

    @pl.when(s == ncast - 1)
    def _():
        o_ref[...] = h1_s[...] + _rms_scale(acc_s[...], g2_ref[...])

    @pl.when(s >= ncast)
    def _():
        h1 = mixer_residual()
        xn = _rms_scale(h1, g1_ref[...]).astype(BF16)
        ff = wu_s.shape[1]
        acc = None
        for c in range(ff // tf):
            d = mlp_chunk(xn, wu_s[:, c * tf:(c + 1) * tf], wd_s[c * tf:(c + 1) * tf, :])
            acc = d if acc is None else acc + d
        o_ref[...] = h1 + _rms_scale(acc, g2_ref[...])


def _proj_mlp(a, h, w_o, g_mix, g_pre, w_up, w_down, g_post, *, layer, tm, tf, wc):
    m, d = h.shape
    kdim = a.shape[1]
    ff = w_up.shape[2]
    ncast = ff // wc
    row = lambda s: (jnp.maximum(s - (ncast - 1), 0), 0)
    const = lambda s: (0, 0)
    return pl.pallas_call(
        functools.partial(_proj_mlp_kernel, tf=tf, ncast=ncast),
        grid=(ncast - 1 + m // tm,),
        in_specs=[
            pl.BlockSpec((tm, kdim), row),
            pl.BlockSpec((tm, d), row),
            pl.BlockSpec((kdim, d), const, pipeline_mode=pl.Buffered(1)),
            pl.BlockSpec((1, d), const),
            pl.BlockSpec((1, d), const),
            pl.BlockSpec((None, d, wc), lambda s: (layer, 0, jnp.minimum(s, ncast - 1))),
            pl.BlockSpec((None, wc, d), lambda s: (layer, jnp.minimum(s, ncast - 1), 0)),
            pl.BlockSpec((1, d), const),
        ],
        out_specs=pl.BlockSpec((tm, d), row),
        out_shape=jax.ShapeDtypeStruct((m, d), F32),
        scratch_shapes=[
            pltpu.VMEM((d, ff), BF16),
            pltpu.VMEM((ff, d), BF16),
            pltpu.VMEM((tm, d), F32),
            pltpu.VMEM((tm, d), BF16),
            pltpu.VMEM((tm, d), F32),
        ],
        compiler_params=_params("arbitrary"),
        name="proj_mlp",
    )(a, h, w_o, g_mix.reshape(1, d), g_pre.reshape(1, d), w_up, w_down, g_post.reshape(1, d))


def _mlstm_in_kernel(x_ref, g_ref, w_ref, wg_ref, bg_ref,
                     qt_ref, k_ref, vt_ref, op_ref, gates_ref,
                     wqt_ref, wvt_ref, *, heads, kscale):
    nq = wqt_ref.shape[0]
    d = wvt_ref.shape[0]

    @pl.when(pl.program_id(0) == 0)
    def _():
        wqt_ref[...] = w_ref[:, :nq].astype(F32).T.astype(BF16)
        wvt_ref[...] = w_ref[:, 2 * nq:2 * nq + d].astype(F32).T.astype(BF16)

    xn = _rms_scale(x_ref[...], g_ref[...]).astype(BF16)
    qt_ref[0] = lax.dot_general(wqt_ref[...], xn, _NT, preferred_element_type=F32).astype(BF16)
    k_ref[...] = (jnp.dot(xn, w_ref[:, nq:2 * nq], preferred_element_type=F32) * kscale).astype(BF16)
    vt_ref[0] = lax.dot_general(wvt_ref[...], xn, _NT, preferred_element_type=F32).astype(BF16)
    op_ref[...] = jnp.dot(xn, w_ref[:, 2 * nq + d:2 * nq + 2 * d],
                          preferred_element_type=F32).astype(BF16)
    z = jnp.dot(xn, wg_ref[...], preferred_element_type=F32) + bg_ref[...]
    z = GATE_SOFTCAP * jnp.tanh(z * (1.0 / GATE_SOFTCAP))
    log_f = jnp.minimum(z, 0.0) - jnp.log1p(jnp.exp(-jnp.abs(z)))
    lane = lax.broadcasted_iota(jnp.int32, z.shape, 1)
    gates_ref[...] = jnp.where(lane < heads, z, log_f) * LOG2E


def _mlstm_in_proj(x, gain, w_in, wg, bg, *, bsz, heads, dqk, tm):
    m, d = x.shape
    seq = m // bsz
    nq = heads * dqk
    dv_all = d
    tiles_per_b = seq // tm
    kern = functools.partial(_mlstm_in_kernel, heads=heads, kscale=dqk ** -0.5)
    const = lambda i: (0, 0)
    row = lambda i: (i, 0)
    tcol = lambda i: (i // tiles_per_b, 0, i % tiles_per_b)
    return pl.pallas_call(
        kern,
        grid=(m // tm,),
        in_specs=[
            pl.BlockSpec((tm, d), row),
            pl.BlockSpec((1, d), const),
            pl.BlockSpec(w_in.shape, const, pipeline_mode=pl.Buffered(1)),
            pl.BlockSpec((d, LANES), const),
            pl.BlockSpec((1, LANES), const),
        ],
        out_specs=[
            pl.BlockSpec((1, nq, tm), tcol),
            pl.BlockSpec((tm, nq), row),
            pl.BlockSpec((1, dv_all, tm), tcol),
            pl.BlockSpec((tm, dv_all), row),
            pl.BlockSpec((tm, LANES), row),
        ],
        out_shape=[
            jax.ShapeDtypeStruct((bsz, nq, seq), BF16),
            jax.ShapeDtypeStruct((m, nq), BF16),
            jax.ShapeDtypeStruct((bsz, dv_all, seq), BF16),
            jax.ShapeDtypeStruct((m, dv_all), BF16),
            jax.ShapeDtypeStruct((m, LANES), F32),
        ],
        scratch_shapes=[
            pltpu.VMEM((nq, d), BF16),
            pltpu.VMEM((dv_all, d), BF16),
        ],
        compiler_params=_params("arbitrary"),
        name="mlstm_in_proj",
    )(x, gain.reshape(1, d), w_in, wg, bg)


def _mlstm_kernel(qt_ref, k_ref, vt_ref, op_ref, g_ref, gnext_ref, nh_ref, y_ref,
                  c_ref, m_ref, rows_ref, ucol_ref, *, heads, dqk, dv, chunk):
    L = chunk
    ss = lax.broadcasted_iota(jnp.int32, (L, L), 0)
    tt = lax.broadcasted_iota(jnp.int32, (L, L), 1)
    causal = ss <= tt

    def gate_terms(gates):
        gates_t = gates.T
        i_rows = gates_t[0:heads]
        f_rows = gates_t[heads:2 * heads]
        tri = causal.astype(BF16)
        f_hi = f_rows.astype(BF16)
        f_r1 = f_rows - f_hi.astype(F32)
        f_mid = f_r1.astype(BF16)
        f_lo = (f_r1 - f_mid.astype(F32)).astype(BF16)
        b_rows = (jnp.dot(f_hi, tri, preferred_element_type=F32)
                  + jnp.dot(f_mid, tri, preferred_element_type=F32)
                  + jnp.dot(f_lo, tri, preferred_element_type=F32))
        rows_ref[0:heads] = i_rows
        rows_ref[heads:2 * heads] = b_rows
        ucol_ref[...] = jnp.concatenate(
            [i_rows - b_rows, jnp.zeros((LANES - heads, L), F32)], axis=0).T

    @pl.when(pl.program_id(1) == 0)
    def _():
        c_ref[...] = jnp.zeros_like(c_ref)
        m_ref[...] = jnp.zeros_like(m_ref)
        gate_terms(g_ref[0])

    ones_rows = (lax.broadcasted_iota(jnp.int32, (BF16_ROWS, L), 0) == 0).astype(BF16)
    sub = lax.broadcasted_iota(jnp.int32, (2 * dqk, 1), 0)
    lane = lax.broadcasted_iota(jnp.int32, (1, 2 * dqk), 1)

    first = lane < dqk
    group = heads // 2
    for h0 in range(0, heads, group):
        hs = range(h0, h0 + group)
        kgs, cts, s_qk, inter = [], [], [], []
        for h in hs:
            p, half = divmod(h, 2)
            qt_pair = qt_ref[0, p * 2 * dqk:(p + 1) * 2 * dqk, :]
            kg = k_ref[0, :, p * 2 * dqk:(p + 1) * 2 * dqk]
            in_head = (sub >= half * dqk) & (sub < (half + 1) * dqk)
            qt_h = jnp.where(in_head, qt_pair, jnp.zeros_like(qt_pair))
            ct = c_ref[p]
            kgs.append(kg)
            cts.append(ct)
            s_qk.append(jnp.dot(kg, qt_h, preferred_element_type=F32))
            inter.append(jnp.dot(ct.astype(BF16), qt_h, preferred_element_type=F32))

        s_ts, m_ts, w_inters, vexts = [], [], [], []
        for n, h in enumerate(hs):
            b_row = rows_ref[heads + h:heads + h + 1, :]
            m_prev = m_ref[h][:, 0:1]
            log_d = jnp.where(causal, ucol_ref[:, h:h + 1] + b_row, -jnp.inf)
            log_inter = b_row + m_prev
            m_t = jnp.maximum(log_inter, jnp.max(log_d, axis=0, keepdims=True))
            s_ts.append((s_qk[n] * jnp.exp2(log_d - m_t)).astype(BF16))
            w_inters.append(jnp.exp2(log_inter - m_t))
            m_ts.append(m_t)
            vexts.append(jnp.concatenate([vt_ref[0, h * dv:(h + 1) * dv, :], ones_rows], axis=0))

        for n, h in enumerate(hs):
            num = (w_inters[n] * inter[n]
                   + jnp.dot(vexts[n], s_ts[n], preferred_element_type=F32))
            den = num[dv:dv + 1]
            hout = num[:dv] * (1.0 / jnp.maximum(jnp.abs(den), jnp.exp2(-m_ts[n])))
            hn = hout * lax.rsqrt(jnp.mean(hout * hout, axis=0, keepdims=True) + RMS_EPS)
            og = 0.5 * jnp.tanh(0.5 * op_ref[0, :, h * dv:(h + 1) * dv].astype(F32)) + 0.5
            y_ref[0, :, h * dv:(h + 1) * dv] = (
                og * (hn.T * nh_ref[:, h * dv:(h + 1) * dv])).astype(BF16)

        upds, decays = [], []
        for n, h in enumerate(hs):
            b_row = rows_ref[heads + h:heads + h + 1, :]
            i_row = rows_ref[h:h + 1, :]
            m_prev = m_ref[h][:, 0:1]
            b_end = b_row[:, L - 1:L]
            log_w = b_end - b_row + i_row
            m_new = jnp.maximum(b_end + m_prev, jnp.max(log_w, axis=1, keepdims=True))
            decays.append(jnp.exp2(b_end + m_prev - m_new))
            vw = (vexts[n].astype(F32) * jnp.exp2(log_w - m_new)).astype(BF16)
            upds.append(jnp.dot(vw, kgs[n], preferred_element_type=F32))
            m_ref[h] = jnp.broadcast_to(m_new, (1, LANES))
        for n in range(0, group, 2):
            c_ref[(h0 + n) // 2] = (jnp.where(first, decays[n], decays[n + 1]) * cts[n]
                                    + jnp.where(first, upds[n], upds[n + 1]))

    gate_terms(gnext_ref[0])


def _mlstm_scan(qt, k, vt, opre, gates, norm_h, *, heads, chunk):
    bsz, dv_all, seq = vt.shape
    dv = dv_all // heads
    dqk = qt.shape[1] // heads
    kern = functools.partial(_mlstm_kernel, heads=heads, dqk=dqk, dv=dv, chunk=chunk)
    nchunk = seq // chunk
    rows = lambda b, c: (b, c, 0)
    cols = lambda b, c: (b, 0, c)
    return pl.pallas_call(
        kern,
        grid=(bsz, seq // chunk),
        in_specs=[
            pl.BlockSpec((1, heads * dqk, chunk), cols),
            pl.BlockSpec((1, chunk, heads * dqk), rows),
            pl.BlockSpec((1, dv_all, chunk), cols),
            pl.BlockSpec((1, chunk, dv_all), rows),
            pl.BlockSpec((1, chunk, LANES), rows),
            pl.BlockSpec((1, chunk, LANES), lambda b, c: (b, jnp.minimum(c + 1, nchunk - 1), 0)),
            pl.BlockSpec((1, dv_all), lambda b, c: (0, 0)),
        ],
        out_specs=pl.BlockSpec((1, chunk, dv_all), rows),
        out_shape=jax.ShapeDtypeStruct((bsz, seq, dv_all), BF16),
        scratch_shapes=[
            pltpu.VMEM((heads // 2, dv + BF16_ROWS, 2 * dqk), F32),
            pltpu.VMEM((heads, 1, LANES), F32),
            pltpu.VMEM((2 * heads, chunk), F32),
            pltpu.VMEM((chunk, LANES), F32),
        ],
        compiler_params=_params("parallel", "arbitrary"),
        name="mlstm_scan",
    )(qt, k, vt, opre, gates, gates, norm_h.reshape(1, dv_all))


def kernel(x, norm_mix_pre, norm_mix_post, norm_ffn_pre, norm_ffn_post, w_up, w_down,
           attn_w_qkv, attn_w_o, mlstm_w_in, mlstm_b_gates, mlstm_norm_h, mlstm_w_out):
    bsz, seq, d = x.shape
    m = bsz * seq
    heads = ATT_HEADS
    mh = MLSTM_HEADS
    dv = d // mh
    dqk = dv // 2

    q, k, km, vt = _qkv_proj(x, norm_mix_pre[0], attn_w_qkv[0], heads=heads, blk=MOBA_BLOCK,
                             tm=4 * MOBA_BLOCK)
    slopes = jnp.exp2(-8.0 * jnp.arange(1, heads + 1, dtype=F32) / heads)
    att = _moba_attention(q, k, vt, km, slopes, heads=heads, blk=MOBA_BLOCK, top=MOBA_TOPK, cb=2)
    h = _proj_mlp(att.reshape(m, d), x.reshape(m, d), attn_w_o[0], norm_mix_post[0],
                  norm_ffn_pre[0], w_up, w_down, norm_ffn_post[0], layer=0, tm=512, tf=1024, wc=512)

    nq = mh * dqk
    w_in = mlstm_w_in[0, :, :2 * nq + 2 * d].astype(BF16)
    wg = jnp.pad(mlstm_w_in[0, :, 2 * nq + 2 * d:], ((0, 0), (0, LANES - 2 * mh))).astype(BF16)
    bg = jnp.pad(mlstm_b_gates[0], (0, LANES - 2 * mh)).reshape(1, LANES)
    qt, kmm, vmt, opre, gates = _mlstm_in_proj(h, norm_mix_pre[1], w_in, wg, bg,
                                               bsz=bsz, heads=mh, dqk=dqk, tm=1024)
    y = _mlstm_scan(qt, kmm.reshape(bsz, seq, nq), vmt, opre.reshape(bsz, seq, d),
                    gates.reshape(bsz, seq, LANES), mlstm_norm_h[0], heads=mh, chunk=256)
    h = _proj_mlp(y.reshape(m, d), h, mlstm_w_out[0], norm_mix_post[1],
                  norm_ffn_pre[1], w_up, w_down, norm_ffn_post[1], layer=1, tm=512, tf=1024, wc=512)
    return h.reshape(bsz, seq, d)
```

```python
import functools

import jax
import jax.numpy as jnp
from jax import lax
from jax.experimental import pallas as pl
from jax.experimental.pallas import tpu as pltpu

F32 = jnp.float32
BF16 = jnp.bfloat16

RMS_EPS = 1e-6
ATT_HEADS = 8
MOBA_BLOCK = 256
MOBA_TOPK = 3
MLSTM_HEADS = 8
GATE_SOFTCAP = 15.0

LANES = 128
VMEM_LIMIT = 56 * 1024 * 1024
NEG_BIG = -1e30
POS_BIG = 1e30
LOG2E = 1.4426950408889634
BF16_ROWS = 16
MOBA_DEPTH = 17
MOBA_ROUNDS = 4

_NT = (((1,), (1,)), ((), ()))


def _params(*sem):
    return pltpu.CompilerParams(dimension_semantics=sem, vmem_limit_bytes=VMEM_LIMIT)


def _rms_scale(x, gain):
    ms = jnp.mean(x * x, axis=-1, keepdims=True)
    return x * lax.rsqrt(ms + RMS_EPS) * gain


def _qkv_kernel(x_ref, g_ref, w_ref, q_ref, k_ref, km_ref, vt_ref, wq_ref, wk_ref, wvt_ref,
                *, scale, blk, heads):
    d = x_ref.shape[1]

    @pl.when(pl.program_id(0) == 0)
    def _():
        wq_ref[...] = w_ref[:, :d].astype(BF16)
        wk_ref[...] = w_ref[:, d:2 * d].astype(BF16)
        wvt_ref[...] = w_ref[:, 2 * d:].T.astype(BF16)

    xn = _rms_scale(x_ref[...], g_ref[...]).astype(BF16)
    tm = xn.shape[0]
    dh = d // heads
    q = jnp.dot(xn, wq_ref[...], preferred_element_type=F32)
    q_ref[...] = (q * scale).astype(BF16)
    k = jnp.dot(xn, wk_ref[...], preferred_element_type=F32)
    k_ref[...] = k.astype(BF16)
    for c in range(tm // blk):
        km_ref[c] = jnp.mean(k[c * blk:(c + 1) * blk], axis=0, keepdims=True)
    vt = lax.dot_general(wvt_ref[...], xn, _NT, preferred_element_type=F32)
    ones_row = (lax.broadcasted_iota(jnp.int32, (BF16_ROWS, blk), 0) == 0).astype(BF16)
    for h in range(heads):
        for c in range(tm // blk):
            vt_ref[0, h, c, :dh, :] = vt[h * dh:(h + 1) * dh, c * blk:(c + 1) * blk].astype(BF16)
            vt_ref[0, h, c, dh:, :] = ones_row


def _qkv_proj(x, gain, w_qkv, *, heads, blk, tm):
    bsz, seq, d = x.shape
    dh = d // heads
    nblk = seq // blk
    m = bsz * seq
    cpt = tm // blk
    tiles_per_b = seq // tm
    kern = functools.partial(_qkv_kernel, scale=dh ** -0.5 * LOG2E, blk=blk, heads=heads)
    const = lambda i: (0, 0)
    q, k, km, vt = pl.pallas_call(
        kern,
        grid=(m // tm,),
        in_specs=[
            pl.BlockSpec((tm, d), lambda i: (i, 0)),
            pl.BlockSpec((1, d), const),
            pl.BlockSpec((d, 3 * d), const, pipeline_mode=pl.Buffered(1)),
        ],
        out_specs=[
            pl.BlockSpec((tm, d), lambda i: (i, 0)),
            pl.BlockSpec((tm, d), lambda i: (i, 0)),
            pl.BlockSpec((cpt, 1, d), lambda i: (i, 0, 0)),
            pl.BlockSpec((1, heads, cpt, dh + BF16_ROWS, blk),
                         lambda i: (i // tiles_per_b, 0, i % tiles_per_b, 0, 0)),
        ],
        out_shape=[
            jax.ShapeDtypeStruct((m, d), BF16),
            jax.ShapeDtypeStruct((m, d), BF16),
            jax.ShapeDtypeStruct((m // blk, 1, d), F32),
            jax.ShapeDtypeStruct((bsz, heads, nblk, dh + BF16_ROWS, blk), BF16),
        ],
        scratch_shapes=[pltpu.VMEM((d, d), BF16)] * 3,
        compiler_params=_params("arbitrary"),
        name="qkv_proj",
    )(x.reshape(m, d), gain.reshape(1, d), w_qkv)
    return (q.reshape(bsz, seq, d), k.reshape(bsz, seq, d),
            km.reshape(bsz, nblk, d), vt)


def _moba_items(nblk, cb):
    tiles, chunks = [], []
    for i in range(nblk):
        for c in range(-(-(i + 1) // cb)):
            tiles.append(i)
            chunks.append(c)
    while len(tiles) % (MOBA_DEPTH * MOBA_ROUNDS):
        assert nblk >= 2 * cb
        tiles.append(0)
        chunks.append(1)
    n_items = len(tiles)
    tiles += [0] * (MOBA_DEPTH - 1)
    chunks += [0] * (MOBA_DEPTH - 1)
    return n_items, tiles, chunks


def _moba_kernel(slope_ref, tile_ref, chunk_ref, q_ref, k_ref, vt_ref, km_ref, o_ref,
                 sel_ref, bias_ref, m_ref, acc_ref, cmax_ref, *s_refs,
                 blk, nblk, top, cb, n_items, qc, dh):
    slope = slope_ref[pl.program_id(1)] * LOG2E

    kr = lax.broadcasted_iota(jnp.int32, (blk, blk), 0)
    qq = lax.broadcasted_iota(jnp.int32, (blk, blk), 1)
    base = kr.astype(F32) * slope
    bias_ref[0] = base
    bias_ref[1] = jnp.where(kr <= qq, base, NEG_BIG)

    m_ref[...] = jnp.full(m_ref.shape, NEG_BIG, F32)
    acc_ref[...] = jnp.zeros(acc_ref.shape, F32)

    km = km_ref[0]
    km_hi = km.astype(BF16)
    km_lo = (km - km_hi.astype(F32)).astype(BF16)
    row = lax.broadcasted_iota(jnp.int32, (nblk, qc), 0)
    rowf = row.astype(F32)
    tiles_per_qc = qc // blk

    def select(u, carry):
        q0 = pl.multiple_of(u * qc, qc)
        qs = q_ref[0, pl.ds(q0, qc), :]
        gate = (lax.dot_general(km_hi, qs, _NT, preferred_element_type=F32)
                + lax.dot_general(km_lo, qs, _NT, preferred_element_type=F32))
        qblk = u * tiles_per_qc + lax.broadcasted_iota(jnp.int32, (nblk, qc), 1) // blk
        past = row < qblk
        g = jnp.where(past, gate, -jnp.inf)
        picked = jnp.zeros((nblk, qc), F32)
        for _ in range(top):
            mx = jnp.max(g, axis=0, keepdims=True)
            first = jnp.min(jnp.where(g == mx, rowf, float(nblk)), axis=0, keepdims=True)
            hit = rowf == first
            picked = jnp.where(hit, 1.0, picked)
            g = jnp.where(hit, -jnp.inf, g)
        selv = jnp.where(((picked > 0.5) & past) | (row == qblk), 1.0, 0.0)
        for t in range(tiles_per_qc):
            sel_ref[u * tiles_per_qc + t] = selv[:, t * blk:(t + 1) * blk]
        return carry

    lax.fori_loop(0, (nblk * blk) // qc, select, 0, unroll=True)

    def scores(item, slot):
        i = tile_ref[item]
        c = chunk_ref[item]
        qi = q_ref[0, pl.ds(pl.multiple_of(i * blk, blk), blk), :]
        for jb in range(cb):
            j = c * cb + jb
            kb = k_ref[0, pl.ds(pl.multiple_of(j * blk, blk), blk), :]
            sb = (lax.dot_general(kb, qi, _NT, preferred_element_type=F32)
                  + bias_ref[jnp.where(j == i, 1, 0)])
            s_refs[slot][jb * blk:(jb + 1) * blk, :] = sb
            cmax_ref[slot, jb] = jnp.max(sb, axis=0, keepdims=True)

    def softmax_pv(item, slot):
        i = tile_ref[item]
        c = chunk_ref[item]
        m_old = m_ref[i]
        sels, cjs = [], []
        m_new = m_old
        for jb in range(cb):
            j = c * cb + jb
            cj = ((j - i) * blk).astype(F32) * slope
            sel = sel_ref[i, pl.ds(j, 1), :] > 0.5
            m_new = jnp.maximum(m_new, jnp.where(sel, cmax_ref[slot, jb] + cj, NEG_BIG))
            sels.append(sel)
            cjs.append(cj)
        alpha = jnp.exp2(m_old - m_new)
        pv = None
        for jb in range(cb):
            shift = jnp.where(sels[jb], m_new - cjs[jb], POS_BIG)
            p = jnp.exp2(s_refs[slot][jb * blk:(jb + 1) * blk, :] - shift)
            d = jnp.dot(vt_ref[0, 0, c * cb + jb], p.astype(BF16), preferred_element_type=F32)
            pv = d if pv is None else pv + d
        acc_ref[i] = alpha * acc_ref[i] + pv
        m_ref[i] = m_new

    for slot in range(MOBA_DEPTH - 1):
        scores(slot, slot)

    def group(kk, carry):
        for u in range(MOBA_DEPTH * MOBA_ROUNDS):
            t = MOBA_DEPTH * MOBA_ROUNDS * kk + u
            softmax_pv(t, u % MOBA_DEPTH)
            scores(t + MOBA_DEPTH - 1, (u + MOBA_DEPTH - 1) % MOBA_DEPTH)
        return carry

    lax.fori_loop(0, n_items // (MOBA_DEPTH * MOBA_ROUNDS), group, 0)

    def finish(i, carry):
        acc = acc_ref[i]
        o = acc[:dh] * (1.0 / acc[dh:dh + 1])
        o_ref[0, pl.ds(pl.multiple_of(i * blk, blk), blk), :] = o.T.astype(BF16)
        return carry

    lax.fori_loop(0, nblk, finish, 0, unroll=True)


def _moba_attention(q, k, vt, km, slopes, *, heads, blk, top, cb):
    bsz, seq, d = q.shape
    dh = d // heads
    nblk = seq // blk
    assert nblk % cb == 0
    qc = min(seq, 4 * blk)
    n_items, tiles, chunks = _moba_items(nblk, cb)
    kern = functools.partial(_moba_kernel, blk=blk, nblk=nblk, top=min(top, nblk), cb=cb,
                             n_items=n_items, qc=qc, dh=dh)
    smem = pl.BlockSpec(memory_space=pltpu.SMEM)
    return pl.pallas_call(
        kern,
        grid=(bsz, heads),
        in_specs=[
            smem, smem, smem,
            pl.BlockSpec((1, seq, dh), lambda b, h: (b, 0, h)),
            pl.BlockSpec((1, seq, dh), lambda b, h: (b, 0, h)),
            pl.BlockSpec((1, 1, nblk, dh + BF16_ROWS, blk), lambda b, h: (b, h, 0, 0, 0)),
            pl.BlockSpec((1, nblk, dh), lambda b, h: (b, 0, h)),
        ],
        out_specs=pl.BlockSpec((1, seq, dh), lambda b, h: (b, 0, h)),
        out_shape=jax.ShapeDtypeStruct((bsz, seq, d), BF16),
        scratch_shapes=[
            pltpu.VMEM((nblk, nblk, blk), F32),
            pltpu.VMEM((2, blk, blk), F32),
            pltpu.VMEM((nblk, 1, blk), F32),
            pltpu.VMEM((nblk, dh + BF16_ROWS, blk), F32),
            pltpu.VMEM((MOBA_DEPTH, cb, 1, blk), F32),
        ] + [pltpu.VMEM((cb * blk, blk), F32)] * MOBA_DEPTH,
        compiler_params=_params("parallel", "parallel"),
        name="moba_attention",
    )(slopes, jnp.asarray(tiles, jnp.int32), jnp.asarray(chunks, jnp.int32), q, k, vt, km)


def _proj_mlp_kernel(a_ref, h_ref, wo_ref, gm_ref, g1_ref, wu_ref, wd_ref, g2_ref, o_ref,
                     wu_s, wd_s, h1_s, xn_s, acc_s, *, tf, ncast):
    s = pl.program_id(0)
    wc = wu_ref.shape[1]

    def mixer_residual():
        u = jnp.dot(a_ref[...], wo_ref[...].astype(BF16), preferred_element_type=F32)
        return h_ref[...] + _rms_scale(u, gm_ref[...])

    def mlp_chunk(xn, wu, wd):
        t = jnp.maximum(jnp.dot(xn, wu, preferred_element_type=F32), 0.0)
        return jnp.dot((t * t).astype(BF16), wd, preferred_element_type=F32)

    @pl.when(s == 0)
    def _():
        h1 = mixer_residual()
        h1_s[...] = h1
        xn_s[...] = _rms_scale(h1, g1_ref[...]).astype(BF16)
        acc_s[...] = jnp.zeros_like(acc_s)

    @pl.when(s < ncast)
    def _():
        c0 = pl.multiple_of(s * wc, wc)
        wu = wu_ref[...].astype(BF16)
        wd = wd_ref[...].astype(BF16)
        wu_s[:, pl.ds(c0, wc)] = wu
        wd_s[pl.ds(c0, wc), :] = wd
        acc_s[...] += mlp_chunk(xn_s[...], wu, wd)

    @pl.when(s == ncast - 1)
    def _():
        o_ref[...] = h1_s[...] + _rms_scale(acc_s[...], g2_ref[...])

    @pl.when(s >= ncast)
    def _():
        h1 = mixer_residual()
        xn = _rms_scale(h1, g1_ref[...]).astype(BF16)
        ff = wu_s.shape[1]
        acc = None
        for c in range(ff // tf):
            d = mlp_chunk(xn, wu_s[:, c * tf:(c + 1) * tf], wd_s[c * tf:(c + 1) * tf, :])
            acc = d if acc is None else acc + d
        o_ref[...] = h1 + _rms_scale(acc, g2_ref[...])


def _proj_mlp(a, h, w_o, g_mix, g_pre, w_up, w_down, g_post, *, layer, tm, tf, wc):
    m, d = h.shape
    kdim = a.shape[1]
    ff = w_up.shape[2]
    ncast = ff // wc
    row = lambda s: (jnp.maximum(s - (ncast - 1), 0), 0)
    const = lambda s: (0, 0)
    return pl.pallas_call(
        functools.partial(_proj_mlp_kernel, tf=tf, ncast=ncast),
        grid=(ncast - 1 + m // tm,),
        in_specs=[
            pl.BlockSpec((tm, kdim), row),
            pl.BlockSpec((tm, d), row),
            pl.BlockSpec((kdim, d), const, pipeline_mode=pl.Buffered(1)),
            pl.BlockSpec((1, d), const),
            pl.BlockSpec((1, d), const),
            pl.BlockSpec((None, d, wc), lambda s: (layer, 0, jnp.minimum(s, ncast - 1))),
            pl.BlockSpec((None, wc, d), lambda s: (layer, jnp.minimum(s, ncast - 1), 0)),
            pl.BlockSpec((1, d), const),
        ],
        out_specs=pl.BlockSpec((tm, d), row),
        out_shape=jax.ShapeDtypeStruct((m, d), F32),
        scratch_shapes=[
            pltpu.VMEM((d, ff), BF16),
            pltpu.VMEM((ff, d), BF16),
            pltpu.VMEM((tm, d), F32),
            pltpu.VMEM((tm, d), BF16),
            pltpu.VMEM((tm, d), F32),
        ],
        compiler_params=_params("arbitrary"),
        name="proj_mlp",
    )(a, h, w_o, g_mix.reshape(1, d), g_pre.reshape(1, d), w_up, w_down, g_post.reshape(1, d))


def _mlstm_in_kernel(x_ref, g_ref, w_ref, wg_ref, bg_ref,
                     qt_ref, k_ref, vt_ref, op_ref, gates_ref,
                     wqt_ref, wvt_ref, *, heads, kscale):
    nq = wqt_ref.shape[0]
    d = wvt_ref.shape[0]

    @pl.when(pl.program_id(0) == 0)
    def _():
        wqt_ref[...] = w_ref[:, :nq].astype(F32).T.astype(BF16)
        wvt_ref[...] = w_ref[:, 2 * nq:2 * nq + d].astype(F32).T.astype(BF16)

    xn = _rms_scale(x_ref[...], g_ref[...]).astype(BF16)
    qt_ref[0] = lax.dot_general(wqt_ref[...], xn, _NT, preferred_element_type=F32).astype(BF16)
    k_ref[...] = (jnp.dot(xn, w_ref[:, nq:2 * nq], preferred_element_type=F32) * kscale).astype(BF16)
    vt_ref[0] = lax.dot_general(wvt_ref[...], xn, _NT, preferred_element_type=F32).astype(BF16)
    op_ref[...] = jnp.dot(xn, w_ref[:, 2 * nq + d:2 * nq + 2 * d],
                          preferred_element_type=F32).astype(BF16)
    z = jnp.dot(xn, wg_ref[...], preferred_element_type=F32) + bg_ref[...]
    z = GATE_SOFTCAP * jnp.tanh(z * (1.0 / GATE_SOFTCAP))
    log_f = jnp.minimum(z, 0.0) - jnp.log1p(jnp.exp(-jnp.abs(z)))
    lane = lax.broadcasted_iota(jnp.int32, z.shape, 1)
    gates_ref[...] = jnp.where(lane < heads, z, log_f) * LOG2E


def _mlstm_in_proj(x, gain, w_in, wg, bg, *, bsz, heads, dqk, tm):
    m, d = x.shape
    seq = m // bsz
    nq = heads * dqk
    dv_all = d
    tiles_per_b = seq // tm
    kern = functools.partial(_mlstm_in_kernel, heads=heads, kscale=dqk ** -0.5)
    const = lambda i: (0, 0)
    row = lambda i: (i, 0)
    tcol = lambda i: (i // tiles_per_b, 0, i % tiles_per_b)
    return pl.pallas_call(
        kern,
        grid=(m // tm,),
        in_specs=[
            pl.BlockSpec((tm, d), row),
            pl.BlockSpec((1, d), const),
            pl.BlockSpec(w_in.shape, const, pipeline_mode=pl.Buffered(1)),
            pl.BlockSpec((d, LANES), const),
            pl.BlockSpec((1, LANES), const),
        ],
        out_specs=[
            pl.BlockSpec((1, nq, tm), tcol),
            pl.BlockSpec((tm, nq), row),
            pl.BlockSpec((1, dv_all, tm), tcol),
            pl.BlockSpec((tm, dv_all), row),
            pl.BlockSpec((tm, LANES), row),
        ],
        out_shape=[
            jax.ShapeDtypeStruct((bsz, nq, seq), BF16),
            jax.ShapeDtypeStruct((m, nq), BF16),
            jax.ShapeDtypeStruct((bsz, dv_all, seq), BF16),
            jax.ShapeDtypeStruct((m, dv_all), BF16),
            jax.ShapeDtypeStruct((m, LANES), F32),
        ],
        scratch_shapes=[
            pltpu.VMEM((nq, d), BF16),
            pltpu.VMEM((dv_all, d), BF16),
        ],
        compiler_params=_params("arbitrary"),
        name="mlstm_in_proj",
    )(x, gain.reshape(1, d), w_in, wg, bg)


def _mlstm_kernel(qt_ref, k_ref, vt_ref, op_ref, g_ref, gnext_ref, nh_ref, y_ref,
                  c_ref, m_ref, rows_ref, ucol_ref, *, heads, dqk, dv, chunk):
    L = chunk
    ss = lax.broadcasted_iota(jnp.int32, (L, L), 0)
    tt = lax.broadcasted_iota(jnp.int32, (L, L), 1)
    causal = ss <= tt

    def gate_terms(gates):
        gates_t = gates.T
        i_rows = gates_t[0:heads]
        f_rows = gates_t[heads:2 * heads]
        tri = causal.astype(BF16)
        f_hi = f_rows.astype(BF16)
        f_r1 = f_rows - f_hi.astype(F32)
        f_mid = f_r1.astype(BF16)
        f_lo = (f_r1 - f_mid.astype(F32)).astype(BF16)
        b_rows = (jnp.dot(f_hi, tri, preferred_element_type=F32)
                  + jnp.dot(f_mid, tri, preferred_element_type=F32)
                  + jnp.dot(f_lo, tri, preferred_element_type=F32))
        rows_ref[0:heads] = i_rows
        rows_ref[heads:2 * heads] = b_rows
        ucol_ref[...] = jnp.concatenate(
            [i_rows - b_rows, jnp.zeros((LANES - heads, L), F32)], axis=0).T

    @pl.when(pl.program_id(1) == 0)
    def _():
        c_ref[...] = jnp.zeros_like(c_ref)
        m_ref[...] = jnp.zeros_like(m_ref)
        gate_terms(g_ref[0])

    ones_rows = (lax.broadcasted_iota(jnp.int32, (BF16_ROWS, L), 0) == 0).astype(BF16)
    sub = lax.broadcasted_iota(jnp.int32, (2 * dqk, 1), 0)
    lane = lax.broadcasted_iota(jnp.int32, (1, 2 * dqk), 1)

    first = lane < dqk
    group = heads // 2
    for h0 in range(0, heads, group):
        hs = range(h0, h0 + group)
        kgs, cts, s_qk, inter = [], [], [], []
        for h in hs:
            p, half = divmod(h, 2)
            qt_pair = qt_ref[0, p * 2 * dqk:(p + 1) * 2 * dqk, :]
            kg = k_ref[0, :, p * 2 * dqk:(p + 1) * 2 * dqk]
            in_head = (sub >= half * dqk) & (sub < (half + 1) * dqk)
            qt_h = jnp.where(in_head, qt_pair, jnp.zeros_like(qt_pair))
            ct = c_ref[p]
            kgs.append(kg)
            cts.append(ct)
            s_qk.append(jnp.dot(kg, qt_h, preferred_element_type=F32))
            inter.append(jnp.dot(ct.astype(BF16), qt_h, preferred_element_type=F32))

        s_ts, m_ts, w_inters, vexts = [], [], [], []
        for n, h in enumerate(hs):
            b_row = rows_ref[heads + h:heads + h + 1, :]
            m_prev = m_ref[h][:, 0:1]
            log_d = jnp.where(causal, ucol_ref[:, h:h + 1] + b_row, -jnp.inf)
            log_inter = b_row + m_prev
            m_t = jnp.maximum(log_inter, jnp.max(log_d, axis=0, keepdims=True))
            s_ts.append((s_qk[n] * jnp.exp2(log_d - m_t)).astype(BF16))
            w_inters.append(jnp.exp2(log_inter - m_t))
            m_ts.append(m_t)
            vexts.append(jnp.concatenate([vt_ref[0, h * dv:(h + 1) * dv, :], ones_rows], axis=0))

        for n, h in enumerate(hs):
            num = (w_inters[n] * inter[n]
                   + jnp.dot(vexts[n], s_ts[n], preferred_element_type=F32))
            den = num[dv:dv + 1]
            hout = num[:dv] * (1.0 / jnp.maximum(jnp.abs(den), jnp.exp2(-m_ts[n])))
            hn = hout * lax.rsqrt(jnp.mean(hout * hout, axis=0, keepdims=True) + RMS_EPS)
            og = 0.5 * jnp.tanh(0.5 * op_ref[0, :, h * dv:(h + 1) * dv].astype(F32)) + 0.5
            y_ref[0, :, h * dv:(h + 1) * dv] = (
                og * (hn.T * nh_ref[:, h * dv:(h + 1) * dv])).astype(BF16)

        upds, decays = [], []
        for n, h in enumerate(hs):
            b_row = rows_ref[heads + h:heads + h + 1, :]
            i_row = rows_ref[h:h + 1, :]
            m_prev = m_ref[h][:, 0:1]
            b_end = b_row[:, L - 1:L]
            log_w = b_end - b_row + i_row
            m_new = jnp.maximum(b_end + m_prev, jnp.max(log_w, axis=1, keepdims=True))
            decays.append(jnp.exp2(b_end + m_prev - m_new))
            vw = (vexts[n].astype(F32) * jnp.exp2(log_w - m_new)).astype(BF16)
            upds.append(jnp.dot(vw, kgs[n], preferred_element_type=F32))
            m_ref[h] = jnp.broadcast_to(m_new, (1, LANES))
        for n in range(0, group, 2):
            c_ref[(h0 + n) // 2] = (jnp.where(first, decays[n], decays[n + 1]) * cts[n]
                                    + jnp.where(first, upds[n], upds[n + 1]))

    gate_terms(gnext_ref[0])


def _mlstm_scan(qt, k, vt, opre, gates, norm_h, *, heads, chunk):
    bsz, dv_all, seq = vt.shape
    dv = dv_all // heads
    dqk = qt.shape[1] // heads
    kern = functools.partial(_mlstm_kernel, heads=heads, dqk=dqk, dv=dv, chunk=chunk)
    nchunk = seq // chunk
    rows = lambda b, c: (b, c, 0)
    cols = lambda b, c: (b, 0, c)
    return pl.pallas_call(
        kern,
        grid=(bsz, seq // chunk),
        in_specs=[
            pl.BlockSpec((1, heads * dqk, chunk), cols),
            pl.BlockSpec((1, chunk, heads * dqk), rows),
            pl.BlockSpec((1, dv_all, chunk), cols),
            pl.BlockSpec((1, chunk, dv_all), rows),
            pl.BlockSpec((1, chunk, LANES), rows),
            pl.BlockSpec((1, chunk, LANES), lambda b, c: (b, jnp.minimum(c + 1, nchunk - 1), 0)),
            pl.BlockSpec((1, dv_all), lambda b, c: (0, 0)),
        ],
        out_specs=pl.BlockSpec((1, chunk, dv_all), rows),
        out_shape=jax.ShapeDtypeStruct((bsz, seq, dv_all), BF16),
        scratch_shapes=[
            pltpu.VMEM((heads // 2, dv + BF16_ROWS, 2 * dqk), F32),
            pltpu.VMEM((heads, 1, LANES), F32),
            pltpu.VMEM((2 * heads, chunk), F32),
            pltpu.VMEM((chunk, LANES), F32),
        ],
        compiler_params=_params("parallel", "arbitrary"),
        name="mlstm_scan",
    )(qt, k, vt, opre, gates, gates, norm_h.reshape(1, dv_all))


def kernel(x, norm_mix_pre, norm_mix_post, norm_ffn_pre, norm_ffn_post, w_up, w_down,
           attn_w_qkv, attn_w_o, mlstm_w_in, mlstm_b_gates, mlstm_norm_h, mlstm_w_out):
    bsz, seq, d = x.shape
    m = bsz * seq
    heads = ATT_HEADS
    mh = MLSTM_HEADS
    dv = d // mh
    dqk = dv // 2

    q, k, km, vt = _qkv_proj(x, norm_mix_pre[0], attn_w_qkv[0], heads=heads, blk=MOBA_BLOCK,
                             tm=4 * MOBA_BLOCK)
    slopes = jnp.exp2(-8.0 * jnp.arange(1, heads + 1, dtype=F32) / heads)
    att = _moba_attention(q, k, vt, km, slopes, heads=heads, blk=MOBA_BLOCK, top=MOBA_TOPK, cb=2)
    h = _proj_mlp(att.reshape(m, d), x.reshape(m, d), attn_w_o[0], norm_mix_post[0],
                  norm_ffn_pre[0], w_up, w_down, norm_ffn_post[0], layer=0, tm=512, tf=1024, wc=512)

    nq = mh * dqk
    w_in = mlstm_w_in[0, :, :2 * nq + 2 * d].astype(BF16)
    wg = jnp.pad(mlstm_w_in[0, :, 2 * nq + 2 * d:], ((0, 0), (0, LANES - 2 * mh))).astype(BF16)
    bg = jnp.pad(mlstm_b_gates[0], (0, LANES - 2 * mh)).reshape(1, LANES)
    qt, kmm, vmt, opre, gates = _mlstm_in_proj(h, norm_mix_pre[1], w_in, wg, bg,
                                               bsz=bsz, heads=mh, dqk=dqk, tm=1024)
    y = _mlstm_scan(qt, kmm.reshape(bsz, seq, nq), vmt, opre.reshape(bsz, seq, d),
                    gates.reshape(bsz, seq, LANES), mlstm_norm_h[0], heads=mh, chunk=256)
    h = _proj_mlp(y.reshape(m, d), h, mlstm_w_out[0], norm_mix_post[1],
                  norm_ffn_pre[1], w_up, w_down, norm_ffn_post[1], layer=1, tm=512, tf=1024, wc=512)
    return h.reshape(bsz, seq, d)
```

```python
import functools

import jax
import jax.numpy as jnp
from jax import lax
from jax.experimental import pallas as pl
from jax.experimental.pallas import tpu as pltpu

F32 = jnp.float32
BF16 = jnp.bfloat16

RMS_EPS = 1e-6
ATT_HEADS = 8
MOBA_BLOCK = 256
MOBA_TOPK = 3
MLSTM_HEADS = 8
GATE_SOFTCAP = 15.0

LANES = 128
VMEM_LIMIT = 56 * 1024 * 1024
NEG_BIG = -1e30
POS_BIG = 1e30
LOG2E = 1.4426950408889634
BF16_ROWS = 16
MOBA_DEPTH = 17
MOBA_ROUNDS = 4

_NT = (((1,), (1,)), ((), ()))


def _params(*sem):
    return pltpu.CompilerParams(dimension_semantics=sem, vmem_limit_bytes=VMEM_LIMIT)


def _rms_scale(x, gain):
    ms = jnp.mean(x * x, axis=-1, keepdims=True)
    return x * lax.rsqrt(ms + RMS_EPS) * gain


def _qkv_kernel(x_ref, g_ref, w_ref, q_ref, k_ref, km_ref, vt_ref, wq_ref, wk_ref, wvt_ref,
                *, scale, blk, heads):
    d = x_ref.shape[1]

    @pl.when(pl.program_id(0) == 0)
    def _():
        wq_ref[...] = w_ref[:, :d].astype(BF16)
        wk_ref[...] = w_ref[:, d:2 * d].astype(BF16)
        wvt_ref[...] = w_ref[:, 2 * d:].T.astype(BF16)

    xn = _rms_scale(x_ref[...], g_ref[...]).astype(BF16)
    tm = xn.shape[0]
    dh = d // heads
    q = jnp.dot(xn, wq_ref[...], preferred_element_type=F32)
    q_ref[...] = (q * scale).astype(BF16)
    k = jnp.dot(xn, wk_ref[...], preferred_element_type=F32)
    k_ref[...] = k.astype(BF16)
    for c in range(tm // blk):
        km_ref[c] = jnp.mean(k[c * blk:(c + 1) * blk], axis=0, keepdims=True)
    vt = lax.dot_general(wvt_ref[...], xn, _NT, preferred_element_type=F32)
    ones_row = (lax.broadcasted_iota(jnp.int32, (BF16_ROWS, blk), 0) == 0).astype(BF16)
    for h in range(heads):
        for c in range(tm // blk):
            vt_ref[0, h, c, :dh, :] = vt[h * dh:(h + 1) * dh, c * blk:(c + 1) * blk].astype(BF16)
            vt_ref[0, h, c, dh:, :] = ones_row


def _qkv_proj(x, gain, w_qkv, *, heads, blk, tm):
    bsz, seq, d = x.shape
    dh = d // heads
    nblk = seq // blk
    m = bsz * seq
    cpt = tm // blk
    tiles_per_b = seq // tm
    kern = functools.partial(_qkv_kernel, scale=dh ** -0.5 * LOG2E, blk=blk, heads=heads)
    const = lambda i: (0, 0)
    q, k, km, vt = pl.pallas_call(
        kern,
        grid=(m // tm,),
        in_specs=[
            pl.BlockSpec((tm, d), lambda i: (i, 0)),
            pl.BlockSpec((1, d), const),
            pl.BlockSpec((d, 3 * d), const, pipeline_mode=pl.Buffered(1)),
        ],
        out_specs=[
            pl.BlockSpec((tm, d), lambda i: (i, 0)),
            pl.BlockSpec((tm, d), lambda i: (i, 0)),
            pl.BlockSpec((cpt, 1, d), lambda i: (i, 0, 0)),
            pl.BlockSpec((1, heads, cpt, dh + BF16_ROWS, blk),
                         lambda i: (i // tiles_per_b, 0, i % tiles_per_b, 0, 0)),
        ],
        out_shape=[
            jax.ShapeDtypeStruct((m, d), BF16),
            jax.ShapeDtypeStruct((m, d), BF16),
            jax.ShapeDtypeStruct((m // blk, 1, d), F32),
            jax.ShapeDtypeStruct((bsz, heads, nblk, dh + BF16_ROWS, blk), BF16),
        ],
        scratch_shapes=[pltpu.VMEM((d, d), BF16)] * 3,
        compiler_params=_params("arbitrary"),
        name="qkv_proj",
    )(x.reshape(m, d), gain.reshape(1, d), w_qkv)
    return (q.reshape(bsz, seq, d), k.reshape(bsz, seq, d),
            km.reshape(bsz, nblk, d), vt)


def _moba_items(nblk, cb):
    tiles, chunks = [], []
    for i in range(nblk):
        for c in range(-(-(i + 1) // cb)):
            tiles.append(i)
            chunks.append(c)
    while len(tiles) % (MOBA_DEPTH * MOBA_ROUNDS):
        assert nblk >= 2 * cb
        tiles.append(0)
        chunks.append(1)
    n_items = len(tiles)
    tiles += [0] * (MOBA_DEPTH - 1)
    chunks += [0] * (MOBA_DEPTH - 1)
    return n_items, tiles, chunks


def _moba_kernel(slope_ref, tile_ref, chunk_ref, q_ref, k_ref, vt_ref, km_ref, o_ref,
                 sel_ref, bias_ref, m_ref, acc_ref, cmax_ref, *s_refs,
                 blk, nblk, top, cb, n_items, qc, dh):
    slope = slope_ref[pl.program_id(1)] * LOG2E

    kr = lax.broadcasted_iota(jnp.int32, (blk, blk), 0)
    qq = lax.broadcasted_iota(jnp.int32, (blk, blk), 1)
    base = kr.astype(F32) * slope
    bias_ref[0] = base
    bias_ref[1] = jnp.where(kr <= qq, base, NEG_BIG)

    m_ref[...] = jnp.full(m_ref.shape, NEG_BIG, F32)
    acc_ref[...] = jnp.zeros(acc_ref.shape, F32)

    km = km_ref[0]
    km_hi = km.astype(BF16)
    km_lo = (km - km_hi.astype(F32)).astype(BF16)
    row = lax.broadcasted_iota(jnp.int32, (nblk, qc), 0)
    rowf = row.astype(F32)
    tiles_per_qc = qc // blk

    def select(u, carry):
        q0 = pl.multiple_of(u * qc, qc)
        qs = q_ref[0, pl.ds(q0, qc), :]
        gate = (lax.dot_general(km_hi, qs, _NT, preferred_element_type=F32)
                + lax.dot_general(km_lo, qs, _NT, preferred_element_type=F32))
        qblk = u * tiles_per_qc + lax.broadcasted_iota(jnp.int32, (nblk, qc), 1) // blk
        past = row < qblk
        g = jnp.where(past, gate, -jnp.inf)
        picked = jnp.zeros((nblk, qc), F32)
        for _ in range(top):
            mx = jnp.max(g, axis=0, keepdims=True)
            first = jnp.min(jnp.where(g == mx, rowf, float(nblk)), axis=0, keepdims=True)
            hit = rowf == first
            picked = jnp.where(hit, 1.0, picked)
            g = jnp.where(hit, -jnp.inf, g)
        selv = jnp.where(((picked > 0.5) & past) | (row == qblk), 1.0, 0.0)
        for t in range(tiles_per_qc):
            sel_ref[u * tiles_per_qc + t] = selv[:, t * blk:(t + 1) * blk]
        return carry

    lax.fori_loop(0, (nblk * blk) // qc, select, 0, unroll=True)

    def scores(item, slot):
        i = tile_ref[item]
        c = chunk_ref[item]
        qi = q_ref[0, pl.ds(pl.multiple_of(i * blk, blk), blk), :]
        for jb in range(cb):
            j = c * cb + jb
            kb = k_ref[0, pl.ds(pl.multiple_of(j * blk, blk), blk), :]
            sb = (lax.dot_general(kb, qi, _NT, preferred_element_type=F32)
                  + bias_ref[jnp.where(j == i, 1, 0)])
            s_refs[slot][jb * blk:(jb + 1) * blk, :] = sb
            cmax_ref[slot, jb] = jnp.max(sb, axis=0, keepdims=True)

    def softmax_pv(item, slot):
        i = tile_ref[item]
        c = chunk_ref[item]
        m_old = m_ref[i]
        sels, cjs = [], []
        m_new = m_old
        for jb in range(cb):
            j = c * cb + jb
            cj = ((j - i) * blk).astype(F32) * slope
            sel = sel_ref[i, pl.ds(j, 1), :] > 0.5
            m_new = jnp.maximum(m_new, jnp.where(sel, cmax_ref[slot, jb] + cj, NEG_BIG))
            sels.append(sel)
            cjs.append(cj)
        alpha = jnp.exp2(m_old - m_new)
        pv = None
        for jb in range(cb):
            shift = jnp.where(sels[jb], m_new - cjs[jb], POS_BIG)
            p = jnp.exp2(s_refs[slot][jb * blk:(jb + 1) * blk, :] - shift)
            d = jnp.dot(vt_ref[0, 0, c * cb + jb], p.astype(BF16), preferred_element_type=F32)
            pv = d if pv is None else pv + d
        acc_ref[i] = alpha * acc_ref[i] + pv
        m_ref[i] = m_new

    for slot in range(MOBA_DEPTH - 1):
        scores(slot, slot)

    def group(kk, carry):
        for u in range(MOBA_DEPTH * MOBA_ROUNDS):
            t = MOBA_DEPTH * MOBA_ROUNDS * kk + u
            softmax_pv(t, u % MOBA_DEPTH)
            scores(t + MOBA_DEPTH - 1, (u + MOBA_DEPTH - 1) % MOBA_DEPTH)
        return carry

    lax.fori_loop(0, n_items // (MOBA_DEPTH * MOBA_ROUNDS), group, 0)

    for i in range(nblk):
        acc = acc_ref[i]
        o_ref[0, :, i * blk:(i + 1) * blk] = (acc[:dh] * (1.0 / acc[dh:dh + 1])).astype(BF16)


def _moba_attention(q, k, vt, km, slopes, *, heads, blk, top, cb):
    bsz, seq, d = q.shape
    dh = d // heads
    nblk = seq // blk
    assert nblk % cb == 0
    qc = min(seq, 4 * blk)
    n_items, tiles, chunks = _moba_items(nblk, cb)
    kern = functools.partial(_moba_kernel, blk=blk, nblk=nblk, top=min(top, nblk), cb=cb,
                             n_items=n_items, qc=qc, dh=dh)
    smem = pl.BlockSpec(memory_space=pltpu.SMEM)
    return pl.pallas_call(
        kern,
        grid=(bsz, heads),
        in_specs=[
            smem, smem, smem,
            pl.BlockSpec((1, seq, dh), lambda b, h: (b, 0, h)),
            pl.BlockSpec((1, seq, dh), lambda b, h: (b, 0, h)),
            pl.BlockSpec((1, 1, nblk, dh + BF16_ROWS, blk), lambda b, h: (b, h, 0, 0, 0)),
            pl.BlockSpec((1, nblk, dh), lambda b, h: (b, 0, h)),
        ],
        out_specs=pl.BlockSpec((1, dh, seq), lambda b, h: (b, h, 0)),
        out_shape=jax.ShapeDtypeStruct((bsz, d, seq), BF16),
        scratch_shapes=[
            pltpu.VMEM((nblk, nblk, blk), F32),
            pltpu.VMEM((2, blk, blk), F32),
            pltpu.VMEM((nblk, 1, blk), F32),
            pltpu.VMEM((nblk, dh + BF16_ROWS, blk), F32),
            pltpu.VMEM((MOBA_DEPTH, cb, 1, blk), F32),
        ] + [pltpu.VMEM((cb * blk, blk), F32)] * MOBA_DEPTH,
        compiler_params=_params("parallel", "parallel"),
        name="moba_attention",
    )(slopes, jnp.asarray(tiles, jnp.int32), jnp.asarray(chunks, jnp.int32), q, k, vt, km)


def _proj_mlp_kernel(a_ref, h_ref, wo_ref, gm_ref, g1_ref, wu_ref, wd_ref, g2_ref, o_ref,
                     wu_s, wd_s, h1_s, xn_s, acc_s, *, tf, ncast, a_transposed):
    s = pl.program_id(0)
    wc = wu_ref.shape[1]

    def mixer_residual():
        wo = wo_ref[...].astype(BF16)
        if a_transposed:
            u = lax.dot_general(a_ref[0], wo, (((0,), (0,)), ((), ())), preferred_element_type=F32)
        else:
            u = jnp.dot(a_ref[...], wo, preferred_element_type=F32)
        return h_ref[...] + _rms_scale(u, gm_ref[...])

    def mlp_chunk(xn, wu, wd):
        t = jnp.maximum(jnp.dot(xn, wu, preferred_element_type=F32), 0.0)
        return jnp.dot((t * t).astype(BF16), wd, preferred_element_type=F32)

    @pl.when(s == 0)
    def _():
        h1 = mixer_residual()
        h1_s[...] = h1
        xn_s[...] = _rms_scale(h1, g1_ref[...]).astype(BF16)
        acc_s[...] = jnp.zeros_like(acc_s)

    @pl.when(s < ncast)
    def _():
        c0 = pl.multiple_of(s * wc, wc)
        wu = wu_ref[...].astype(BF16)
        wd = wd_ref[...].astype(BF16)
        wu_s[:, pl.ds(c0, wc)] = wu
        wd_s[pl.ds(c0, wc), :] = wd
        acc_s[...] += mlp_chunk(xn_s[...], wu, wd)

    @pl.when(s == ncast - 1)
    def _():
        o_ref[...] = h1_s[...] + _rms_scale(acc_s[...], g2_ref[...])

    @pl.when(s >= ncast)
    def _():
        h1 = mixer_residual()
        xn = _rms_scale(h1, g1_ref[...]).astype(BF16)
        ff = wu_s.shape[1]
        acc = None
        for c in range(ff // tf):
            d = mlp_chunk(xn, wu_s[:, c * tf:(c + 1) * tf], wd_s[c * tf:(c + 1) * tf, :])
            acc = d if acc is None else acc + d
        o_ref[...] = h1 + _rms_scale(acc, g2_ref[...])


def _proj_mlp(a, h, w_o, g_mix, g_pre, w_up, w_down, g_post, *, layer, tm, tf, wc,
              a_transposed=False):
    m, d = h.shape
    kdim = w_o.shape[0]
    ff = w_up.shape[2]
    ncast = ff // wc
    row = lambda s: (jnp.maximum(s - (ncast - 1), 0), 0)
    const = lambda s: (0, 0)
    if a_transposed:
        per_b = a.shape[2] // tm
        a_spec = pl.BlockSpec(
            (1, kdim, tm), lambda s: (row(s)[0] // per_b, 0, row(s)[0] % per_b))
    else:
        a_spec = pl.BlockSpec((tm, kdim), row)
    return pl.pallas_call(
        functools.partial(_proj_mlp_kernel, tf=tf, ncast=ncast, a_transposed=a_transposed),
        grid=(ncast - 1 + m // tm,),
        in_specs=[
            a_spec,
            pl.BlockSpec((tm, d), row),
            pl.BlockSpec((kdim, d), const, pipeline_mode=pl.Buffered(1)),
            pl.BlockSpec((1, d), const),
            pl.BlockSpec((1, d), const),
            pl.BlockSpec((None, d, wc), lambda s: (layer, 0, jnp.minimum(s, ncast - 1))),
            pl.BlockSpec((None, wc, d), lambda s: (layer, jnp.minimum(s, ncast - 1), 0)),
            pl.BlockSpec((1, d), const),
        ],
        out_specs=pl.BlockSpec((tm, d), row),
        out_shape=jax.ShapeDtypeStruct((m, d), F32),
        scratch_shapes=[
            pltpu.VMEM((d, ff), BF16),
            pltpu.VMEM((ff, d), BF16),
            pltpu.VMEM((tm, d), F32),
            pltpu.VMEM((tm, d), BF16),
            pltpu.VMEM((tm, d), F32),
        ],
        compiler_params=_params("arbitrary"),
        name="proj_mlp",
    )(a, h, w_o, g_mix.reshape(1, d), g_pre.reshape(1, d), w_up, w_down, g_post.reshape(1, d))


def _mlstm_in_kernel(x_ref, g_ref, w_ref, wg_ref, bg_ref,
                     qt_ref, k_ref, vt_ref, op_ref, gates_ref,
                     wqt_ref, wvt_ref, *, heads, kscale):
    nq = wqt_ref.shape[0]
    d = wvt_ref.shape[0]

    @pl.when(pl.program_id(0) == 0)
    def _():
        wqt_ref[...] = w_ref[:, :nq].astype(F32).T.astype(BF16)
        wvt_ref[...] = w_ref[:, 2 * nq:2 * nq + d].astype(F32).T.astype(BF16)

    xn = _rms_scale(x_ref[...], g_ref[...]).astype(BF16)
    qt_ref[0] = lax.dot_general(wqt_ref[...], xn, _NT, preferred_element_type=F32).astype(BF16)
    k_ref[...] = (jnp.dot(xn, w_ref[:, nq:2 * nq], preferred_element_type=F32) * kscale).astype(BF16)
    vt_ref[0] = lax.dot_general(wvt_ref[...], xn, _NT, preferred_element_type=F32).astype(BF16)
    op_ref[...] = jnp.dot(xn, w_ref[:, 2 * nq + d:2 * nq + 2 * d],
                          preferred_element_type=F32).astype(BF16)
    z = jnp.dot(xn, wg_ref[...], preferred_element_type=F32) + bg_ref[...]
    z = GATE_SOFTCAP * jnp.tanh(z * (1.0 / GATE_SOFTCAP))
    log_f = jnp.minimum(z, 0.0) - jnp.log1p(jnp.exp(-jnp.abs(z)))
    lane = lax.broadcasted_iota(jnp.int32, z.shape, 1)
    gates_ref[...] = jnp.where(lane < heads, z, log_f) * LOG2E


def _mlstm_in_proj(x, gain, w_in, wg, bg, *, bsz, heads, dqk, tm):
    m, d = x.shape
    seq = m // bsz
    nq = heads * dqk
    dv_all = d
    tiles_per_b = seq // tm
    kern = functools.partial(_mlstm_in_kernel, heads=heads, kscale=dqk ** -0.5)
    const = lambda i: (0, 0)
    row = lambda i: (i, 0)
    tcol = lambda i: (i // tiles_per_b, 0, i % tiles_per_b)
    return pl.pallas_call(
        kern,
        grid=(m // tm,),
        in_specs=[
            pl.BlockSpec((tm, d), row),
            pl.BlockSpec((1, d), const),
            pl.BlockSpec(w_in.shape, const, pipeline_mode=pl.Buffered(1)),
            pl.BlockSpec((d, LANES), const),
            pl.BlockSpec((1, LANES), const),
        ],
        out_specs=[
            pl.BlockSpec((1, nq, tm), tcol),
            pl.BlockSpec((tm, nq), row),
            pl.BlockSpec((1, dv_all, tm), tcol),
            pl.BlockSpec((tm, dv_all), row),
            pl.BlockSpec((tm, LANES), row),
        ],
        out_shape=[
            jax.ShapeDtypeStruct((bsz, nq, seq), BF16),
            jax.ShapeDtypeStruct((m, nq), BF16),
            jax.ShapeDtypeStruct((bsz, dv_all, seq), BF16),
            jax.ShapeDtypeStruct((m, dv_all), BF16),
            jax.ShapeDtypeStruct((m, LANES), F32),
        ],
        scratch_shapes=[
            pltpu.VMEM((nq, d), BF16),
            pltpu.VMEM((dv_all, d), BF16),
        ],
        compiler_params=_params("arbitrary"),
        name="mlstm_in_proj",
    )(x, gain.reshape(1, d), w_in, wg, bg)


def _mlstm_kernel(qt_ref, k_ref, vt_ref, op_ref, g_ref, gnext_ref, nh_ref, y_ref,
                  c_ref, m_ref, rows_ref, ucol_ref, *, heads, dqk, dv, chunk):
    L = chunk
    ss = lax.broadcasted_iota(jnp.int32, (L, L), 0)
    tt = lax.broadcasted_iota(jnp.int32, (L, L), 1)
    causal = ss <= tt

    def gate_terms(gates):
        gates_t = gates.T
        i_rows = gates_t[0:heads]
        f_rows = gates_t[heads:2 * heads]
        tri = causal.astype(BF16)
        f_hi = f_rows.astype(BF16)
        f_r1 = f_rows - f_hi.astype(F32)
        f_mid = f_r1.astype(BF16)
        f_lo = (f_r1 - f_mid.astype(F32)).astype(BF16)
        b_rows = (jnp.dot(f_hi, tri, preferred_element_type=F32)
                  + jnp.dot(f_mid, tri, preferred_element_type=F32)
                  + jnp.dot(f_lo, tri, preferred_element_type=F32))
        rows_ref[0:heads] = i_rows
        rows_ref[heads:2 * heads] = b_rows
        ucol_ref[...] = jnp.concatenate(
            [i_rows - b_rows, jnp.zeros((LANES - heads, L), F32)], axis=0).T

    @pl.when(pl.program_id(1) == 0)
    def _():
        c_ref[...] = jnp.zeros_like(c_ref)
        m_ref[...] = jnp.zeros_like(m_ref)
        gate_terms(g_ref[0])

    ones_rows = (lax.broadcasted_iota(jnp.int32, (BF16_ROWS, L), 0) == 0).astype(BF16)
    sub = lax.broadcasted_iota(jnp.int32, (2 * dqk, 1), 0)
    lane = lax.broadcasted_iota(jnp.int32, (1, 2 * dqk), 1)

    first = lane < dqk
    group = heads // 2
    for h0 in range(0, heads, group):
        hs = range(h0, h0 + group)
        kgs, cts, s_qk, inter = [], [], [], []
        for h in hs:
            p, half = divmod(h, 2)
            qt_pair = qt_ref[0, p * 2 * dqk:(p + 1) * 2 * dqk, :]
            kg = k_ref[0, :, p * 2 * dqk:(p + 1) * 2 * dqk]
            in_head = (sub >= half * dqk) & (sub < (half + 1) * dqk)
            qt_h = jnp.where(in_head, qt_pair, jnp.zeros_like(qt_pair))
            ct = c_ref[p]
            kgs.append(kg)
            cts.append(ct)
            s_qk.append(jnp.dot(kg, qt_h, preferred_element_type=F32))
            inter.append(jnp.dot(ct.astype(BF16), qt_h, preferred_element_type=F32))

        s_ts, m_ts, w_inters, vexts = [], [], [], []
        for n, h in enumerate(hs):
            b_row = rows_ref[heads + h:heads + h + 1, :]
            m_prev = m_ref[h][:, 0:1]
            log_d = jnp.where(causal, ucol_ref[:, h:h + 1] + b_row, -jnp.inf)
            log_inter = b_row + m_prev
            m_t = jnp.maximum(log_inter, jnp.max(log_d, axis=0, keepdims=True))
            s_ts.append((s_qk[n] * jnp.exp2(log_d - m_t)).astype(BF16))
            w_inters.append(jnp.exp2(log_inter - m_t))
            m_ts.append(m_t)
            vexts.append(jnp.concatenate([vt_ref[0, h * dv:(h + 1) * dv, :], ones_rows], axis=0))

        for n, h in enumerate(hs):
            num = (w_inters[n] * inter[n]
                   + jnp.dot(vexts[n], s_ts[n], preferred_element_type=F32))
            den = num[dv:dv + 1]
            hout = num[:dv] * (1.0 / jnp.maximum(jnp.abs(den), jnp.exp2(-m_ts[n])))
            hn = hout * lax.rsqrt(jnp.mean(hout * hout, axis=0, keepdims=True) + RMS_EPS)
            og = 0.5 * jnp.tanh(0.5 * op_ref[0, :, h * dv:(h + 1) * dv].astype(F32)) + 0.5
            y_ref[0, :, h * dv:(h + 1) * dv] = (
                og * (hn.T * nh_ref[:, h * dv:(h + 1) * dv])).astype(BF16)

        upds, decays = [], []
        for n, h in enumerate(hs):
            b_row = rows_ref[heads + h:heads + h + 1, :]
            i_row = rows_ref[h:h + 1, :]
            m_prev = m_ref[h][:, 0:1]
            b_end = b_row[:, L - 1:L]
            log_w = b_end - b_row + i_row
            m_new = jnp.maximum(b_end + m_prev, jnp.max(log_w, axis=1, keepdims=True))
            decays.append(jnp.exp2(b_end + m_prev - m_new))
            vw = (vexts[n].astype(F32) * jnp.exp2(log_w - m_new)).astype(BF16)
            upds.append(jnp.dot(vw, kgs[n], preferred_element_type=F32))
            m_ref[h] = jnp.broadcast_to(m_new, (1, LANES))
        for n in range(0, group, 2):
            c_ref[(h0 + n) // 2] = (jnp.where(first, decays[n], decays[n + 1]) * cts[n]
                                    + jnp.where(first, upds[n], upds[n + 1]))

    gate_terms(gnext_ref[0])


def _mlstm_scan(qt, k, vt, opre, gates, norm_h, *, heads, chunk):
    bsz, dv_all, seq = vt.shape
    dv = dv_all // heads
    dqk = qt.shape[1] // heads
    kern = functools.partial(_mlstm_kernel, heads=heads, dqk=dqk, dv=dv, chunk=chunk)
    nchunk = seq // chunk
    rows = lambda b, c: (b, c, 0)
    cols = lambda b, c: (b, 0, c)
    return pl.pallas_call(
        kern,
        grid=(bsz, seq // chunk),
        in_specs=[
            pl.BlockSpec((1, heads * dqk, chunk), cols),
            pl.BlockSpec((1, chunk, heads * dqk), rows),
            pl.BlockSpec((1, dv_all, chunk), cols),
            pl.BlockSpec((1, chunk, dv_all), rows),
            pl.BlockSpec((1, chunk, LANES), rows),
            pl.BlockSpec((1, chunk, LANES), lambda b, c: (b, jnp.minimum(c + 1, nchunk - 1), 0)),
            pl.BlockSpec((1, dv_all), lambda b, c: (0, 0)),
        ],
        out_specs=pl.BlockSpec((1, chunk, dv_all), rows),
        out_shape=jax.ShapeDtypeStruct((bsz, seq, dv_all), BF16),
        scratch_shapes=[
            pltpu.VMEM((heads // 2, dv + BF16_ROWS, 2 * dqk), F32),
            pltpu.VMEM((heads, 1, LANES), F32),
            pltpu.VMEM((2 * heads, chunk), F32),
            pltpu.VMEM((chunk, LANES), F32),
        ],
        compiler_params=_params("parallel", "arbitrary"),
        name="mlstm_scan",
    )(qt, k, vt, opre, gates, gates, norm_h.reshape(1, dv_all))


def kernel(x, norm_mix_pre, norm_mix_post, norm_ffn_pre, norm_ffn_post, w_up, w_down,
           attn_w_qkv, attn_w_o, mlstm_w_in, mlstm_b_gates, mlstm_norm_h, mlstm_w_out):
    bsz, seq, d = x.shape
    m = bsz * seq
    heads = ATT_HEADS
    mh = MLSTM_HEADS
    dv = d // mh
    dqk = dv // 2

    q, k, km, vt = _qkv_proj(x, norm_mix_pre[0], attn_w_qkv[0], heads=heads, blk=MOBA_BLOCK,
                             tm=4 * MOBA_BLOCK)
    slopes = jnp.exp2(-8.0 * jnp.arange(1, heads + 1, dtype=F32) / heads)
    att = _moba_attention(q, k, vt, km, slopes, heads=heads, blk=MOBA_BLOCK, top=MOBA_TOPK, cb=2)
    h = _proj_mlp(att, x.reshape(m, d), attn_w_o[0], norm_mix_post[0], norm_ffn_pre[0],
                  w_up, w_down, norm_ffn_post[0], layer=0, tm=512, tf=1024, wc=512, a_transposed=True)

    nq = mh * dqk
    w_in = mlstm_w_in[0, :, :2 * nq + 2 * d].astype(BF16)
    wg = jnp.pad(mlstm_w_in[0, :, 2 * nq + 2 * d:], ((0, 0), (0, LANES - 2 * mh))).astype(BF16)
    bg = jnp.pad(mlstm_b_gates[0], (0, LANES - 2 * mh)).reshape(1, LANES)
    qt, kmm, vmt, opre, gates = _mlstm_in_proj(h, norm_mix_pre[1], w_in, wg, bg,
                                               bsz=bsz, heads=mh, dqk=dqk, tm=1024)
    y = _mlstm_scan(qt, kmm.reshape(bsz, seq, nq), vmt, opre.reshape(bsz, seq, d),
                    gates.reshape(bsz, seq, LANES), mlstm_norm_h[0], heads=mh, chunk=256)
    h = _proj_mlp(y.reshape(m, d), h, mlstm_w_out[0], norm_mix_post[1],
                  norm_ffn_pre[1], w_up, w_down, norm_ffn_post[1], layer=1, tm=512, tf=1024, wc=512)
    return h.reshape(bsz, seq, d)
```

```python
import functools

import jax
import jax.numpy as jnp
from jax import lax
from jax.experimental import pallas as pl
from jax.experimental.pallas import tpu as pltpu

F32 = jnp.float32
BF16 = jnp.bfloat16

RMS_EPS = 1e-6
ATT_HEADS = 8
MOBA_BLOCK = 256
MOBA_TOPK = 3
MLSTM_HEADS = 8
GATE_SOFTCAP = 15.0

LANES = 128
VMEM_LIMIT = 56 * 1024 * 1024
NEG_BIG = -1e30
POS_BIG = 1e30
LOG2E = 1.4426950408889634
BF16_ROWS = 16
MOBA_DEPTH = 17
MOBA_ROUNDS = 4

_NT = (((1,), (1,)), ((), ()))


def _params(*sem, vmem=VMEM_LIMIT):
    return pltpu.CompilerParams(dimension_semantics=sem, vmem_limit_bytes=vmem)


def _rms_scale(x, gain):
    ms = jnp.mean(x * x, axis=-1, keepdims=True)
    return x * lax.rsqrt(ms + RMS_EPS) * gain


def _qkv_kernel(x_ref, g_ref, w_ref, q_ref, k_ref, km_ref, vt_ref, wq_ref, wk_ref, wvt_ref,
                *, scale, blk, heads):
    d = x_ref.shape[1]

    @pl.when(pl.program_id(0) == 0)
    def _():
        wq_ref[...] = w_ref[:, :d].astype(BF16)
        wk_ref[...] = w_ref[:, d:2 * d].astype(BF16)
        wvt_ref[...] = w_ref[:, 2 * d:].T.astype(BF16)

    xn = _rms_scale(x_ref[...], g_ref[...]).astype(BF16)
    tm = xn.shape[0]
    dh = d // heads
    q = jnp.dot(xn, wq_ref[...], preferred_element_type=F32)
    q_ref[...] = (q * scale).astype(BF16)
    k = jnp.dot(xn, wk_ref[...], preferred_element_type=F32)
    k_ref[...] = k.astype(BF16)
    for c in range(tm // blk):
        km_ref[c] = jnp.mean(k[c * blk:(c + 1) * blk], axis=0, keepdims=True)
    vt = lax.dot_general(wvt_ref[...], xn, _NT, preferred_element_type=F32)
    ones_row = (lax.broadcasted_iota(jnp.int32, (BF16_ROWS, blk), 0) == 0).astype(BF16)
    for h in range(heads):
        for c in range(tm // blk):
            vt_ref[0, h, c, :dh, :] = vt[h * dh:(h + 1) * dh, c * blk:(c + 1) * blk].astype(BF16)
            vt_ref[0, h, c, dh:, :] = ones_row


def _qkv_proj(x, gain, w_qkv, *, heads, blk, tm):
    bsz, seq, d = x.shape
    dh = d // heads
    nblk = seq // blk
    m = bsz * seq
    cpt = tm // blk
    tiles_per_b = seq // tm
    kern = functools.partial(_qkv_kernel, scale=dh ** -0.5 * LOG2E, blk=blk, heads=heads)
    const = lambda i: (0, 0)
    q, k, km, vt = pl.pallas_call(
        kern,
        grid=(m // tm,),
        in_specs=[
            pl.BlockSpec((tm, d), lambda i: (i, 0)),
            pl.BlockSpec((1, d), const),
            pl.BlockSpec((d, 3 * d), const, pipeline_mode=pl.Buffered(1)),
        ],
        out_specs=[
            pl.BlockSpec((tm, d), lambda i: (i, 0)),
            pl.BlockSpec((tm, d), lambda i: (i, 0)),
            pl.BlockSpec((cpt, 1, d), lambda i: (i, 0, 0)),
            pl.BlockSpec((1, heads, cpt, dh + BF16_ROWS, blk),
                         lambda i: (i // tiles_per_b, 0, i % tiles_per_b, 0, 0)),
        ],
        out_shape=[
            jax.ShapeDtypeStruct((m, d), BF16),
            jax.ShapeDtypeStruct((m, d), BF16),
            jax.ShapeDtypeStruct((m // blk, 1, d), F32),
            jax.ShapeDtypeStruct((bsz, heads, nblk, dh + BF16_ROWS, blk), BF16),
        ],
        scratch_shapes=[pltpu.VMEM((d, d), BF16)] * 3,
        compiler_params=_params("arbitrary"),
        name="qkv_proj",
    )(x.reshape(m, d), gain.reshape(1, d), w_qkv)
    return (q.reshape(bsz, seq, d), k.reshape(bsz, seq, d),
            km.reshape(bsz, nblk, d), vt)


def _moba_items(nblk, cb):
    tiles, chunks = [], []
    for i in range(nblk):
        for c in range(-(-(i + 1) // cb)):
            tiles.append(i)
            chunks.append(c)
    while len(tiles) % (MOBA_DEPTH * MOBA_ROUNDS):
        assert nblk >= 2 * cb
        tiles.append(0)
        chunks.append(1)
    n_items = len(tiles)
    tiles += [0] * (MOBA_DEPTH - 1)
    chunks += [0] * (MOBA_DEPTH - 1)
    return n_items, tiles, chunks


def _moba_kernel(slope_ref, tile_ref, chunk_ref, q_ref, k_ref, vt_ref, km_ref, o_ref,
                 sel_ref, bias_ref, m_ref, acc_ref, cmax_ref, *s_refs,
                 blk, nblk, top, cb, n_items, qc, dh):
    slope = slope_ref[pl.program_id(1)] * LOG2E

    kr = lax.broadcasted_iota(jnp.int32, (blk, blk), 0)
    qq = lax.broadcasted_iota(jnp.int32, (blk, blk), 1)
    base = kr.astype(F32) * slope
    bias_ref[0] = base
    bias_ref[1] = jnp.where(kr <= qq, base, NEG_BIG)

    m_ref[...] = jnp.full(m_ref.shape, NEG_BIG, F32)
    acc_ref[...] = jnp.zeros(acc_ref.shape, F32)

    km = km_ref[0]
    km_hi = km.astype(BF16)
    km_lo = (km - km_hi.astype(F32)).astype(BF16)
    row = lax.broadcasted_iota(jnp.int32, (nblk, qc), 0)
    rowf = row.astype(F32)
    tiles_per_qc = qc // blk

    def select(u, carry):
        q0 = pl.multiple_of(u * qc, qc)
        qs = q_ref[0, pl.ds(q0, qc), :]
        gate = (lax.dot_general(km_hi, qs, _NT, preferred_element_type=F32)
                + lax.dot_general(km_lo, qs, _NT, preferred_element_type=F32))
        qblk = u * tiles_per_qc + lax.broadcasted_iota(jnp.int32, (nblk, qc), 1) // blk
        past = row < qblk
        g = jnp.where(past, gate, -jnp.inf)
        picked = jnp.zeros((nblk, qc), F32)
        for _ in range(top):
            mx = jnp.max(g, axis=0, keepdims=True)
            first = jnp.min(jnp.where(g == mx, rowf, float(nblk)), axis=0, keepdims=True)
            hit = rowf == first
            picked = jnp.where(hit, 1.0, picked)
            g = jnp.where(hit, -jnp.inf, g)
        selv = jnp.where(((picked > 0.5) & past) | (row == qblk), 1.0, 0.0)
        for t in range(tiles_per_qc):
            sel_ref[u * tiles_per_qc + t] = selv[:, t * blk:(t + 1) * blk]
        return carry

    lax.fori_loop(0, (nblk * blk) // qc, select, 0, unroll=True)

    def scores(item, slot):
        i = tile_ref[item]
        c = chunk_ref[item]
        qi = q_ref[0, pl.ds(pl.multiple_of(i * blk, blk), blk), :]
        for jb in range(cb):
            j = c * cb + jb
            kb = k_ref[0, pl.ds(pl.multiple_of(j * blk, blk), blk), :]
            sb = (lax.dot_general(kb, qi, _NT, preferred_element_type=F32)
                  + bias_ref[jnp.where(j == i, 1, 0)])
            s_refs[slot][jb * blk:(jb + 1) * blk, :] = sb
            cmax_ref[slot, jb] = jnp.max(sb, axis=0, keepdims=True)

    def softmax_pv(item, slot):
        i = tile_ref[item]
        c = chunk_ref[item]
        m_old = m_ref[i]
        sels, cjs = [], []
        m_new = m_old
        for jb in range(cb):
            j = c * cb + jb
            cj = ((j - i) * blk).astype(F32) * slope
            sel = sel_ref[i, pl.ds(j, 1), :] > 0.5
            m_new = jnp.maximum(m_new, jnp.where(sel, cmax_ref[slot, jb] + cj, NEG_BIG))
            sels.append(sel)
            cjs.append(cj)
        alpha = jnp.exp2(m_old - m_new)
        pv = None
        for jb in range(cb):
            shift = jnp.where(sels[jb], m_new - cjs[jb], POS_BIG)
            p = jnp.exp2(s_refs[slot][jb * blk:(jb + 1) * blk, :] - shift)
            d = jnp.dot(vt_ref[0, 0, c * cb + jb], p.astype(BF16), preferred_element_type=F32)
            pv = d if pv is None else pv + d
        acc_ref[i] = alpha * acc_ref[i] + pv
        m_ref[i] = m_new

    for slot in range(MOBA_DEPTH - 1):
        scores(slot, slot)

    def group(kk, carry):
        for u in range(MOBA_DEPTH * MOBA_ROUNDS):
            t = MOBA_DEPTH * MOBA_ROUNDS * kk + u
            softmax_pv(t, u % MOBA_DEPTH)
            scores(t + MOBA_DEPTH - 1, (u + MOBA_DEPTH - 1) % MOBA_DEPTH)
        return carry

    lax.fori_loop(0, n_items // (MOBA_DEPTH * MOBA_ROUNDS), group, 0)

    def finish(i, carry):
        acc = acc_ref[i]
        o = acc[:dh] * (1.0 / acc[dh:dh + 1])
        o_ref[0, pl.ds(pl.multiple_of(i * blk, blk), blk), :] = o.T.astype(BF16)
        return carry

    lax.fori_loop(0, nblk, finish, 0, unroll=True)


def _moba_attention(q, k, vt, km, slopes, *, heads, blk, top, cb):
    bsz, seq, d = q.shape
    dh = d // heads
    nblk = seq // blk
    assert nblk % cb == 0
    qc = min(seq, 4 * blk)
    n_items, tiles, chunks = _moba_items(nblk, cb)
    kern = functools.partial(_moba_kernel, blk=blk, nblk=nblk, top=min(top, nblk), cb=cb,
                             n_items=n_items, qc=qc, dh=dh)
    smem = pl.BlockSpec(memory_space=pltpu.SMEM)
    return pl.pallas_call(
        kern,
        grid=(bsz, heads),
        in_specs=[
            smem, smem, smem,
            pl.BlockSpec((1, seq, dh), lambda b, h: (b, 0, h)),
            pl.BlockSpec((1, seq, dh), lambda b, h: (b, 0, h)),
            pl.BlockSpec((1, 1, nblk, dh + BF16_ROWS, blk), lambda b, h: (b, h, 0, 0, 0)),
            pl.BlockSpec((1, nblk, dh), lambda b, h: (b, 0, h)),
        ],
        out_specs=pl.BlockSpec((1, seq, dh), lambda b, h: (b, 0, h)),
        out_shape=jax.ShapeDtypeStruct((bsz, seq, d), BF16),
        scratch_shapes=[
            pltpu.VMEM((nblk, nblk, blk), F32),
            pltpu.VMEM((2, blk, blk), F32),
            pltpu.VMEM((nblk, 1, blk), F32),
            pltpu.VMEM((nblk, dh + BF16_ROWS, blk), F32),
            pltpu.VMEM((MOBA_DEPTH, cb, 1, blk), F32),
        ] + [pltpu.VMEM((cb * blk, blk), F32)] * MOBA_DEPTH,
        compiler_params=_params("parallel", "parallel"),
        name="moba_attention",
    )(slopes, jnp.asarray(tiles, jnp.int32), jnp.asarray(chunks, jnp.int32), q, k, vt, km)


def _proj_mlp_kernel(a_ref, h_ref, wo_ref, gm_ref, g1_ref, wu_ref, wd_ref, g2_ref, o_ref,
                     wu_s, wd_s, h1_s, xn_s, acc_s, *, tf, ncast):
    s = pl.program_id(0)
    wc = wu_ref.shape[1]

    def mixer_residual():
        u = jnp.dot(a_ref[...], wo_ref[...].astype(BF16), preferred_element_type=F32)
        return h_ref[...] + _rms_scale(u, gm_ref[...])

    def mlp_chunk(xn, wu, wd):
        t = jnp.maximum(jnp.dot(xn, wu, preferred_element_type=F32), 0.0)
        return jnp.dot((t * t).astype(BF16), wd, preferred_element_type=F32)

    @pl.when(s == 0)
    def _():
        h1 = mixer_residual()
        h1_s[...] = h1
        xn_s[...] = _rms_scale(h1, g1_ref[...]).astype(BF16)
        acc_s[...] = jnp.zeros_like(acc_s)

    @pl.when(s < ncast)
    def _():
        c0 = pl.multiple_of(s * wc, wc)
        wu = wu_ref[...].astype(BF16)
        wd = wd_ref[...].astype(BF16)
        wu_s[:, pl.ds(c0, wc)] = wu
        wd_s[pl.ds(c0, wc), :] = wd
        acc_s[...] += mlp_chunk(xn_s[...], wu, wd)

    @pl.when(s == ncast - 1)
    def _():
        o_ref[...] = h1_s[...] + _rms_scale(acc_s[...], g2_ref[...])

    @pl.when(s >= ncast)
    def _():
        h1 = mixer_residual()
        xn = _rms_scale(h1, g1_ref[...]).astype(BF16)
        ff = wu_s.shape[1]
        acc = None
        for c in range(ff // tf):
            d = mlp_chunk(xn, wu_s[:, c * tf:(c + 1) * tf], wd_s[c * tf:(c + 1) * tf, :])
            acc = d if acc is None else acc + d
        o_ref[...] = h1 + _rms_scale(acc, g2_ref[...])


def _proj_mlp(a, h, w_o, g_mix, g_pre, w_up, w_down, g_post, *, layer, tm, tf, wc):
    m, d = h.shape
    kdim = a.shape[1]
    ff = w_up.shape[2]
    ncast = ff // wc
    row = lambda s: (jnp.maximum(s - (ncast - 1), 0), 0)
    const = lambda s: (0, 0)
    return pl.pallas_call(
        functools.partial(_proj_mlp_kernel, tf=tf, ncast=ncast),
        grid=(ncast - 1 + m // tm,),
        in_specs=[
            pl.BlockSpec((tm, kdim), row),
            pl.BlockSpec((tm, d), row),
            pl.BlockSpec((kdim, d), const, pipeline_mode=pl.Buffered(1)),
            pl.BlockSpec((1, d), const),
            pl.BlockSpec((1, d), const),
            pl.BlockSpec((None, d, wc), lambda s: (layer, 0, jnp.minimum(s, ncast - 1))),
            pl.BlockSpec((None, wc, d), lambda s: (layer, jnp.minimum(s, ncast - 1), 0)),
            pl.BlockSpec((1, d), const),
        ],
        out_specs=pl.BlockSpec((tm, d), row),
        out_shape=jax.ShapeDtypeStruct((m, d), F32),
        scratch_shapes=[
            pltpu.VMEM((d, ff), BF16),
            pltpu.VMEM((ff, d), BF16),
            pltpu.VMEM((tm, d), F32),
            pltpu.VMEM((tm, d), BF16),
            pltpu.VMEM((tm, d), F32),
        ],
        compiler_params=_params("arbitrary"),
        name="proj_mlp",
    )(a, h, w_o, g_mix.reshape(1, d), g_pre.reshape(1, d), w_up, w_down, g_post.reshape(1, d))


def _mlstm_in_kernel(x_ref, g_ref, w_ref, wg_ref, bg_ref,
                     qt_ref, k_ref, vt_ref, op_ref, gates_ref,
                     wqt_ref, wvt_ref, *, heads, kscale):
    nq = wqt_ref.shape[0]
    d = wvt_ref.shape[0]

    @pl.when(pl.program_id(0) == 0)
    def _():
        wqt_ref[...] = w_ref[:, :nq].astype(F32).T.astype(BF16)
        wvt_ref[...] = w_ref[:, 2 * nq:2 * nq + d].astype(F32).T.astype(BF16)

    xn = _rms_scale(x_ref[...], g_ref[...]).astype(BF16)
    qt_ref[0] = lax.dot_general(wqt_ref[...], xn, _NT, preferred_element_type=F32).astype(BF16)
    k_ref[...] = (jnp.dot(xn, w_ref[:, nq:2 * nq], preferred_element_type=F32) * kscale).astype(BF16)
    vt_ref[0] = lax.dot_general(wvt_ref[...], xn, _NT, preferred_element_type=F32).astype(BF16)
    op_ref[...] = jnp.dot(xn, w_ref[:, 2 * nq + d:2 * nq + 2 * d],
                          preferred_element_type=F32).astype(BF16)
    z = jnp.dot(xn, wg_ref[...], preferred_element_type=F32) + bg_ref[...]
    z = GATE_SOFTCAP * jnp.tanh(z * (1.0 / GATE_SOFTCAP))
    log_f = jnp.minimum(z, 0.0) - jnp.log1p(jnp.exp(-jnp.abs(z)))
    lane = lax.broadcasted_iota(jnp.int32, z.shape, 1)
    gates_ref[...] = jnp.where(lane < heads, z, log_f) * LOG2E


def _mlstm_in_proj(x, gain, w_in, wg, bg, *, bsz, heads, dqk, tm):
    m, d = x.shape
    seq = m // bsz
    nq = heads * dqk
    dv_all = d
    tiles_per_b = seq // tm
    kern = functools.partial(_mlstm_in_kernel, heads=heads, kscale=dqk ** -0.5)
    const = lambda i: (0, 0)
    row = lambda i: (i, 0)
    tcol = lambda i: (i // tiles_per_b, 0, i % tiles_per_b)
    return pl.pallas_call(
        kern,
        grid=(m // tm,),
        in_specs=[
            pl.BlockSpec((tm, d), row),
            pl.BlockSpec((1, d), const),
            pl.BlockSpec(w_in.shape, const, pipeline_mode=pl.Buffered(1)),
            pl.BlockSpec((d, LANES), const),
            pl.BlockSpec((1, LANES), const),
        ],
        out_specs=[
            pl.BlockSpec((1, nq, tm), tcol),
            pl.BlockSpec((tm, nq), row),
            pl.BlockSpec((1, dv_all, tm), tcol),
            pl.BlockSpec((tm, dv_all), row),
            pl.BlockSpec((tm, LANES), row),
        ],
        out_shape=[
            jax.ShapeDtypeStruct((bsz, nq, seq), BF16),
            jax.ShapeDtypeStruct((m, nq), BF16),
            jax.ShapeDtypeStruct((bsz, dv_all, seq), BF16),
            jax.ShapeDtypeStruct((m, dv_all), BF16),
            jax.ShapeDtypeStruct((m, LANES), F32),
        ],
        scratch_shapes=[
            pltpu.VMEM((nq, d), BF16),
            pltpu.VMEM((dv_all, d), BF16),
        ],
        compiler_params=_params("arbitrary"),
        name="mlstm_in_proj",
    )(x, gain.reshape(1, d), w_in, wg, bg)


def _mlstm_kernel(qt_ref, k_ref, vt_ref, op_ref, g_ref, gnext_ref, nh_ref, y_ref,
                  c_ref, m_ref, rows_ref, ucol_ref, *, heads, dqk, dv, chunk):
    L = chunk
    ss = lax.broadcasted_iota(jnp.int32, (L, L), 0)
    tt = lax.broadcasted_iota(jnp.int32, (L, L), 1)
    causal = ss <= tt

    def gate_terms(gates):
        gates_t = gates.T
        i_rows = gates_t[0:heads]
        f_rows = gates_t[heads:2 * heads]
        tri = causal.astype(BF16)
        f_hi = f_rows.astype(BF16)
        f_r1 = f_rows - f_hi.astype(F32)
        f_mid = f_r1.astype(BF16)
        f_lo = (f_r1 - f_mid.astype(F32)).astype(BF16)
        b_rows = (jnp.dot(f_hi, tri, preferred_element_type=F32)
                  + jnp.dot(f_mid, tri, preferred_element_type=F32)
                  + jnp.dot(f_lo, tri, preferred_element_type=F32))
        rows_ref[0:heads] = i_rows
        rows_ref[heads:2 * heads] = b_rows
        ucol_ref[...] = jnp.concatenate(
            [i_rows - b_rows, jnp.zeros((LANES - heads, L), F32)], axis=0).T

    @pl.when(pl.program_id(1) == 0)
    def _():
        c_ref[...] = jnp.zeros_like(c_ref)
        m_ref[...] = jnp.zeros_like(m_ref)
        gate_terms(g_ref[0])

    ones_rows = (lax.broadcasted_iota(jnp.int32, (BF16_ROWS, L), 0) == 0).astype(BF16)
    sub = lax.broadcasted_iota(jnp.int32, (2 * dqk, 1), 0)
    lane = lax.broadcasted_iota(jnp.int32, (1, 2 * dqk), 1)

    first = lane < dqk
    group = heads // 2
    for h0 in range(0, heads, group):
        hs = range(h0, h0 + group)
        kgs, cts, s_qk, inter = [], [], [], []
        for h in hs:
            p, half = divmod(h, 2)
            qt_pair = qt_ref[0, p * 2 * dqk:(p + 1) * 2 * dqk, :]
            kg = k_ref[0, :, p * 2 * dqk:(p + 1) * 2 * dqk]
            in_head = (sub >= half * dqk) & (sub < (half + 1) * dqk)
            qt_h = jnp.where(in_head, qt_pair, jnp.zeros_like(qt_pair))
            ct = c_ref[p]
            kgs.append(kg)
            cts.append(ct)
            s_qk.append(jnp.dot(kg, qt_h, preferred_element_type=F32))
            inter.append(jnp.dot(ct.astype(BF16), qt_h, preferred_element_type=F32))

        s_ts, m_ts, w_inters, vexts = [], [], [], []
        for n, h in enumerate(hs):
            b_row = rows_ref[heads + h:heads + h + 1, :]
            m_prev = m_ref[h][:, 0:1]
            log_d = jnp.where(causal, ucol_ref[:, h:h + 1] + b_row, -jnp.inf)
            log_inter = b_row + m_prev
            m_t = jnp.maximum(log_inter, jnp.max(log_d, axis=0, keepdims=True))
            s_ts.append((s_qk[n] * jnp.exp2(log_d - m_t)).astype(BF16))
            w_inters.append(jnp.exp2(log_inter - m_t))
            m_ts.append(m_t)
            vexts.append(jnp.concatenate([vt_ref[0, h * dv:(h + 1) * dv, :], ones_rows], axis=0))

        for n, h in enumerate(hs):
            num = (w_inters[n] * inter[n]
                   + jnp.dot(vexts[n], s_ts[n], preferred_element_type=F32))
            den = num[dv:dv + 1]
            hout = num[:dv] * (1.0 / jnp.maximum(jnp.abs(den), jnp.exp2(-m_ts[n])))
            hn = hout * lax.rsqrt(jnp.mean(hout * hout, axis=0, keepdims=True) + RMS_EPS)
            og = 0.5 * jnp.tanh(0.5 * op_ref[0, :, h * dv:(h + 1) * dv].astype(F32)) + 0.5
            y_ref[0, :, h * dv:(h + 1) * dv] = (
                og * (hn.T * nh_ref[:, h * dv:(h + 1) * dv])).astype(BF16)

        upds, decays = [], []
        for n, h in enumerate(hs):
            b_row = rows_ref[heads + h:heads + h + 1, :]
            i_row = rows_ref[h:h + 1, :]
            m_prev = m_ref[h][:, 0:1]
            b_end = b_row[:, L - 1:L]
            log_w = b_end - b_row + i_row
            m_new = jnp.maximum(b_end + m_prev, jnp.max(log_w, axis=1, keepdims=True))
            decays.append(jnp.exp2(b_end + m_prev - m_new))
            vw = (vexts[n].astype(F32) * jnp.exp2(log_w - m_new)).astype(BF16)
            upds.append(jnp.dot(vw, kgs[n], preferred_element_type=F32))
            m_ref[h] = jnp.broadcast_to(m_new, (1, LANES))
        for n in range(0, group, 2):
            c_ref[(h0 + n) // 2] = (jnp.where(first, decays[n], decays[n + 1]) * cts[n]
                                    + jnp.where(first, upds[n], upds[n + 1]))

    gate_terms(gnext_ref[0])


def _mlstm_scan(qt, k, vt, opre, gates, norm_h, *, heads, chunk):
    bsz, dv_all, seq = vt.shape
    dv = dv_all // heads
    dqk = qt.shape[1] // heads
    kern = functools.partial(_mlstm_kernel, heads=heads, dqk=dqk, dv=dv, chunk=chunk)
    nchunk = seq // chunk
    rows = lambda b, c: (b, c, 0)
    cols = lambda b, c: (b, 0, c)
    return pl.pallas_call(
        kern,
        grid=(bsz, seq // chunk),
        in_specs=[
            pl.BlockSpec((1, heads * dqk, chunk), cols),
            pl.BlockSpec((1, chunk, heads * dqk), rows),
            pl.BlockSpec((1, dv_all, chunk), cols),
            pl.BlockSpec((1, chunk, dv_all), rows),
            pl.BlockSpec((1, chunk, LANES), rows),
            pl.BlockSpec((1, chunk, LANES), lambda b, c: (b, jnp.minimum(c + 1, nchunk - 1), 0)),
            pl.BlockSpec((1, dv_all), lambda b, c: (0, 0)),
        ],
        out_specs=pl.BlockSpec((1, chunk, dv_all), rows),
        out_shape=jax.ShapeDtypeStruct((bsz, seq, dv_all), BF16),
        scratch_shapes=[
            pltpu.VMEM((heads // 2, dv + BF16_ROWS, 2 * dqk), F32),
            pltpu.VMEM((heads, 1, LANES), F32),
            pltpu.VMEM((2 * heads, chunk), F32),
            pltpu.VMEM((chunk, LANES), F32),
        ],
        compiler_params=_params("parallel", "arbitrary", vmem=VMEM_LIMIT // 4),
        name="mlstm_scan",
    )(qt, k, vt, opre, gates, gates, norm_h.reshape(1, dv_all))


def kernel(x, norm_mix_pre, norm_mix_post, norm_ffn_pre, norm_ffn_post, w_up, w_down,
           attn_w_qkv, attn_w_o, mlstm_w_in, mlstm_b_gates, mlstm_norm_h, mlstm_w_out):
    bsz, seq, d = x.shape
    m = bsz * seq
    heads = ATT_HEADS
    mh = MLSTM_HEADS
    dv = d // mh
    dqk = dv // 2

    q, k, km, vt = _qkv_proj(x, norm_mix_pre[0], attn_w_qkv[0], heads=heads, blk=MOBA_BLOCK,
                             tm=4 * MOBA_BLOCK)
    slopes = jnp.exp2(-8.0 * jnp.arange(1, heads + 1, dtype=F32) / heads)
    att = _moba_attention(q, k, vt, km, slopes, heads=heads, blk=MOBA_BLOCK, top=MOBA_TOPK, cb=2)
    h = _proj_mlp(att.reshape(m, d), x.reshape(m, d), attn_w_o[0], norm_mix_post[0],
                  norm_ffn_pre[0], w_up, w_down, norm_ffn_post[0], layer=0, tm=512, tf=1024, wc=512)

    nq = mh * dqk
    w_in = mlstm_w_in[0, :, :2 * nq + 2 * d].astype(BF16)
    wg = jnp.pad(mlstm_w_in[0, :, 2 * nq + 2 * d:], ((0, 0), (0, LANES - 2 * mh))).astype(BF16)
    bg = jnp.pad(mlstm_b_gates[0], (0, LANES - 2 * mh)).reshape(1, LANES)
    qt, kmm, vmt, opre, gates = _mlstm_in_proj(h, norm_mix_pre[1], w_in, wg, bg,
                                               bsz=bsz, heads=mh, dqk=dqk, tm=1024)
    y = _mlstm_scan(qt, kmm.reshape(bsz, seq, nq), vmt, opre.reshape(bsz, seq, d),
                    gates.reshape(bsz, seq, LANES), mlstm_norm_h[0], heads=mh, chunk=256)
    h = _proj_mlp(y.reshape(m, d), h, mlstm_w_out[0], norm_mix_post[1],
                  norm_ffn_pre[1], w_up, w_down, norm_ffn_post[1], layer=1, tm=512, tf=1024, wc=512)
    return h.reshape(bsz, seq, d)
```
